```python
import math
import jax, jax.numpy as jnp
from jax import lax
import numpy as np

D_MODEL = 1024
BATCH = 32
SEQ = 2048
DEPTH = 1

CHUNK = 64
SSM_GROUP = 16
D_SSM = D_MODEL // 4
SSM_GROUPS = D_SSM // SSM_GROUP
SSM_STATE = 64
DT_MIN = 0.001
DT_MAX = 0.1
D_POOL = D_MODEL // 2
POOL_WINDOWS = (2, 4, 8, 16)
POOL_GROUPS = len(POOL_WINDOWS)
POOL_GC = D_POOL // POOL_GROUPS
POOL_MAX = max(POOL_WINDOWS)
D_IN = D_SSM + D_POOL + 2 * D_MODEL
N_EXPERT_GROUPS = 4
EXPERTS_PER_GROUP = 8
N_EXPERTS = N_EXPERT_GROUPS * EXPERTS_PER_GROUP
TOP_K_FINE = 2
D_EXPERT = D_MODEL // 2
N_MOD = 6
EPS = 1e-6

kernel_name = "hybrid_s5_pool_hiermoe_block"


def rms_norm(x, g):
    xf = x.astype(jnp.float32)
    y = xf * lax.rsqrt(jnp.mean(xf * xf, axis=-1, keepdims=True) + EPS)
    return (y * g.astype(jnp.float32)).astype(x.dtype)


def modulate(h, g, shift, scale):
    return rms_norm(h, g) * (1.0 + scale[:, None, :]) + shift[:, None, :]


def s5_mixer(u, lam_re, lam_im, log_dt, b_re, b_im, c_re, c_im, d_skip, w_glu, b_glu):
    bsz, seq, _ = u.shape
    uf = u.astype(jnp.float32).reshape(bsz, seq, SSM_GROUPS, SSM_GROUP)
    lr = lam_re.astype(jnp.float32)
    li = lam_im.astype(jnp.float32)
    dt = jnp.exp(log_dt.astype(jnp.float32))[:, None]
    mag = jnp.exp(lr * dt)
    lb_re = mag * jnp.cos(li * dt)
    lb_im = mag * jnp.sin(li * dt)
    den = lr * lr + li * li
    nr = lb_re - 1.0
    f_re = (nr * lr + lb_im * li) / den
    f_im = (lb_im * lr - nr * li) / den
    br = b_re.astype(jnp.float32)
    bi = b_im.astype(jnp.float32)
    bb_re = f_re[..., None] * br - f_im[..., None] * bi
    bb_im = f_re[..., None] * bi + f_im[..., None] * br
    bu_re = jnp.einsum('bsgh,gph->bsgp', uf, bb_re)
    bu_im = jnp.einsum('bsgh,gph->bsgp', uf, bb_im)
    a_re = jnp.broadcast_to(lb_re, (1, seq, SSM_GROUPS, SSM_STATE))
    a_im = jnp.broadcast_to(lb_im, (1, seq, SSM_GROUPS, SSM_STATE))

    def combine(left, right):
        ar_l, ai_l, xr_l, xi_l = left
        ar_r, ai_r, xr_r, xi_r = right
        ar = ar_r * ar_l - ai_r * ai_l
        ai = ar_r * ai_l + ai_r * ar_l
        xr = ar_r * xr_l - ai_r * xi_l + xr_r
        xi = ar_r * xi_l + ai_r * xr_l + xi_r
        return (ar, ai, xr, xi)

    _, _, st_re, st_im = lax.associative_scan(combine, (a_re, a_im, bu_re, bu_im), axis=1)
    y = (jnp.einsum('bsgp,ghp->bsgh', st_re, c_re.astype(jnp.float32))
         - jnp.einsum('bsgp,ghp->bsgh', st_im, c_im.astype(jnp.float32))
         + d_skip.astype(jnp.float32).reshape(SSM_GROUPS, SSM_GROUP) * uf)
    y = jax.nn.gelu(y.reshape(bsz, seq, D_SSM))
    z = y * jax.nn.sigmoid(y @ w_glu.astype(jnp.float32) + b_glu.astype(jnp.float32))
    return z.astype(u.dtype)


def pool_mixer(u, w_pool, pool_scale):
    bsz, seq, _ = u.shape
    uf = u.astype(jnp.float32)
    cs = jnp.pad(jnp.cumsum(uf, axis=1), ((0, 0), (POOL_MAX, 0), (0, 0)))
    pos = jnp.arange(seq, dtype=jnp.float32)
    outs = []
    for gi, w in enumerate(POOL_WINDOWS):
        lo, hi = gi * POOL_GC, (gi + 1) * POOL_GC
        win = cs[:, POOL_MAX:POOL_MAX + seq, lo:hi] - cs[:, POOL_MAX - w:POOL_MAX - w + seq, lo:hi]
        cnt = jnp.minimum(pos + 1.0, float(w))[None, :, None]
        outs.append(win / cnt - uf[:, :, lo:hi])
    m = jnp.stack(outs, axis=2)
    y = jnp.einsum('bsgc,gcd->bsgd', m, w_pool.astype(jnp.float32)).reshape(bsz, seq, D_POOL)
    return (y * pool_scale.astype(jnp.float32)).astype(u.dtype)


def hier_moe(u, w_coarse, b_coarse, w_fine, b_fine, w_gate, w_up, w_down):
    bsz, seq, d = u.shape
    n_tok = bsz * seq
    xf = u.reshape(n_tok, d)
    x32 = xf.astype(jnp.float32)
    coarse = x32 @ w_coarse.astype(jnp.float32) + b_coarse.astype(jnp.float32)
    p_coarse = jax.nn.softmax(coarse, axis=-1)
    _, grp = lax.top_k(coarse, 1)
    p_grp = jnp.take_along_axis(p_coarse, grp, axis=-1)
    fine = (x32 @ w_fine.astype(jnp.float32) + b_fine.astype(jnp.float32)).reshape(
        n_tok, N_EXPERT_GROUPS, EXPERTS_PER_GROUP)
    fine_sel = jnp.take_along_axis(fine, grp[:, :, None], axis=1)[:, 0, :]
    p_fine = jax.nn.softmax(fine_sel, axis=-1)
    top_p, top_i = lax.top_k(p_fine, TOP_K_FINE)
    weights = p_grp * top_p / jnp.sum(top_p, axis=-1, keepdims=True)
    expert = grp * EXPERTS_PER_GROUP + top_i

    e_flat = expert.reshape(-1)
    w_flat = weights.reshape(-1)
    tok_flat = jnp.repeat(jnp.arange(n_tok, dtype=jnp.int32), TOP_K_FINE)
    order = jnp.argsort(e_flat)
    tok_sorted = tok_flat[order]
    w_sorted = w_flat[order]
    group_sizes = jnp.bincount(e_flat, length=N_EXPERTS).astype(jnp.int32)
    xs = xf[tok_sorted]
    h1 = lax.ragged_dot(xs, w_gate.astype(xs.dtype), group_sizes)
    h2 = lax.ragged_dot(xs, w_up.astype(xs.dtype), group_sizes)
    act = jax.nn.silu(h1) * h2
    ys = lax.ragged_dot(act, w_down.astype(act.dtype), group_sizes)
    ys = ys * w_sorted[:, None].astype(ys.dtype)
    out = jax.ops.segment_sum(ys, tok_sorted, num_segments=n_tok)
    return out.reshape(bsz, seq, d).astype(u.dtype)


def setup_inputs(seed: int = 0) -> dict:
    key = jax.random.key(seed)
    ks = jax.random.split(key, 32)
    f32 = jnp.float32
    L, D = DEPTH, D_MODEL
    nrm = lambda k, shape, s: jax.random.normal(k, shape, f32) * s
    n_idx = jnp.arange(SSM_STATE, dtype=f32)
    lam_re = -0.5 + 0.01 * jax.random.normal(ks[6], (L, SSM_GROUPS, SSM_STATE), f32)
    lam_im = math.pi * n_idx[None, None, :] + 0.01 * jax.random.normal(ks[7], (L, SSM_GROUPS, SSM_STATE), f32)
    log_dt = jax.random.uniform(ks[8], (L, SSM_GROUPS), f32, math.log(DT_MIN), math.log(DT_MAX))
    return {
        "x": nrm(ks[0], (BATCH, SEQ, D), 1.0),
        "c": nrm(ks[1], (BATCH, D), 1.0),
        "w_mod": nrm(ks[2], (L, D, N_MOD * D), D ** -0.5),
        "b_mod": nrm(ks[3], (L, N_MOD * D), 0.02),
        "norm1_g": 1.0 + nrm(ks[4], (L, D), 0.05),
        "w_in": nrm(ks[5], (L, D, D_IN), D ** -0.5),
        "b_in": nrm(ks[9], (L, D_IN), 0.02),
        "ssm_lam_re": lam_re,
        "ssm_lam_im": lam_im,
        "ssm_log_dt": log_dt,
        "ssm_b_re": nrm(ks[10], (L, SSM_GROUPS, SSM_STATE, SSM_GROUP), (2 * SSM_GROUP) ** -0.5),
        "ssm_b_im": nrm(ks[11], (L, SSM_GROUPS, SSM_STATE, SSM_GROUP), (2 * SSM_GROUP) ** -0.5),
        "ssm_c_re": nrm(ks[12], (L, SSM_GROUPS, SSM_GROUP, SSM_STATE), SSM_STATE ** -0.5),
        "ssm_c_im": nrm(ks[13], (L, SSM_GROUPS, SSM_GROUP, SSM_STATE), SSM_STATE ** -0.5),
        "ssm_d": nrm(ks[14], (L, D_SSM), 1.0),
        "ssm_w_glu": nrm(ks[15], (L, D_SSM, D_SSM), D_SSM ** -0.5),
        "ssm_b_glu": nrm(ks[16], (L, D_SSM), 0.02),
        "pool_w": nrm(ks[17], (L, POOL_GROUPS, POOL_GC, POOL_GC), POOL_GC ** -0.5),
        "pool_scale": 1.0 + nrm(ks[18], (L, D_POOL), 0.1),
        "w_proj_ssm": nrm(ks[19], (L, D_SSM, D), D_SSM ** -0.5),
        "w_proj_pool": nrm(ks[20], (L, D_POOL, D), D_POOL ** -0.5),
        "w_out": nrm(ks[21], (L, D, D), D ** -0.5),
        "norm2_g": 1.0 + nrm(ks[22], (L, D), 0.05),
        "router_coarse_w": nrm(ks[23], (L, D, N_EXPERT_GROUPS), D ** -0.5),
        "router_coarse_b": nrm(ks[24], (L, N_EXPERT_GROUPS), 0.01),
        "router_fine_w": nrm(ks[25], (L, D, N_EXPERTS), D ** -0.5),
        "router_fine_b": nrm(ks[26], (L, N_EXPERTS), 0.01),
        "moe_w_gate": nrm(ks[27], (L, N_EXPERTS, D, D_EXPERT), D ** -0.5),
        "moe_w_up": nrm(ks[28], (L, N_EXPERTS, D, D_EXPERT), D ** -0.5),
        "moe_w_down": nrm(ks[29], (L, N_EXPERTS, D_EXPERT, D), D_EXPERT ** -0.5),
        "norm_f_g": 1.0 + nrm(ks[30], (D,), 0.05),
    }


def reference(x, c, w_mod, b_mod, norm1_g, w_in, b_in, ssm_lam_re, ssm_lam_im, ssm_log_dt,
              ssm_b_re, ssm_b_im, ssm_c_re, ssm_c_im, ssm_d, ssm_w_glu, ssm_b_glu,
              pool_w, pool_scale, w_proj_ssm, w_proj_pool, w_out, norm2_g,
              router_coarse_w, router_coarse_b, router_fine_w, router_fine_b,
              moe_w_gate, moe_w_up, moe_w_down, norm_f_g):
    h = x
    c_act = jax.nn.silu(c)
    for l in range(DEPTH):
        mod = c_act @ w_mod[l] + b_mod[l]
        shift1, scale1, gate1, shift2, scale2, gate2 = jnp.split(mod, N_MOD, axis=-1)

        u = modulate(h, norm1_g[l], shift1, scale1)
        proj = u @ w_in[l] + b_in[l]
        u_ssm = proj[..., :D_SSM]
        u_pool = proj[..., D_SSM:D_SSM + D_POOL]
        g_ssm = jax.nn.sigmoid(proj[..., D_SSM + D_POOL:D_SSM + D_POOL + D_MODEL])
        g_pool = jax.nn.sigmoid(proj[..., D_SSM + D_POOL + D_MODEL:])
        a = s5_mixer(u_ssm, ssm_lam_re[l], ssm_lam_im[l], ssm_log_dt[l], ssm_b_re[l], ssm_b_im[l],
                     ssm_c_re[l], ssm_c_im[l], ssm_d[l], ssm_w_glu[l], ssm_b_glu[l])
        b = pool_mixer(u_pool, pool_w[l], pool_scale[l])
        merged = g_ssm * (a @ w_proj_ssm[l]) + g_pool * (b @ w_proj_pool[l])
        h = h + gate1[:, None, :] * (merged @ w_out[l])

        u2 = modulate(h, norm2_g[l], shift2, scale2)
        y = hier_moe(u2, router_coarse_w[l], router_coarse_b[l], router_fine_w[l], router_fine_b[l],
                     moe_w_gate[l], moe_w_up[l], moe_w_down[l])
        h = h + gate2[:, None, :] * y
    return rms_norm(h, norm_f_g)
```

```python
import functools
import math

import jax
import jax.numpy as jnp
from jax import lax
from jax.experimental import pallas as pl
from jax.experimental.pallas import tpu as pltpu

EPS = 1e-6
POOL_WINDOWS = (2, 4, 8, 16)
TOP_K_FINE = 2
N_MOD = 6
LANES = 128
SSM_SLABS = 8
ROUTE_LANES = 128
VMEM_LIMIT = 56 * 1024 * 1024
NEG = -1e30

f32 = jnp.float32
bf16 = jnp.bfloat16


def _cparams(sem):
    return pltpu.CompilerParams(dimension_semantics=sem, vmem_limit_bytes=VMEM_LIMIT)


def _dot(a, b):
    return jnp.dot(a, b, preferred_element_type=f32)


def _mod_kernel(c_ref, w_ref, b_ref, o_ref):
    c = c_ref[...]
    ca = c * jax.nn.sigmoid(c)
    o_ref[...] = jnp.dot(ca, w_ref[...], preferred_element_type=f32,
                         precision=lax.Precision.HIGHEST) + b_ref[...]


def _mod_call(c, w_mod, b_mod):
    bsz, d = c.shape
    n = w_mod.shape[1]
    tn = d
    return pl.pallas_call(
        _mod_kernel,
        grid=(n // tn,),
        in_specs=[pl.BlockSpec((bsz, d), lambda j: (0, 0)),
                  pl.BlockSpec((d, tn), lambda j: (0, j)),
                  pl.BlockSpec((1, tn), lambda j: (0, j))],
        out_specs=pl.BlockSpec((bsz, tn), lambda j: (0, j)),
        out_shape=jax.ShapeDtypeStruct((bsz, n), f32),
        compiler_params=_cparams(("parallel",)),
        name="mod",
    )(c, w_mod, b_mod.reshape(1, n))


def _ssm_prep_kernel(lr_ref, li_ref, ldt_ref, btr_ref, bti_ref, pw_ref, bbr_ref, bbi_ref,
                     *, n_pow, n_grp):
    lr = lr_ref[...]
    li = li_ref[...]
    dt = jnp.exp(ldt_ref[...])
    for k in range(1, n_pow + 1):
        mag = jnp.exp(lr * dt * float(k))
        ang = li * dt * float(k)
        pw_ref[k - 1, 0] = mag * jnp.cos(ang)
        pw_ref[k - 1, 1] = mag * jnp.sin(ang)
    lb_re = pw_ref[0, 0]
    lb_im = pw_ref[0, 1]
    den = lr * lr + li * li
    nr = lb_re - 1.0
    f_re = (nr * lr + lb_im * li) / den
    f_im = (lb_im * lr - nr * li) / den
    for g in range(n_grp):
        fr, fi = f_re[g:g + 1, :], f_im[g:g + 1, :]
        bbr_ref[g] = fr * btr_ref[g] - fi * bti_ref[g]
        bbi_ref[g] = fr * bti_ref[g] + fi * btr_ref[g]


def _ssm_prep_call(lam_re, lam_im, log_dt, bt_re, bt_im, n_pow):
    g, p = lam_re.shape
    return pl.pallas_call(
        functools.partial(_ssm_prep_kernel, n_pow=n_pow, n_grp=g),
        out_shape=(jax.ShapeDtypeStruct((n_pow, 2, g, p), f32),
                   jax.ShapeDtypeStruct(bt_re.shape, f32),
                   jax.ShapeDtypeStruct(bt_re.shape, f32)),
        name="ssm_prep",
    )(lam_re, lam_im, log_dt.reshape(g, 1), bt_re, bt_im)


def _block_diag(blocks):
    g, a, b = blocks.shape
    eye = jnp.eye(g, dtype=blocks.dtype)
    return (eye[:, None, :, None] * blocks[:, :, None, :]).reshape(g * a, g * b)


def _modulated_norm(x, gain, shift, scale):
    ms = jnp.mean(x * x, axis=-1, keepdims=True)
    y = x * lax.rsqrt(ms + EPS) * gain
    return y * (1.0 + scale) + shift


def _in_proj_kernel(x_ref, mod_ref, g_ref, w_ref, b_ref, us_ref, up_ref, *, d, d_ssm):
    u = _modulated_norm(x_ref[...], g_ref[...], mod_ref[:, 0:d], mod_ref[:, d:2 * d])
    r = _dot(u.astype(bf16), w_ref[...]) + b_ref[...]
    for j in range(d_ssm // LANES):
        us_ref[j] = r[:, j * LANES:(j + 1) * LANES]
    up_ref[...] = r[:, d_ssm:]


def _in_proj_call(x2, mod3, g1, w_sp, b_sp, seq, tm, d_ssm):
    t, d = x2.shape
    n = w_sp.shape[1]
    per = seq // tm
    nl = d_ssm // LANES
    return pl.pallas_call(
        functools.partial(_in_proj_kernel, d=d, d_ssm=d_ssm),
        grid=(t // tm,),
        in_specs=[pl.BlockSpec((tm, d), lambda i: (i, 0)),
                  pl.BlockSpec((None, 1, mod3.shape[2]), lambda i: (i // per, 0, 0)),
                  pl.BlockSpec((1, d), lambda i: (0, 0)),
                  pl.BlockSpec((d, n), lambda i: (0, 0)),
                  pl.BlockSpec((1, n), lambda i: (0, 0))],
        out_specs=[pl.BlockSpec((nl, tm, LANES), lambda i: (0, i, 0)),
                   pl.BlockSpec((tm, n - d_ssm), lambda i: (i, 0))],
        out_shape=(jax.ShapeDtypeStruct((nl, t, LANES), f32),
                   jax.ShapeDtypeStruct((t, n - d_ssm), f32)),
        compiler_params=_cparams(("parallel",)),
        name="in_proj",
    )(x2, mod3, g1, w_sp, b_sp)


def _cmul_add(ar, ai, xr, xi, br, bi):
    return ar * xr - ai * xi + br, ar * xi + ai * xr + bi


def _mixers_kernel(us_ref, up_ref, bbd_ref, cbd_ref, pw_ref, dsk_ref, wglu_ref, bglu_ref,
                   pw_pool_ref, psc_ref, a_ref, b_ref, e_ref, s_ref, y_ref,
                   *, seq, d_ssm, d_pool, n_state, slabs):
    nc = seq // slabs
    nl = d_ssm // LANES
    ns = n_state
    dsk = dsk_ref[...]
    l1r = pw_ref[0:1, 0:ns]
    l1i = pw_ref[0:1, ns:2 * ns]

    xr = xi = None
    for t in range(slabs):
        ut = jnp.concatenate([us_ref[j, pl.ds(t, nc, stride=slabs), :] for j in range(nl)], axis=1)
        bu = _dot(ut.astype(bf16), bbd_ref[...])
        br, bi = bu[:, 0:ns], bu[:, ns:2 * ns]
        if t == 0:
            xr, xi = br, bi
        else:
            xr, xi = _cmul_add(l1r, l1i, xr, xi, br, bi)
        xc = jnp.concatenate([xr, xi], axis=1).astype(bf16)
        y_ref[t] = _dot(xc, cbd_ref[...]) + dsk * ut
    e_ref[:, 0:ns] = xr
    e_ref[:, ns:2 * ns] = xi

    lLr = pw_ref[slabs - 1:slabs, 0:ns]
    lLi = pw_ref[slabs - 1:slabs, ns:2 * ns]

    def chunk_step(c, carry):
        sr, si = carry
        s_ref[pl.ds(c, 1), 0:ns] = sr
        s_ref[pl.ds(c, 1), ns:2 * ns] = si
        er = e_ref[pl.ds(c, 1), 0:ns]
        ei = e_ref[pl.ds(c, 1), ns:2 * ns]
        return _cmul_add(lLr, lLi, sr, si, er, ei)

    zero = jnp.zeros((1, ns), f32)
    lax.fori_loop(0, nc, chunk_step, (zero, zero))

    sr = s_ref[:, 0:ns]
    si = s_ref[:, ns:2 * ns]
    for t in range(slabs):
        pr = pw_ref[t:t + 1, 0:ns]
        pi = pw_ref[t:t + 1, ns:2 * ns]
        zr = pr * sr - pi * si
        zi = pr * si + pi * sr
        zc = jnp.concatenate([zr, zi], axis=1).astype(bf16)
        y = y_ref[t] + _dot(zc, cbd_ref[...])
        y = jax.nn.gelu(y)
        z = y * jax.nn.sigmoid(_dot(y.astype(bf16), wglu_ref[...]) + bglu_ref[...])
        for j in range(nl):
            a_ref[j, pl.ds(t, nc, stride=slabs), :] = z[:, j * LANES:(j + 1) * LANES]

    gc = d_pool // len(POOL_WINDOWS)
    row = lax.broadcasted_iota(jnp.int32, (seq, gc), 0)
    for gi, w in enumerate(POOL_WINDOWS):
        lo = gi * gc
        v = up_ref[:, lo:lo + gc]
        acc = v
        span = 1
        while span < w:
            acc = acc + jnp.where(row >= span, pltpu.roll(acc, span, axis=0), 0.0)
            span *= 2
        cnt = jnp.minimum(row + 1, w).astype(f32)
        m = acc / cnt - v
        yg = _dot(m.astype(bf16), pw_pool_ref[gi])
        b_ref[:, lo:lo + gc] = yg * psc_ref[:, lo:lo + gc]


def _mixers_call(us, up, bbd, cbd, pw, dsk, wglu, bglu, pool_w, pool_scale, bsz, seq):
    nl, t, _ = us.shape
    d_ssm = nl * LANES
    d_pool = up.shape[1]
    ns2 = bbd.shape[1]
    slabs = SSM_SLABS
    nc = seq // slabs
    kern = functools.partial(_mixers_kernel, seq=seq, d_ssm=d_ssm, d_pool=d_pool,
                             n_state=ns2 // 2, slabs=slabs)
    const2 = lambda b: (0, 0)
    return pl.pallas_call(
        kern,
        grid=(bsz,),
        in_specs=[pl.BlockSpec((nl, seq, LANES), lambda b: (0, b, 0)),
                  pl.BlockSpec((seq, d_pool), lambda b: (b, 0)),
                  pl.BlockSpec(bbd.shape, const2),
                  pl.BlockSpec(cbd.shape, const2),
                  pl.BlockSpec(pw.shape, const2),
                  pl.BlockSpec(dsk.shape, const2),
                  pl.BlockSpec(wglu.shape, const2),
                  pl.BlockSpec(bglu.shape, const2),
                  pl.BlockSpec(pool_w.shape, lambda b: (0, 0, 0)),
                  pl.BlockSpec(pool_scale.shape, const2)],
        out_specs=[pl.BlockSpec((nl, seq, LANES), lambda b: (0, b, 0)),
                   pl.BlockSpec((seq, d_pool), lambda b: (b, 0))],
        out_shape=(jax.ShapeDtypeStruct((nl, t, LANES), f32),
                   jax.ShapeDtypeStruct((t, d_pool), f32)),
        scratch_shapes=[pltpu.VMEM((nc, ns2), f32),
                        pltpu.VMEM((nc, ns2), f32),
                        pltpu.VMEM((slabs, nc, d_ssm), f32)],
        compiler_params=_cparams(("parallel",)),
        name="mixers",
    )(us, up, bbd, cbd, pw, dsk, wglu, bglu, pool_w, pool_scale)


def _post_kernel(x_ref, a_ref, b_ref, mod_ref, g1_ref, g2_ref, wg_ref, bg_ref, wps_ref, wpp_ref,
                 wout_ref, wr_ref, br_ref,
                 h_ref, up_ref, meta_ref, cnt_ref, carry_ref,
                 *, d, d_ssm, n_grp, n_exp, tm):
    i = pl.program_id(0)

    @pl.when(i == 0)
    def _():
        carry_ref[...] = jnp.zeros_like(carry_ref)

    x = x_ref[...]
    shift1, scale1, gate1 = mod_ref[:, 0:d], mod_ref[:, d:2 * d], mod_ref[:, 2 * d:3 * d]
    shift2, scale2 = mod_ref[:, 3 * d:4 * d], mod_ref[:, 4 * d:5 * d]
    u = _modulated_norm(x, g1_ref[...], shift1, scale1).astype(bf16)
    gates = _dot(u, wg_ref[...]) + bg_ref[...]
    a = jnp.concatenate([a_ref[j] for j in range(d_ssm // LANES)], axis=1).astype(bf16)
    b = b_ref[...].astype(bf16)
    merged = (jax.nn.sigmoid(gates[:, 0:d]) * _dot(a, wps_ref[...])
              + jax.nn.sigmoid(gates[:, d:2 * d]) * _dot(b, wpp_ref[...]))
    h = x + gate1 * _dot(merged.astype(bf16), wout_ref[...])
    h_ref[...] = h

    u2 = _modulated_norm(h, g2_ref[...], shift2, scale2)

    half = d // 2
    hi_bits = lax.bitcast_convert_type(u2[:, 0:half].astype(bf16).astype(f32), jnp.uint32)
    lo_bits = lax.bitcast_convert_type(u2[:, half:d].astype(bf16).astype(f32), jnp.uint32)
    up_ref[...] = hi_bits | (lo_bits >> 16)

    u_hi = u2.astype(bf16)
    u_lo = (u2 - u_hi.astype(f32)).astype(bf16)
    r1 = _dot(u_hi, wr_ref[...])
    r2 = _dot(u_lo, wr_ref[...])
    lg = r1 + pltpu.roll(r1, ROUTE_LANES // 2, axis=1) + r2 + br_ref[...]

    lane = lax.broadcasted_iota(jnp.int32, (tm, ROUTE_LANES), 1).astype(f32)
    big = float(ROUTE_LANES)
    epg = float(n_exp // n_grp)
    is_c = lane < n_grp
    cl = jnp.where(is_c, lg, NEG)
    cmax = jnp.max(cl, axis=-1, keepdims=True)
    grp = jnp.min(jnp.where(cl == cmax, lane, big), axis=-1, keepdims=True)
    p_grp = 1.0 / jnp.sum(jnp.where(is_c, jnp.exp(cl - cmax), 0.0), axis=-1, keepdims=True)

    f_lo = n_grp + grp * epg
    fl = jnp.where((lane >= f_lo) & (lane < f_lo + epg), lg, NEG)
    f1 = jnp.max(fl, axis=-1, keepdims=True)
    i1 = jnp.min(jnp.where(fl == f1, lane, big), axis=-1, keepdims=True)
    fl2 = jnp.where(lane == i1, NEG, fl)
    f2 = jnp.max(fl2, axis=-1, keepdims=True)
    i2 = jnp.min(jnp.where(fl2 == f2, lane, big), axis=-1, keepdims=True)
    t2 = jnp.exp(f2 - f1)
    w0 = p_grp / (1.0 + t2)
    w1 = p_grp * t2 / (1.0 + t2)

    sel0 = lane == i1
    sel1 = lane == i2
    oh = jnp.where(sel0 | sel1, 1.0, 0.0)
    r_i = lax.broadcasted_iota(jnp.int32, (tm, tm), 0)
    c_i = lax.broadcasted_iota(jnp.int32, (tm, tm), 1)
    tri = jnp.where(c_i < r_i, 1.0, 0.0).astype(bf16)
    before = carry_ref[...] + _dot(tri, oh.astype(bf16))
    rank0 = jnp.sum(jnp.where(sel0, before, 0.0), axis=-1, keepdims=True)
    rank1 = jnp.sum(jnp.where(sel1, before, 0.0), axis=-1, keepdims=True)
    carry_ref[...] = carry_ref[...] + jnp.sum(oh, axis=0, keepdims=True)
    cnt_ref[...] = carry_ref[...]

    e0 = i1 - n_grp
    e1 = i2 - n_grp
    meta = jnp.zeros((tm, ROUTE_LANES), f32)
    for k, val in enumerate((e0, e1, w0, w1, rank0, rank1)):
        meta = jnp.where(lane == float(k), val, meta)
    meta_ref[...] = meta


def _post_call(x2, a3, b2, mod3, g1, g2, wg, bg, wps, wpp, wout, wr, br, seq, tm, n_grp, n_exp):
    t, d = x2.shape
    d_ssm = wps.shape[0]
    nl = a3.shape[0]
    per = seq // tm
    kern = functools.partial(_post_kernel, d=d, d_ssm=d_ssm, n_grp=n_grp, n_exp=n_exp, tm=tm)
    const2 = lambda i: (0, 0)
    return pl.pallas_call(
        kern,
        grid=(t // tm,),
        in_specs=[pl.BlockSpec((tm, d), lambda i: (i, 0)),
                  pl.BlockSpec((nl, tm, LANES), lambda i: (0, i, 0)),
                  pl.BlockSpec((tm, b2.shape[1]), lambda i: (i, 0)),
                  pl.BlockSpec((None, 1, mod3.shape[2]), lambda i: (i // per, 0, 0)),
                  pl.BlockSpec((1, d), const2),
                  pl.BlockSpec((1, d), const2),
                  pl.BlockSpec(wg.shape, const2),
                  pl.BlockSpec(bg.shape, const2),
                  pl.BlockSpec(wps.shape, const2),
                  pl.BlockSpec(wpp.shape, const2),
                  pl.BlockSpec(wout.shape, const2),
                  pl.BlockSpec(wr.shape, const2),
                  pl.BlockSpec(br.shape, const2)],
        out_specs=[pl.BlockSpec((tm, d), lambda i: (i, 0)),
                   pl.BlockSpec((tm, d // 2), lambda i: (i, 0)),
                   pl.BlockSpec((tm, ROUTE_LANES), lambda i: (i, 0)),
                   pl.BlockSpec((1, ROUTE_LANES), const2)],
        out_shape=(jax.ShapeDtypeStruct((t, d), f32),
                   jax.ShapeDtypeStruct((t, d // 2), jnp.uint32),
                   jax.ShapeDtypeStruct((t, ROUTE_LANES), f32),
                   jax.ShapeDtypeStruct((1, ROUTE_LANES), f32)),
        scratch_shapes=[pltpu.VMEM((1, ROUTE_LANES), f32)],
        compiler_params=_cparams(("arbitrary",)),
        name="post",
    )(x2, a3, b2, mod3, g1, g2, wg, bg, wps, wpp, wout, wr, br)


def _row_copy(src_ref, dst_ref, src_row, dst_row, sem):
    return pltpu.make_async_copy(src_ref.at[pl.ds(src_row, 1)], dst_ref.at[pl.ds(dst_row, 1)], sem)


def _scatter_kernel(pos_ref, up_ref, xs_ref, sem, *, tm):
    def issue(r, c):
        _row_copy(up_ref, xs_ref, r, pos_ref[0, 0, 2 * r], sem).start()
        _row_copy(up_ref, xs_ref, r, pos_ref[0, 0, 2 * r + 1], sem).start()
        return c

    lax.fori_loop(0, tm, issue, 0)

    def drain(r, c):
        _row_copy(up_ref, xs_ref, 0, 0, sem).wait()
        _row_copy(up_ref, xs_ref, 0, 0, sem).wait()
        return c

    lax.fori_loop(0, tm, drain, 0)


def _scatter_call(up, pos3, n_rows, tm):
    t, w = up.shape
    return pl.pallas_call(
        functools.partial(_scatter_kernel, tm=tm),
        grid=(t // tm,),
        in_specs=[pl.BlockSpec((1, 1, 2 * tm), lambda i: (i, 0, 0), memory_space=pltpu.SMEM),
                  pl.BlockSpec((tm, w), lambda i: (i, 0))],
        out_specs=pl.BlockSpec(memory_space=pl.ANY),
        out_shape=jax.ShapeDtypeStruct((n_rows, w), up.dtype),
        scratch_shapes=[pltpu.SemaphoreType.DMA(())],
        compiler_params=_cparams(("arbitrary",)),
        name="scatter_rows",
    )(pos3, up)


def _experts_kernel(te_ref, nt_ref, xs_ref, wgu_ref, wd_ref, ys_ref, *, d_exp):
    i = pl.program_id(0)

    @pl.when(i < nt_ref[0])
    def _():
        w = xs_ref[...]
        hi = lax.bitcast_convert_type(w & jnp.uint32(0xFFFF0000), f32)
        lo = lax.bitcast_convert_type(w << 16, f32)
        x = jnp.concatenate([hi, lo], axis=1).astype(bf16)
        h = _dot(x, wgu_ref[...])
        h1 = h[:, 0:d_exp]
        act = h1 * jax.nn.sigmoid(h1) * h[:, d_exp:2 * d_exp]
        ys_ref[...] = _dot(act.astype(bf16), wd_ref[...])


def _experts_call(tile_expert, n_tiles, xs, wgu, wd, tr):
    n_rows, w = xs.shape
    n_exp, d, two_de = wgu.shape
    d_exp = two_de // 2
    nt_max = n_rows // tr

    def row_map(i, te, nt):
        return (jnp.minimum(i, nt[0] - 1), 0)

    def w_map(i, te, nt):
        return (te[jnp.minimum(i, nt[0] - 1)], 0, 0)

    grid_spec = pltpu.PrefetchScalarGridSpec(
        num_scalar_prefetch=2,
        grid=(nt_max,),
        in_specs=[pl.BlockSpec((tr, w), row_map),
                  pl.BlockSpec((None, d, two_de), w_map),
                  pl.BlockSpec((None, d_exp, d), w_map)],
        out_specs=pl.BlockSpec((tr, d), row_map),
    )
    return pl.pallas_call(
        functools.partial(_experts_kernel, d_exp=d_exp),
        grid_spec=grid_spec,
        out_shape=jax.ShapeDtypeStruct((n_rows, d), f32),
        compiler_params=_cparams(("arbitrary",)),
        name="experts",
    )(tile_expert, n_tiles, xs, wgu, wd)


def _combine_kernel(pos_ref, h_ref, meta_ref, mod_ref, gf_ref, ys_ref, o_ref, g0_ref, g1_ref, sem,
                    *, d, tm, final_norm):
    def issue(r, c):
        _row_copy(ys_ref, g0_ref, pos_ref[0, 0, 2 * r], r, sem).start()
        _row_copy(ys_ref, g1_ref, pos_ref[0, 0, 2 * r + 1], r, sem).start()
        return c

    lax.fori_loop(0, tm, issue, 0)

    def drain(r, c):
        _row_copy(ys_ref, g0_ref, 0, 0, sem).wait()
        _row_copy(ys_ref, g1_ref, 0, 0, sem).wait()
        return c

    lax.fori_loop(0, tm, drain, 0)

    gate2 = mod_ref[:, 5 * d:6 * d]
    w0 = meta_ref[:, 2:3]
    w1 = meta_ref[:, 3:4]
    h = h_ref[...] + gate2 * (w0 * g0_ref[...] + w1 * g1_ref[...])
    if final_norm:
        ms = jnp.mean(h * h, axis=-1, keepdims=True)
        h = h * lax.rsqrt(ms + EPS) * gf_ref[...]
    o_ref[...] = h


def _combine_call(pos3, h1, meta, mod3, gf, ys, seq, tm, final_norm):
    t, d = h1.shape
    per = seq // tm
    return pl.pallas_call(
        functools.partial(_combine_kernel, d=d, tm=tm, final_norm=final_norm),
        grid=(t // tm,),
        in_specs=[pl.BlockSpec((1, 1, 2 * tm), lambda i: (i, 0, 0), memory_space=pltpu.SMEM),
                  pl.BlockSpec((tm, d), lambda i: (i, 0)),
                  pl.BlockSpec((tm, ROUTE_LANES), lambda i: (i, 0)),
                  pl.BlockSpec((None, 1, mod3.shape[2]), lambda i: (i // per, 0, 0)),
                  pl.BlockSpec((1, d), lambda i: (0, 0)),
                  pl.BlockSpec(memory_space=pl.ANY)],
        out_specs=pl.BlockSpec((tm, d), lambda i: (i, 0)),
        out_shape=jax.ShapeDtypeStruct((t, d), f32),
        scratch_shapes=[pltpu.VMEM((tm, d), f32), pltpu.VMEM((tm, d), f32),
                        pltpu.SemaphoreType.DMA(())],
        compiler_params=_cparams(("arbitrary",)),
        name="combine",
    )(pos3, h1, meta, mod3, gf, ys)


def _row_tile(seq, target):
    tm = min(target, seq)
    assert seq % tm == 0 and tm % 8 == 0
    return tm


def _layer(h2, mod3, p, bsz, seq, final_gain, final_norm):
    t, d = h2.shape
    g, n_p, n_h = p["ssm_b_re"].shape
    d_ssm = g * n_h
    d_pool = p["pool_scale"].shape[-1]
    n_grp = p["router_coarse_w"].shape[-1]
    n_exp = p["router_fine_w"].shape[-1]
    assert n_grp + n_exp <= ROUTE_LANES // 2 and seq % SSM_SLABS == 0
    tm = _row_tile(seq, 512)

    pw, bbt_re, bbt_im = _ssm_prep_call(
        p["ssm_lam_re"], p["ssm_lam_im"], p["ssm_log_dt"],
        jnp.swapaxes(p["ssm_b_re"], 1, 2), jnp.swapaxes(p["ssm_b_im"], 1, 2), SSM_SLABS)
    bbd = jnp.concatenate([_block_diag(bbt_re), _block_diag(bbt_im)], axis=1).astype(bf16)
    cbd = jnp.concatenate([_block_diag(jnp.swapaxes(p["ssm_c_re"], 1, 2)),
                           -_block_diag(jnp.swapaxes(p["ssm_c_im"], 1, 2))], axis=0).astype(bf16)
    pw2 = pw.reshape(SSM_SLABS, 2 * g * n_p)

    w_in, b_in = p["w_in"], p["b_in"]
    n_sp = d_ssm + d_pool
    us, upool = _in_proj_call(h2, mod3, p["norm1_g"].reshape(1, d), w_in[:, :n_sp].astype(bf16),
                              b_in[:n_sp].reshape(1, n_sp), seq, tm, d_ssm)
    a3, b2 = _mixers_call(us, upool, bbd, cbd, pw2, p["ssm_d"].reshape(1, d_ssm),
                          p["ssm_w_glu"].astype(bf16), p["ssm_b_glu"].reshape(1, d_ssm),
                          p["pool_w"].astype(bf16), p["pool_scale"].reshape(1, d_pool),
                          bsz, seq)

    wr = jnp.concatenate([p["router_coarse_w"], p["router_fine_w"]], axis=1)
    wr_hi = wr.astype(bf16)
    wr_lo = (wr - wr_hi.astype(f32)).astype(bf16)
    padc = ROUTE_LANES // 2 - wr.shape[1]
    wr_cat = jnp.concatenate([jnp.pad(wr_hi, ((0, 0), (0, padc))),
                              jnp.pad(wr_lo, ((0, 0), (0, padc)))], axis=1)
    br = jnp.pad(jnp.concatenate([p["router_coarse_b"], p["router_fine_b"]]),
                 (0, ROUTE_LANES - wr.shape[1])).reshape(1, ROUTE_LANES)

    h1, up, meta, cnt = _post_call(
        h2, a3, b2, mod3, p["norm1_g"].reshape(1, d), p["norm2_g"].reshape(1, d),
        w_in[:, n_sp:].astype(bf16), b_in[n_sp:].reshape(1, 2 * d),
        p["w_proj_ssm"].astype(bf16), p["w_proj_pool"].astype(bf16), p["w_out"].astype(bf16),
        wr_cat, br, seq, tm, n_grp, n_exp)

    tr = 512
    counts = cnt[0, n_grp:n_grp + n_exp].astype(jnp.int32)
    padded = ((counts + tr - 1) // tr) * tr
    ends = jnp.cumsum(padded)
    offs = ends - padded
    e01 = meta[:, 0:2].astype(jnp.int32)
    pos = offs[e01] + meta[:, 4:6].astype(jnp.int32)
    pos3 = pos.reshape(t // tm, 1, 2 * tm)
    nt_max = (TOP_K_FINE * t) // tr + n_exp
    n_tiles = (ends[-1] // tr).astype(jnp.int32).reshape(1)
    tile_expert = jnp.minimum(
        jnp.searchsorted(ends, jnp.arange(nt_max, dtype=jnp.int32) * tr, side="right"),
        n_exp - 1).astype(jnp.int32)

    xs = _scatter_call(up, pos3, nt_max * tr, tm)
    wgu = jnp.concatenate([p["moe_w_gate"], p["moe_w_up"]], axis=-1).astype(bf16)
    ys = _experts_call(tile_expert, n_tiles, xs, wgu, p["moe_w_down"].astype(bf16), tr)
    return _combine_call(pos3, h1, meta, mod3, final_gain.reshape(1, d), ys, seq, tm, final_norm)


def kernel(x, c, w_mod, b_mod, norm1_g, w_in, b_in, ssm_lam_re, ssm_lam_im, ssm_log_dt, ssm_b_re, ssm_b_im, ssm_c_re, ssm_c_im, ssm_d, ssm_w_glu, ssm_b_glu, pool_w, pool_scale, w_proj_ssm, w_proj_pool, w_out, norm2_g, router_coarse_w, router_coarse_b, router_fine_w, router_fine_b, moe_w_gate, moe_w_up, moe_w_down, norm_f_g):
    bsz, seq, d = x.shape
    depth = w_mod.shape[0]
    per_layer = dict(
        norm1_g=norm1_g, w_in=w_in, b_in=b_in, ssm_lam_re=ssm_lam_re, ssm_lam_im=ssm_lam_im,
        ssm_log_dt=ssm_log_dt, ssm_b_re=ssm_b_re, ssm_b_im=ssm_b_im, ssm_c_re=ssm_c_re,
        ssm_c_im=ssm_c_im, ssm_d=ssm_d, ssm_w_glu=ssm_w_glu, ssm_b_glu=ssm_b_glu, pool_w=pool_w,
        pool_scale=pool_scale, w_proj_ssm=w_proj_ssm, w_proj_pool=w_proj_pool, w_out=w_out,
        norm2_g=norm2_g, router_coarse_w=router_coarse_w, router_coarse_b=router_coarse_b,
        router_fine_w=router_fine_w, router_fine_b=router_fine_b, moe_w_gate=moe_w_gate,
        moe_w_up=moe_w_up, moe_w_down=moe_w_down)
    h2 = x.reshape(bsz * seq, d)
    for l in range(depth):
        p = {k: v[l] for k, v in per_layer.items()}
        mod3 = _mod_call(c, w_mod[l], b_mod[l]).reshape(bsz, 1, N_MOD * d)
        h2 = _layer(h2, mod3, p, bsz, seq, norm_f_g, final_norm=(l == depth - 1))
    return h2.reshape(bsz, seq, d)
```

```python
import functools
import math

import jax
import jax.numpy as jnp
from jax import lax
from jax.experimental import pallas as pl
from jax.experimental.pallas import tpu as pltpu

EPS = 1e-6
POOL_WINDOWS = (2, 4, 8, 16)
TOP_K_FINE = 2
N_MOD = 6
LANES = 128
SSM_SLABS = 8
ROUTE_LANES = 128
ROW_DMA_UNROLL = 8
VMEM_LIMIT = 56 * 1024 * 1024
NEG = -1e30

f32 = jnp.float32
bf16 = jnp.bfloat16


def _cparams(sem):
    return pltpu.CompilerParams(dimension_semantics=sem, vmem_limit_bytes=VMEM_LIMIT)


def _dot(a, b):
    return jnp.dot(a, b, preferred_element_type=f32)


def _store_panels(ref, val, rows):
    p = val.shape[1] // LANES
    for j in range(p):
        ref[pl.ds(j, rows, stride=p), :] = val[:, j * LANES:(j + 1) * LANES]


def _load_panels(ref, rows, p):
    return jnp.concatenate([ref[pl.ds(j, rows, stride=p), :] for j in range(p)], axis=1)


def _mod_kernel(c_ref, w_ref, b_ref, o_ref):
    c = c_ref[...]
    ca = c * jax.nn.sigmoid(c)
    o_ref[...] = jnp.dot(ca, w_ref[...], preferred_element_type=f32,
                         precision=lax.Precision.HIGHEST) + b_ref[...]


def _mod_call(c, w_mod, b_mod):
    bsz, d = c.shape
    n = w_mod.shape[1]
    tn = d
    return pl.pallas_call(
        _mod_kernel,
        grid=(n // tn,),
        in_specs=[pl.BlockSpec((bsz, d), lambda j: (0, 0)),
                  pl.BlockSpec((d, tn), lambda j: (0, j)),
                  pl.BlockSpec((1, tn), lambda j: (0, j))],
        out_specs=pl.BlockSpec((bsz, tn), lambda j: (0, j)),
        out_shape=jax.ShapeDtypeStruct((bsz, n), f32),
        compiler_params=_cparams(("parallel",)),
        name="mod",
    )(c, w_mod, b_mod.reshape(1, n))


def _ssm_prep_kernel(lr_ref, li_ref, ldt_ref, btr_ref, bti_ref, pw_ref, bbr_ref, bbi_ref,
                     *, n_pow, n_grp):
    lr = lr_ref[...]
    li = li_ref[...]
    dt = jnp.exp(ldt_ref[...])
    for k in range(1, n_pow + 1):
        mag = jnp.exp(lr * dt * float(k))
        ang = li * dt * float(k)
        pw_ref[k - 1, 0] = mag * jnp.cos(ang)
        pw_ref[k - 1, 1] = mag * jnp.sin(ang)
    lb_re = pw_ref[0, 0]
    lb_im = pw_ref[0, 1]
    den = lr * lr + li * li
    nr = lb_re - 1.0
    f_re = (nr * lr + lb_im * li) / den
    f_im = (lb_im * lr - nr * li) / den
    for g in range(n_grp):
        fr, fi = f_re[g:g + 1, :], f_im[g:g + 1, :]
        bbr_ref[g] = fr * btr_ref[g] - fi * bti_ref[g]
        bbi_ref[g] = fr * bti_ref[g] + fi * btr_ref[g]


def _ssm_prep_call(lam_re, lam_im, log_dt, bt_re, bt_im, n_pow):
    g, p = lam_re.shape
    return pl.pallas_call(
        functools.partial(_ssm_prep_kernel, n_pow=n_pow, n_grp=g),
        out_shape=(jax.ShapeDtypeStruct((n_pow, 2, g, p), f32),
                   jax.ShapeDtypeStruct(bt_re.shape, f32),
                   jax.ShapeDtypeStruct(bt_re.shape, f32)),
        name="ssm_prep",
    )(lam_re, lam_im, log_dt.reshape(g, 1), bt_re, bt_im)


def _block_diag(blocks):
    g, a, b = blocks.shape
    eye = jnp.eye(g, dtype=blocks.dtype)
    return (eye[:, None, :, None] * blocks[:, :, None, :]).reshape(g * a, g * b)


def _modulated_norm(x, gain, shift, scale):
    ms = jnp.mean(x * x, axis=-1, keepdims=True)
    y = x * lax.rsqrt(ms + EPS) * gain
    return y * (1.0 + scale) + shift


def _in_proj_kernel(x_ref, mod_ref, g_ref, w_ref, b_ref, us_ref, up_ref, *, d, d_ssm):
    u = _modulated_norm(x_ref[...], g_ref[...], mod_ref[:, 0:d], mod_ref[:, d:2 * d])
    r = _dot(u.astype(bf16), w_ref[...]) + b_ref[...]
    for j in range(d_ssm // LANES):
        us_ref[j] = r[:, j * LANES:(j + 1) * LANES]
    up_ref[...] = r[:, d_ssm:]


def _in_proj_call(x2, mod3, g1, w_sp, b_sp, seq, tm, d_ssm):
    t, d = x2.shape
    n = w_sp.shape[1]
    per = seq // tm
    nl = d_ssm // LANES
    return pl.pallas_call(
        functools.partial(_in_proj_kernel, d=d, d_ssm=d_ssm),
        grid=(t // tm,),
        in_specs=[pl.BlockSpec((tm, d), lambda i: (i, 0)),
                  pl.BlockSpec((None, 1, mod3.shape[2]), lambda i: (i // per, 0, 0)),
                  pl.BlockSpec((1, d), lambda i: (0, 0)),
                  pl.BlockSpec((d, n), lambda i: (0, 0)),
                  pl.BlockSpec((1, n), lambda i: (0, 0))],
        out_specs=[pl.BlockSpec((nl, tm, LANES), lambda i: (0, i, 0)),
                   pl.BlockSpec((tm, n - d_ssm), lambda i: (i, 0))],
        out_shape=(jax.ShapeDtypeStruct((nl, t, LANES), f32),
                   jax.ShapeDtypeStruct((t, n - d_ssm), f32)),
        compiler_params=_cparams(("parallel",)),
        name="in_proj",
    )(x2, mod3, g1, w_sp, b_sp)


def _cmul_add(ar, ai, xr, xi, br, bi):
    return ar * xr - ai * xi + br, ar * xi + ai * xr + bi


def _mixers_kernel(us_ref, up_ref, bbd_ref, cbd_ref, pw_ref, dsk_ref, wglu_ref, bglu_ref,
                   pw_pool_ref, psc_ref, a_ref, b_ref, e_ref, s_ref, y_ref,
                   *, seq, d_ssm, d_pool, n_state, slabs):
    nc = seq // slabs
    nl = d_ssm // LANES
    ns = n_state
    dsk = dsk_ref[...]
    l1r = pw_ref[0:1, 0:ns]
    l1i = pw_ref[0:1, ns:2 * ns]

    xr = xi = None
    for t in range(slabs):
        ut = jnp.concatenate([us_ref[j, pl.ds(t, nc, stride=slabs), :] for j in range(nl)], axis=1)
        bu = _dot(ut.astype(bf16), bbd_ref[...])
        br, bi = bu[:, 0:ns], bu[:, ns:2 * ns]
        if t == 0:
            xr, xi = br, bi
        else:
            xr, xi = _cmul_add(l1r, l1i, xr, xi, br, bi)
        xc = jnp.concatenate([xr, xi], axis=1).astype(bf16)
        y_ref[t] = _dot(xc, cbd_ref[...]) + dsk * ut
    e_ref[:, 0:ns] = xr
    e_ref[:, ns:2 * ns] = xi

    lLr = pw_ref[slabs - 1:slabs, 0:ns]
    lLi = pw_ref[slabs - 1:slabs, ns:2 * ns]

    def chunk_step(c, carry):
        sr, si = carry
        s_ref[pl.ds(c, 1), 0:ns] = sr
        s_ref[pl.ds(c, 1), ns:2 * ns] = si
        er = e_ref[pl.ds(c, 1), 0:ns]
        ei = e_ref[pl.ds(c, 1), ns:2 * ns]
        return _cmul_add(lLr, lLi, sr, si, er, ei)

    zero = jnp.zeros((1, ns), f32)
    lax.fori_loop(0, nc, chunk_step, (zero, zero))

    sr = s_ref[:, 0:ns]
    si = s_ref[:, ns:2 * ns]
    for t in range(slabs):
        pr = pw_ref[t:t + 1, 0:ns]
        pi = pw_ref[t:t + 1, ns:2 * ns]
        zr = pr * sr - pi * si
        zi = pr * si + pi * sr
        zc = jnp.concatenate([zr, zi], axis=1).astype(bf16)
        y = y_ref[t] + _dot(zc, cbd_ref[...])
        y = jax.nn.gelu(y)
        z = y * jax.nn.sigmoid(_dot(y.astype(bf16), wglu_ref[...]) + bglu_ref[...])
        for j in range(nl):
            a_ref[j, pl.ds(t, nc, stride=slabs), :] = z[:, j * LANES:(j + 1) * LANES]

    gc = d_pool // len(POOL_WINDOWS)
    row = lax.broadcasted_iota(jnp.int32, (seq, gc), 0)
    for gi, w in enumerate(POOL_WINDOWS):
        lo = gi * gc
        v = up_ref[:, lo:lo + gc]
        acc = v
        span = 1
        while span < w:
            acc = acc + jnp.where(row >= span, pltpu.roll(acc, span, axis=0), 0.0)
            span *= 2
        cnt = jnp.minimum(row + 1, w).astype(f32)
        m = acc / cnt - v
        yg = _dot(m.astype(bf16), pw_pool_ref[gi])
        b_ref[:, lo:lo + gc] = yg * psc_ref[:, lo:lo + gc]


def _mixers_call(us, up, bbd, cbd, pw, dsk, wglu, bglu, pool_w, pool_scale, bsz, seq):
    nl, t, _ = us.shape
    d_ssm = nl * LANES
    d_pool = up.shape[1]
    ns2 = bbd.shape[1]
    slabs = SSM_SLABS
    nc = seq // slabs
    kern = functools.partial(_mixers_kernel, seq=seq, d_ssm=d_ssm, d_pool=d_pool,
                             n_state=ns2 // 2, slabs=slabs)
    const2 = lambda b: (0, 0)
    return pl.pallas_call(
        kern,
        grid=(bsz,),
        in_specs=[pl.BlockSpec((nl, seq, LANES), lambda b: (0, b, 0)),
                  pl.BlockSpec((seq, d_pool), lambda b: (b, 0)),
                  pl.BlockSpec(bbd.shape, const2),
                  pl.BlockSpec(cbd.shape, const2),
                  pl.BlockSpec(pw.shape, const2),
                  pl.BlockSpec(dsk.shape, const2),
                  pl.BlockSpec(wglu.shape, const2),
                  pl.BlockSpec(bglu.shape, const2),
                  pl.BlockSpec(pool_w.shape, lambda b: (0, 0, 0)),
                  pl.BlockSpec(pool_scale.shape, const2)],
        out_specs=[pl.BlockSpec((nl, seq, LANES), lambda b: (0, b, 0)),
                   pl.BlockSpec((seq, d_pool), lambda b: (b, 0))],
        out_shape=(jax.ShapeDtypeStruct((nl, t, LANES), f32),
                   jax.ShapeDtypeStruct((t, d_pool), f32)),
        scratch_shapes=[pltpu.VMEM((nc, ns2), f32),
                        pltpu.VMEM((nc, ns2), f32),
                        pltpu.VMEM((slabs, nc, d_ssm), f32)],
        compiler_params=_cparams(("parallel",)),
        name="mixers",
    )(us, up, bbd, cbd, pw, dsk, wglu, bglu, pool_w, pool_scale)


def _post_kernel(x_ref, a_ref, b_ref, mod_ref, g1_ref, g2_ref, wg_ref, bg_ref, wps_ref, wpp_ref,
                 wout_ref, wr_ref, br_ref,
                 h_ref, up_ref, meta_ref, cnt_ref, carry_ref,
                 *, d, d_ssm, n_grp, n_exp, tm):
    i = pl.program_id(0)

    @pl.when(i == 0)
    def _():
        carry_ref[...] = jnp.zeros_like(carry_ref)

    x = x_ref[...]
    shift1, scale1, gate1 = mod_ref[:, 0:d], mod_ref[:, d:2 * d], mod_ref[:, 2 * d:3 * d]
    shift2, scale2 = mod_ref[:, 3 * d:4 * d], mod_ref[:, 4 * d:5 * d]
    u = _modulated_norm(x, g1_ref[...], shift1, scale1).astype(bf16)
    gates = _dot(u, wg_ref[...]) + bg_ref[...]
    a = jnp.concatenate([a_ref[j] for j in range(d_ssm // LANES)], axis=1).astype(bf16)
    b = b_ref[...].astype(bf16)
    merged = (jax.nn.sigmoid(gates[:, 0:d]) * _dot(a, wps_ref[...])
              + jax.nn.sigmoid(gates[:, d:2 * d]) * _dot(b, wpp_ref[...]))
    h = x + gate1 * _dot(merged.astype(bf16), wout_ref[...])
    h_ref[...] = h

    u2 = _modulated_norm(h, g2_ref[...], shift2, scale2)

    half = d // 2
    hi_bits = lax.bitcast_convert_type(u2[:, 0:half].astype(bf16).astype(f32), jnp.uint32)
    lo_bits = lax.bitcast_convert_type(u2[:, half:d].astype(bf16).astype(f32), jnp.uint32)
    _store_panels(up_ref, hi_bits | (lo_bits >> 16), tm)

    u_hi = u2.astype(bf16)
    u_lo = (u2 - u_hi.astype(f32)).astype(bf16)
    r1 = _dot(u_hi, wr_ref[...])
    r2 = _dot(u_lo, wr_ref[...])
    lg = r1 + pltpu.roll(r1, ROUTE_LANES // 2, axis=1) + r2 + br_ref[...]

    lane = lax.broadcasted_iota(jnp.int32, (tm, ROUTE_LANES), 1).astype(f32)
    big = float(ROUTE_LANES)
    epg = float(n_exp // n_grp)
    is_c = lane < n_grp
    cl = jnp.where(is_c, lg, NEG)
    cmax = jnp.max(cl, axis=-1, keepdims=True)
    grp = jnp.min(jnp.where(cl == cmax, lane, big), axis=-1, keepdims=True)
    p_grp = 1.0 / jnp.sum(jnp.where(is_c, jnp.exp(cl - cmax), 0.0), axis=-1, keepdims=True)

    f_lo = n_grp + grp * epg
    fl = jnp.where((lane >= f_lo) & (lane < f_lo + epg), lg, NEG)
    f1 = jnp.max(fl, axis=-1, keepdims=True)
    i1 = jnp.min(jnp.where(fl == f1, lane, big), axis=-1, keepdims=True)
    fl2 = jnp.where(lane == i1, NEG, fl)
    f2 = jnp.max(fl2, axis=-1, keepdims=True)
    i2 = jnp.min(jnp.where(fl2 == f2, lane, big), axis=-1, keepdims=True)
    t2 = jnp.exp(f2 - f1)
    w0 = p_grp / (1.0 + t2)
    w1 = p_grp * t2 / (1.0 + t2)

    sel0 = lane == i1
    sel1 = lane == i2
    oh = jnp.where(sel0 | sel1, 1.0, 0.0)
    r_i = lax.broadcasted_iota(jnp.int32, (tm, tm), 0)
    c_i = lax.broadcasted_iota(jnp.int32, (tm, tm), 1)
    tri = jnp.where(c_i < r_i, 1.0, 0.0).astype(bf16)
    before = carry_ref[...] + _dot(tri, oh.astype(bf16))
    rank0 = jnp.sum(jnp.where(sel0, before, 0.0), axis=-1, keepdims=True)
    rank1 = jnp.sum(jnp.where(sel1, before, 0.0), axis=-1, keepdims=True)
    carry_ref[...] = carry_ref[...] + jnp.sum(oh, axis=0, keepdims=True)
    cnt_ref[...] = carry_ref[...]

    e0 = i1 - n_grp
    e1 = i2 - n_grp
    meta = jnp.zeros((tm, ROUTE_LANES), f32)
    for k, val in enumerate((e0, e1, w0, w1, rank0, rank1)):
        meta = jnp.where(lane == float(k), val, meta)
    meta_ref[...] = meta


def _post_call(x2, a3, b2, mod3, g1, g2, wg, bg, wps, wpp, wout, wr, br, seq, tm, n_grp, n_exp):
    t, d = x2.shape
    d_ssm = wps.shape[0]
    nl = a3.shape[0]
    per = seq // tm
    kern = functools.partial(_post_kernel, d=d, d_ssm=d_ssm, n_grp=n_grp, n_exp=n_exp, tm=tm)
    const2 = lambda i: (0, 0)
    return pl.pallas_call(
        kern,
        grid=(t // tm,),
        in_specs=[pl.BlockSpec((tm, d), lambda i: (i, 0)),
                  pl.BlockSpec((nl, tm, LANES), lambda i: (0, i, 0)),
                  pl.BlockSpec((tm, b2.shape[1]), lambda i: (i, 0)),
                  pl.BlockSpec((None, 1, mod3.shape[2]), lambda i: (i // per, 0, 0)),
                  pl.BlockSpec((1, d), const2),
                  pl.BlockSpec((1, d), const2),
                  pl.BlockSpec(wg.shape, const2),
                  pl.BlockSpec(bg.shape, const2),
                  pl.BlockSpec(wps.shape, const2),
                  pl.BlockSpec(wpp.shape, const2),
                  pl.BlockSpec(wout.shape, const2),
                  pl.BlockSpec(wr.shape, const2),
                  pl.BlockSpec(br.shape, const2)],
        out_specs=[pl.BlockSpec((tm, d), lambda i: (i, 0)),
                   pl.BlockSpec((tm * (d // 2 // LANES), LANES), lambda i: (i, 0)),
                   pl.BlockSpec((tm, ROUTE_LANES), lambda i: (i, 0)),
                   pl.BlockSpec((1, ROUTE_LANES), const2)],
        out_shape=(jax.ShapeDtypeStruct((t, d), f32),
                   jax.ShapeDtypeStruct((t * (d // 2 // LANES), LANES), jnp.uint32),
                   jax.ShapeDtypeStruct((t, ROUTE_LANES), f32),
                   jax.ShapeDtypeStruct((1, ROUTE_LANES), f32)),
        scratch_shapes=[pltpu.VMEM((1, ROUTE_LANES), f32)],
        compiler_params=_cparams(("arbitrary",)),
        name="post",
    )(x2, a3, b2, mod3, g1, g2, wg, bg, wps, wpp, wout, wr, br)


def _row_copy(src_ref, dst_ref, src_row, dst_row, p, sem):
    src = src_ref.at[pl.ds(pl.multiple_of(src_row * p, p), p)]
    dst = dst_ref.at[pl.ds(pl.multiple_of(dst_row * p, p), p)]
    return pltpu.make_async_copy(src, dst, sem)


def _wait_rows(src_ref, dst_ref, n_rows, p, sem):
    pltpu.make_async_copy(src_ref.at[pl.ds(0, n_rows * p)], dst_ref.at[pl.ds(0, n_rows * p)],
                          sem).wait()


def _scatter_kernel(pos_ref, up_ref, xs_ref, sem, *, tm, n_steps, p):
    i = pl.program_id(0)
    base = i * tm

    def issue(g, c):
        r0 = g * ROW_DMA_UNROLL
        idx = [[pos_ref[0, k, r0 + u] for k in range(TOP_K_FINE)] for u in range(ROW_DMA_UNROLL)]
        for u in range(ROW_DMA_UNROLL):
            for k in range(TOP_K_FINE):
                _row_copy(up_ref, xs_ref, base + r0 + u, idx[u][k], p, sem).start()
        return c

    lax.fori_loop(0, tm // ROW_DMA_UNROLL, issue, 0)

    def wait_step():
        for _ in range(TOP_K_FINE):
            _wait_rows(up_ref, xs_ref, tm, p, sem)

    pl.when(i > 0)(wait_step)
    pl.when(i == n_steps - 1)(wait_step)


def _scatter_call(up, pos3, n_rows, tm, p):
    n_steps = up.shape[0] // (tm * p)
    return pl.pallas_call(
        functools.partial(_scatter_kernel, tm=tm, n_steps=n_steps, p=p),
        grid=(n_steps,),
        in_specs=[pl.BlockSpec((1, TOP_K_FINE, tm), lambda i: (i, 0, 0), memory_space=pltpu.SMEM),
                  pl.BlockSpec(memory_space=pl.ANY)],
        out_specs=pl.BlockSpec(memory_space=pl.ANY),
        out_shape=jax.ShapeDtypeStruct((n_rows * p, LANES), up.dtype),
        scratch_shapes=[pltpu.SemaphoreType.DMA(())],
        compiler_params=_cparams(("arbitrary",)),
        name="scatter_rows",
    )(pos3, up)


def _experts_kernel(te_ref, nt_ref, xs_ref, wgu_ref, wd_ref, ys_ref, *, d_exp, tr, px):
    i = pl.program_id(0)

    @pl.when(i < nt_ref[0])
    def _():
        w = _load_panels(xs_ref, tr, px)
        hi = lax.bitcast_convert_type(w & jnp.uint32(0xFFFF0000), f32)
        lo = lax.bitcast_convert_type(w << 16, f32)
        x = jnp.concatenate([hi, lo], axis=1).astype(bf16)
        h = _dot(x, wgu_ref[...])
        h1 = h[:, 0:d_exp]
        act = h1 * jax.nn.sigmoid(h1) * h[:, d_exp:2 * d_exp]
        _store_panels(ys_ref, _dot(act.astype(bf16), wd_ref[...]), tr)


def _experts_call(tile_expert, n_tiles, xs, wgu, wd, tr):
    n_exp, d, two_de = wgu.shape
    d_exp = two_de // 2
    px = d // 2 // LANES
    py = d // LANES
    n_rows = xs.shape[0] // px
    nt_max = n_rows // tr

    def row_map(i, te, nt):
        return (jnp.minimum(i, nt[0] - 1), 0)

    def w_map(i, te, nt):
        return (te[jnp.minimum(i, nt[0] - 1)], 0, 0)

    grid_spec = pltpu.PrefetchScalarGridSpec(
        num_scalar_prefetch=2,
        grid=(nt_max,),
        in_specs=[pl.BlockSpec((tr * px, LANES), row_map),
                  pl.BlockSpec((None, d, two_de), w_map),
                  pl.BlockSpec((None, d_exp, d), w_map)],
        out_specs=pl.BlockSpec((tr * py, LANES), row_map),
    )
    return pl.pallas_call(
        functools.partial(_experts_kernel, d_exp=d_exp, tr=tr, px=px),
        grid_spec=grid_spec,
        out_shape=jax.ShapeDtypeStruct((n_rows * py, LANES), f32),
        compiler_params=_cparams(("arbitrary",)),
        name="experts",
    )(tile_expert, n_tiles, xs, wgu, wd)


def _combine_kernel(pos_ref, posn_ref, h_ref, meta_ref, mod_ref, gf_ref, ys_ref, o_ref,
                    g_ref, sem, *, d, tm, n_steps, final_norm):
    i = pl.program_id(0)
    slot = i % 2
    py = d // LANES

    def issue(p_ref, s):
        def body(g, c):
            r0 = g * ROW_DMA_UNROLL
            idx = [[p_ref[0, k, r0 + u] for k in range(TOP_K_FINE)] for u in range(ROW_DMA_UNROLL)]
            for u in range(ROW_DMA_UNROLL):
                for k in range(TOP_K_FINE):
                    _row_copy(ys_ref, g_ref.at[s, k], idx[u][k], r0 + u, py, sem.at[s]).start()
            return c

        lax.fori_loop(0, tm // ROW_DMA_UNROLL, body, 0)

    pl.when(i == 0)(lambda: issue(pos_ref, 0))
    pl.when(i + 1 < n_steps)(lambda: issue(posn_ref, 1 - slot))
    for k in range(TOP_K_FINE):
        _wait_rows(ys_ref, g_ref.at[slot, k], tm, py, sem.at[slot])

    gate2 = mod_ref[:, 5 * d:6 * d]
    w0 = meta_ref[:, 2:3]
    w1 = meta_ref[:, 3:4]
    y0 = _load_panels(g_ref.at[slot, 0], tm, py)
    y1 = _load_panels(g_ref.at[slot, 1], tm, py)
    h = h_ref[...] + gate2 * (w0 * y0 + w1 * y1)
    if final_norm:
        ms = jnp.mean(h * h, axis=-1, keepdims=True)
        h = h * lax.rsqrt(ms + EPS) * gf_ref[...]
    o_ref[...] = h


def _combine_call(pos3, h1, meta, mod3, gf, ys, seq, tm, final_norm):
    t, d = h1.shape
    per = seq // tm
    n_steps = t // tm
    pos_spec = lambda f: pl.BlockSpec((1, TOP_K_FINE, tm), f, memory_space=pltpu.SMEM)
    return pl.pallas_call(
        functools.partial(_combine_kernel, d=d, tm=tm, n_steps=n_steps, final_norm=final_norm),
        grid=(n_steps,),
        in_specs=[pos_spec(lambda i: (i, 0, 0)),
                  pos_spec(lambda i: (jnp.minimum(i + 1, n_steps - 1), 0, 0)),
                  pl.BlockSpec((tm, d), lambda i: (i, 0)),
                  pl.BlockSpec((tm, ROUTE_LANES), lambda i: (i, 0)),
                  pl.BlockSpec((None, 1, mod3.shape[2]), lambda i: (i // per, 0, 0)),
                  pl.BlockSpec((1, d), lambda i: (0, 0)),
                  pl.BlockSpec(memory_space=pl.ANY)],
        out_specs=pl.BlockSpec((tm, d), lambda i: (i, 0)),
        out_shape=jax.ShapeDtypeStruct((t, d), f32),
        scratch_shapes=[pltpu.VMEM((2, TOP_K_FINE, tm * (d // LANES), LANES), f32),
                        pltpu.SemaphoreType.DMA((2,))],
        compiler_params=_cparams(("arbitrary",)),
        name="combine",
    )(pos3, pos3, h1, meta, mod3, gf, ys)


def _row_tile(seq, target):
    tm = min(target, seq)
    assert seq % tm == 0 and tm % 8 == 0
    return tm


def _layer(h2, mod3, p, bsz, seq, final_gain, final_norm):
    t, d = h2.shape
    g, n_p, n_h = p["ssm_b_re"].shape
    d_ssm = g * n_h
    d_pool = p["pool_scale"].shape[-1]
    n_grp = p["router_coarse_w"].shape[-1]
    n_exp = p["router_fine_w"].shape[-1]
    assert n_grp + n_exp <= ROUTE_LANES // 2 and seq % SSM_SLABS == 0
    tm = _row_tile(seq, 512)

    pw, bbt_re, bbt_im = _ssm_prep_call(
        p["ssm_lam_re"], p["ssm_lam_im"], p["ssm_log_dt"],
        jnp.swapaxes(p["ssm_b_re"], 1, 2), jnp.swapaxes(p["ssm_b_im"], 1, 2), SSM_SLABS)
    bbd = jnp.concatenate([_block_diag(bbt_re), _block_diag(bbt_im)], axis=1).astype(bf16)
    cbd = jnp.concatenate([_block_diag(jnp.swapaxes(p["ssm_c_re"], 1, 2)),
                           -_block_diag(jnp.swapaxes(p["ssm_c_im"], 1, 2))], axis=0).astype(bf16)
    pw2 = pw.reshape(SSM_SLABS, 2 * g * n_p)

    w_in, b_in = p["w_in"], p["b_in"]
    n_sp = d_ssm + d_pool
    us, upool = _in_proj_call(h2, mod3, p["norm1_g"].reshape(1, d), w_in[:, :n_sp].astype(bf16),
                              b_in[:n_sp].reshape(1, n_sp), seq, tm, d_ssm)
    a3, b2 = _mixers_call(us, upool, bbd, cbd, pw2, p["ssm_d"].reshape(1, d_ssm),
                          p["ssm_w_glu"].astype(bf16), p["ssm_b_glu"].reshape(1, d_ssm),
                          p["pool_w"].astype(bf16), p["pool_scale"].reshape(1, d_pool),
                          bsz, seq)

    wr = jnp.concatenate([p["router_coarse_w"], p["router_fine_w"]], axis=1)
    wr_hi = wr.astype(bf16)
    wr_lo = (wr - wr_hi.astype(f32)).astype(bf16)
    padc = ROUTE_LANES // 2 - wr.shape[1]
    wr_cat = jnp.concatenate([jnp.pad(wr_hi, ((0, 0), (0, padc))),
                              jnp.pad(wr_lo, ((0, 0), (0, padc)))], axis=1)
    br = jnp.pad(jnp.concatenate([p["router_coarse_b"], p["router_fine_b"]]),
                 (0, ROUTE_LANES - wr.shape[1])).reshape(1, ROUTE_LANES)

    h1, up, meta, cnt = _post_call(
        h2, a3, b2, mod3, p["norm1_g"].reshape(1, d), p["norm2_g"].reshape(1, d),
        w_in[:, n_sp:].astype(bf16), b_in[n_sp:].reshape(1, 2 * d),
        p["w_proj_ssm"].astype(bf16), p["w_proj_pool"].astype(bf16), p["w_out"].astype(bf16),
        wr_cat, br, seq, tm, n_grp, n_exp)

    tr = 512
    counts = cnt[0, n_grp:n_grp + n_exp].astype(jnp.int32)
    padded = ((counts + tr - 1) // tr) * tr
    ends = jnp.cumsum(padded)
    offs = ends - padded
    ids = jnp.arange(n_exp, dtype=jnp.int32)

    def sorted_pos(k):
        e = meta[:, k].astype(jnp.int32)
        off = jnp.sum(jnp.where(e[:, None] == ids[None, :], offs[None, :], 0), axis=1)
        return (off + meta[:, 4 + k].astype(jnp.int32)).reshape(t // tm, tm)

    pos3 = jnp.stack([sorted_pos(k) for k in range(TOP_K_FINE)], axis=1)
    nt_max = (TOP_K_FINE * t) // tr + n_exp
    n_tiles = (ends[-1] // tr).astype(jnp.int32).reshape(1)
    tile_start = jnp.arange(nt_max, dtype=jnp.int32) * tr
    tile_expert = jnp.minimum(
        jnp.sum((ends[None, :] <= tile_start[:, None]).astype(jnp.int32), axis=1), n_exp - 1)

    xs = _scatter_call(up, pos3, nt_max * tr, tm, d // 2 // LANES)
    wgu = jnp.concatenate([p["moe_w_gate"], p["moe_w_up"]], axis=-1).astype(bf16)
    ys = _experts_call(tile_expert, n_tiles, xs, wgu, p["moe_w_down"].astype(bf16), tr)
    return _combine_call(pos3, h1, meta, mod3, final_gain.reshape(1, d), ys, seq, tm, final_norm)


def kernel(x, c, w_mod, b_mod, norm1_g, w_in, b_in, ssm_lam_re, ssm_lam_im, ssm_log_dt, ssm_b_re, ssm_b_im, ssm_c_re, ssm_c_im, ssm_d, ssm_w_glu, ssm_b_glu, pool_w, pool_scale, w_proj_ssm, w_proj_pool, w_out, norm2_g, router_coarse_w, router_coarse_b, router_fine_w, router_fine_b, moe_w_gate, moe_w_up, moe_w_down, norm_f_g):
    bsz, seq, d = x.shape
    depth = w_mod.shape[0]
    per_layer = dict(
        norm1_g=norm1_g, w_in=w_in, b_in=b_in, ssm_lam_re=ssm_lam_re, ssm_lam_im=ssm_lam_im,
        ssm_log_dt=ssm_log_dt, ssm_b_re=ssm_b_re, ssm_b_im=ssm_b_im, ssm_c_re=ssm_c_re,
        ssm_c_im=ssm_c_im, ssm_d=ssm_d, ssm_w_glu=ssm_w_glu, ssm_b_glu=ssm_b_glu, pool_w=pool_w,
        pool_scale=pool_scale, w_proj_ssm=w_proj_ssm, w_proj_pool=w_proj_pool, w_out=w_out,
        norm2_g=norm2_g, router_coarse_w=router_coarse_w, router_coarse_b=router_coarse_b,
        router_fine_w=router_fine_w, router_fine_b=router_fine_b, moe_w_gate=moe_w_gate,
        moe_w_up=moe_w_up, moe_w_down=moe_w_down)
    h2 = x.reshape(bsz * seq, d)
    for l in range(depth):
        p = {k: v[l] for k, v in per_layer.items()}
        mod3 = _mod_call(c, w_mod[l], b_mod[l]).reshape(bsz, 1, N_MOD * d)
        h2 = _layer(h2, mod3, p, bsz, seq, norm_f_g, final_norm=(l == depth - 1))
    return h2.reshape(bsz, seq, d)
```

```python
import functools
import math

import jax
import jax.numpy as jnp
from jax import lax
from jax.experimental import pallas as pl
from jax.experimental.pallas import tpu as pltpu

EPS = 1e-6
POOL_WINDOWS = (2, 4, 8, 16)
TOP_K_FINE = 2
N_MOD = 6
LANES = 128
SSM_SLABS = 8
ROUTE_LANES = 128
ROW_DMA_UNROLL = 8
VMEM_LIMIT = 56 * 1024 * 1024
NEG = -1e30

f32 = jnp.float32
bf16 = jnp.bfloat16


def _cparams(sem):
    return pltpu.CompilerParams(dimension_semantics=sem, vmem_limit_bytes=VMEM_LIMIT)


def _dot(a, b):
    return jnp.dot(a, b, preferred_element_type=f32)


def _store_panels(ref, val, rows):
    p = val.shape[1] // LANES
    for j in range(p):
        ref[pl.ds(j, rows, stride=p), :] = val[:, j * LANES:(j + 1) * LANES]


def _load_panels(ref, rows, p):
    return jnp.concatenate([ref[pl.ds(j, rows, stride=p), :] for j in range(p)], axis=1)


def _split_bf16(v):
    hi = v.astype(bf16)
    return hi, (v - hi.astype(f32)).astype(bf16)


def _mod_kernel(c_ref, w_ref, b_ref, o_ref):
    c = c_ref[...]
    a_hi, a_lo = _split_bf16(c * jax.nn.sigmoid(c))
    w_hi, w_lo = _split_bf16(w_ref[...])
    o_ref[...] = _dot(a_hi, w_hi) + (_dot(a_lo, w_hi) + _dot(a_hi, w_lo)) + b_ref[...]


def _mod_call(c, w_mod, b_mod):
    bsz, d = c.shape
    n = w_mod.shape[1]
    tn = d
    return pl.pallas_call(
        _mod_kernel,
        grid=(n // tn,),
        in_specs=[pl.BlockSpec((bsz, d), lambda j: (0, 0)),
                  pl.BlockSpec((d, tn), lambda j: (0, j)),
                  pl.BlockSpec((1, tn), lambda j: (0, j))],
        out_specs=pl.BlockSpec((bsz, tn), lambda j: (0, j)),
        out_shape=jax.ShapeDtypeStruct((bsz, n), f32),
        compiler_params=_cparams(("parallel",)),
        name="mod",
    )(c, w_mod, b_mod.reshape(1, n))


def _ssm_prep_kernel(lr_ref, li_ref, ldt_ref, btr_ref, bti_ref, pw_ref, bbr_ref, bbi_ref,
                     *, n_pow, n_grp):
    lr = lr_ref[...]
    li = li_ref[...]
    dt = jnp.exp(ldt_ref[...])
    for k in range(1, n_pow + 1):
        mag = jnp.exp(lr * dt * float(k))
        ang = li * dt * float(k)
        pw_ref[k - 1, 0] = mag * jnp.cos(ang)
        pw_ref[k - 1, 1] = mag * jnp.sin(ang)
    lb_re = pw_ref[0, 0]
    lb_im = pw_ref[0, 1]
    den = lr * lr + li * li
    nr = lb_re - 1.0
    f_re = (nr * lr + lb_im * li) / den
    f_im = (lb_im * lr - nr * li) / den
    for g in range(n_grp):
        fr, fi = f_re[g:g + 1, :], f_im[g:g + 1, :]
        bbr_ref[g] = fr * btr_ref[g] - fi * bti_ref[g]
        bbi_ref[g] = fr * bti_ref[g] + fi * btr_ref[g]


def _ssm_prep_call(lam_re, lam_im, log_dt, bt_re, bt_im, n_pow):
    g, p = lam_re.shape
    return pl.pallas_call(
        functools.partial(_ssm_prep_kernel, n_pow=n_pow, n_grp=g),
        out_shape=(jax.ShapeDtypeStruct((n_pow, 2, g, p), f32),
                   jax.ShapeDtypeStruct(bt_re.shape, f32),
                   jax.ShapeDtypeStruct(bt_re.shape, f32)),
        name="ssm_prep",
    )(lam_re, lam_im, log_dt.reshape(g, 1), bt_re, bt_im)


def _block_diag(blocks):
    g, a, b = blocks.shape
    eye = jnp.eye(g, dtype=blocks.dtype)
    return (eye[:, None, :, None] * blocks[:, :, None, :]).reshape(g * a, g * b)


def _modulated_norm(x, gain, shift, scale):
    ms = jnp.mean(x * x, axis=-1, keepdims=True)
    y = x * lax.rsqrt(ms + EPS) * gain
    return y * (1.0 + scale) + shift


def _in_proj_kernel(x_ref, mod_ref, g_ref, w_ref, b_ref, us_ref, up_ref, *, d, d_ssm):
    u = _modulated_norm(x_ref[...], g_ref[...], mod_ref[:, 0:d], mod_ref[:, d:2 * d])
    r = _dot(u.astype(bf16), w_ref[...]) + b_ref[...]
    for j in range(d_ssm // LANES):
        us_ref[j] = r[:, j * LANES:(j + 1) * LANES]
    up_ref[...] = r[:, d_ssm:]


def _in_proj_call(x2, mod3, g1, w_sp, b_sp, seq, tm, d_ssm):
    t, d = x2.shape
    n = w_sp.shape[1]
    per = seq // tm
    nl = d_ssm // LANES
    return pl.pallas_call(
        functools.partial(_in_proj_kernel, d=d, d_ssm=d_ssm),
        grid=(t // tm,),
        in_specs=[pl.BlockSpec((tm, d), lambda i: (i, 0)),
                  pl.BlockSpec((None, 1, mod3.shape[2]), lambda i: (i // per, 0, 0)),
                  pl.BlockSpec((1, d), lambda i: (0, 0)),
                  pl.BlockSpec((d, n), lambda i: (0, 0)),
                  pl.BlockSpec((1, n), lambda i: (0, 0))],
        out_specs=[pl.BlockSpec((nl, tm, LANES), lambda i: (0, i, 0)),
                   pl.BlockSpec((tm, n - d_ssm), lambda i: (i, 0))],
        out_shape=(jax.ShapeDtypeStruct((nl, t, LANES), f32),
                   jax.ShapeDtypeStruct((t, n - d_ssm), f32)),
        compiler_params=_cparams(("parallel",)),
        name="in_proj",
    )(x2, mod3, g1, w_sp, b_sp)


def _cmul_add(ar, ai, xr, xi, br, bi):
    return ar * xr - ai * xi + br, ar * xi + ai * xr + bi


def _mixers_kernel(us_ref, up_ref, bbd_ref, cbd_ref, pw_ref, dsk_ref, wglu_ref, bglu_ref,
                   pw_pool_ref, psc_ref, a_ref, b_ref, e_ref, s_ref, y_ref,
                   *, seq, d_ssm, d_pool, n_state, slabs):
    nc = seq // slabs
    nl = d_ssm // LANES
    ns = n_state
    dsk = dsk_ref[...]
    l1r = pw_ref[0:1, 0:ns]
    l1i = pw_ref[0:1, ns:2 * ns]

    xr = xi = None
    for t in range(slabs):
        ut = jnp.concatenate([us_ref[j, pl.ds(t, nc, stride=slabs), :] for j in range(nl)], axis=1)
        bu = _dot(ut.astype(bf16), bbd_ref[...])
        br, bi = bu[:, 0:ns], bu[:, ns:2 * ns]
        if t == 0:
            xr, xi = br, bi
        else:
            xr, xi = _cmul_add(l1r, l1i, xr, xi, br, bi)
        xc = jnp.concatenate([xr, xi], axis=1).astype(bf16)
        y_ref[t] = _dot(xc, cbd_ref[...]) + dsk * ut
    e_ref[:, 0:ns] = xr
    e_ref[:, ns:2 * ns] = xi

    lLr = pw_ref[slabs - 1:slabs, 0:ns]
    lLi = pw_ref[slabs - 1:slabs, ns:2 * ns]

    def chunk_step(c, carry):
        sr, si = carry
        s_ref[pl.ds(c, 1), 0:ns] = sr
        s_ref[pl.ds(c, 1), ns:2 * ns] = si
        er = e_ref[pl.ds(c, 1), 0:ns]
        ei = e_ref[pl.ds(c, 1), ns:2 * ns]
        return _cmul_add(lLr, lLi, sr, si, er, ei)

    zero = jnp.zeros((1, ns), f32)
    lax.fori_loop(0, nc, chunk_step, (zero, zero))

    sr = s_ref[:, 0:ns]
    si = s_ref[:, ns:2 * ns]
    for t in range(slabs):
        pr = pw_ref[t:t + 1, 0:ns]
        pi = pw_ref[t:t + 1, ns:2 * ns]
        zr = pr * sr - pi * si
        zi = pr * si + pi * sr
        zc = jnp.concatenate([zr, zi], axis=1).astype(bf16)
        y = y_ref[t] + _dot(zc, cbd_ref[...])
        y = jax.nn.gelu(y)
        z = y * jax.nn.sigmoid(_dot(y.astype(bf16), wglu_ref[...]) + bglu_ref[...])
        for j in range(nl):
            a_ref[j, pl.ds(t, nc, stride=slabs), :] = z[:, j * LANES:(j + 1) * LANES]

    gc = d_pool // len(POOL_WINDOWS)
    row = lax.broadcasted_iota(jnp.int32, (seq, gc), 0)
    for gi, w in enumerate(POOL_WINDOWS):
        lo = gi * gc
        v = up_ref[:, lo:lo + gc]
        acc = v
        span = 1
        while span < w:
            acc = acc + jnp.where(row >= span, pltpu.roll(acc, span, axis=0), 0.0)
            span *= 2
        cnt = jnp.minimum(row + 1, w).astype(f32)
        m = acc / cnt - v
        yg = _dot(m.astype(bf16), pw_pool_ref[gi])
        b_ref[:, lo:lo + gc] = yg * psc_ref[:, lo:lo + gc]


def _mixers_call(us, up, bbd, cbd, pw, dsk, wglu, bglu, pool_w, pool_scale, bsz, seq):
    nl, t, _ = us.shape
    d_ssm = nl * LANES
    d_pool = up.shape[1]
    ns2 = bbd.shape[1]
    slabs = SSM_SLABS
    nc = seq // slabs
    kern = functools.partial(_mixers_kernel, seq=seq, d_ssm=d_ssm, d_pool=d_pool,
                             n_state=ns2 // 2, slabs=slabs)
    const2 = lambda b: (0, 0)
    return pl.pallas_call(
        kern,
        grid=(bsz,),
        in_specs=[pl.BlockSpec((nl, seq, LANES), lambda b: (0, b, 0)),
                  pl.BlockSpec((seq, d_pool), lambda b: (b, 0)),
                  pl.BlockSpec(bbd.shape, const2),
                  pl.BlockSpec(cbd.shape, const2),
                  pl.BlockSpec(pw.shape, const2),
                  pl.BlockSpec(dsk.shape, const2),
                  pl.BlockSpec(wglu.shape, const2),
                  pl.BlockSpec(bglu.shape, const2),
                  pl.BlockSpec(pool_w.shape, lambda b: (0, 0, 0)),
                  pl.BlockSpec(pool_scale.shape, const2)],
        out_specs=[pl.BlockSpec((nl, seq, LANES), lambda b: (0, b, 0)),
                   pl.BlockSpec((seq, d_pool), lambda b: (b, 0))],
        out_shape=(jax.ShapeDtypeStruct((nl, t, LANES), f32),
                   jax.ShapeDtypeStruct((t, d_pool), f32)),
        scratch_shapes=[pltpu.VMEM((nc, ns2), f32),
                        pltpu.VMEM((nc, ns2), f32),
                        pltpu.VMEM((slabs, nc, d_ssm), f32)],
        compiler_params=_cparams(("parallel",)),
        name="mixers",
    )(us, up, bbd, cbd, pw, dsk, wglu, bglu, pool_w, pool_scale)


def _post_kernel(x_ref, a_ref, b_ref, mod_ref, g1_ref, g2_ref, wg_ref, bg_ref, wps_ref, wpp_ref,
                 wout_ref, wr_ref, br_ref,
                 h_ref, up_ref, meta_ref, metat_ref, cnt_ref, carry_ref,
                 *, d, d_ssm, n_grp, n_exp, tm):
    i = pl.program_id(0)

    @pl.when(i == 0)
    def _():
        carry_ref[...] = jnp.zeros_like(carry_ref)

    x = x_ref[...]
    shift1, scale1, gate1 = mod_ref[:, 0:d], mod_ref[:, d:2 * d], mod_ref[:, 2 * d:3 * d]
    shift2, scale2 = mod_ref[:, 3 * d:4 * d], mod_ref[:, 4 * d:5 * d]
    u = _modulated_norm(x, g1_ref[...], shift1, scale1).astype(bf16)
    gates = _dot(u, wg_ref[...]) + bg_ref[...]
    a = jnp.concatenate([a_ref[j] for j in range(d_ssm // LANES)], axis=1).astype(bf16)
    b = b_ref[...].astype(bf16)
    merged = (jax.nn.sigmoid(gates[:, 0:d]) * _dot(a, wps_ref[...])
              + jax.nn.sigmoid(gates[:, d:2 * d]) * _dot(b, wpp_ref[...]))
    h = x + gate1 * _dot(merged.astype(bf16), wout_ref[...])
    h_ref[...] = h

    u2 = _modulated_norm(h, g2_ref[...], shift2, scale2)

    half = d // 2
    hi_bits = lax.bitcast_convert_type(u2[:, 0:half].astype(bf16).astype(f32), jnp.uint32)
    lo_bits = lax.bitcast_convert_type(u2[:, half:d].astype(bf16).astype(f32), jnp.uint32)
    _store_panels(up_ref, hi_bits | (lo_bits >> 16), tm)

    u_hi = u2.astype(bf16)
    u_lo = (u2 - u_hi.astype(f32)).astype(bf16)
    r1 = _dot(u_hi, wr_ref[...])
    r2 = _dot(u_lo, wr_ref[...])
    lg = r1 + pltpu.roll(r1, ROUTE_LANES // 2, axis=1) + r2 + br_ref[...]

    lane = lax.broadcasted_iota(jnp.int32, (tm, ROUTE_LANES), 1).astype(f32)
    big = float(ROUTE_LANES)
    epg = float(n_exp // n_grp)
    is_c = lane < n_grp
    cl = jnp.where(is_c, lg, NEG)
    cmax = jnp.max(cl, axis=-1, keepdims=True)
    grp = jnp.min(jnp.where(cl == cmax, lane, big), axis=-1, keepdims=True)
    p_grp = 1.0 / jnp.sum(jnp.where(is_c, jnp.exp(cl - cmax), 0.0), axis=-1, keepdims=True)

    f_lo = n_grp + grp * epg
    fl = jnp.where((lane >= f_lo) & (lane < f_lo + epg), lg, NEG)
    f1 = jnp.max(fl, axis=-1, keepdims=True)
    i1 = jnp.min(jnp.where(fl == f1, lane, big), axis=-1, keepdims=True)
    fl2 = jnp.where(lane == i1, NEG, fl)
    f2 = jnp.max(fl2, axis=-1, keepdims=True)
    i2 = jnp.min(jnp.where(fl2 == f2, lane, big), axis=-1, keepdims=True)
    t2 = jnp.exp(f2 - f1)
    w0 = p_grp / (1.0 + t2)
    w1 = p_grp * t2 / (1.0 + t2)

    sel0 = lane == i1
    sel1 = lane == i2
    oh = jnp.where(sel0 | sel1, 1.0, 0.0)
    r_i = lax.broadcasted_iota(jnp.int32, (tm, tm), 0)
    c_i = lax.broadcasted_iota(jnp.int32, (tm, tm), 1)
    tri = jnp.where(c_i < r_i, 1.0, 0.0).astype(bf16)
    before = carry_ref[...] + _dot(tri, oh.astype(bf16))
    rank0 = jnp.sum(jnp.where(sel0, before, 0.0), axis=-1, keepdims=True)
    rank1 = jnp.sum(jnp.where(sel1, before, 0.0), axis=-1, keepdims=True)
    carry_ref[...] = carry_ref[...] + jnp.sum(oh, axis=0, keepdims=True)
    cnt_ref[...] = carry_ref[...]

    e0 = i1 - n_grp
    e1 = i2 - n_grp
    meta = jnp.zeros((tm, ROUTE_LANES), f32)
    for k, val in enumerate((e0, e1, w0, w1, rank0, rank1)):
        meta = jnp.where(lane == float(k), val, meta)
    meta_ref[...] = meta
    metat_ref[...] = meta.T[0:8, :]


def _post_call(x2, a3, b2, mod3, g1, g2, wg, bg, wps, wpp, wout, wr, br, seq, tm, n_grp, n_exp):
    t, d = x2.shape
    d_ssm = wps.shape[0]
    nl = a3.shape[0]
    per = seq // tm
    kern = functools.partial(_post_kernel, d=d, d_ssm=d_ssm, n_grp=n_grp, n_exp=n_exp, tm=tm)
    const2 = lambda i: (0, 0)
    return pl.pallas_call(
        kern,
        grid=(t // tm,),
        in_specs=[pl.BlockSpec((tm, d), lambda i: (i, 0)),
                  pl.BlockSpec((nl, tm, LANES), lambda i: (0, i, 0)),
                  pl.BlockSpec((tm, b2.shape[1]), lambda i: (i, 0)),
                  pl.BlockSpec((None, 1, mod3.shape[2]), lambda i: (i // per, 0, 0)),
                  pl.BlockSpec((1, d), const2),
                  pl.BlockSpec((1, d), const2),
                  pl.BlockSpec(wg.shape, const2),
                  pl.BlockSpec(bg.shape, const2),
                  pl.BlockSpec(wps.shape, const2),
                  pl.BlockSpec(wpp.shape, const2),
                  pl.BlockSpec(wout.shape, const2),
                  pl.BlockSpec(wr.shape, const2),
                  pl.BlockSpec(br.shape, const2)],
        out_specs=[pl.BlockSpec((tm, d), lambda i: (i, 0)),
                   pl.BlockSpec((tm * (d // 2 // LANES), LANES), lambda i: (i, 0)),
                   pl.BlockSpec((tm, ROUTE_LANES), lambda i: (i, 0)),
                   pl.BlockSpec((8, tm), lambda i: (0, i)),
                   pl.BlockSpec((1, ROUTE_LANES), const2)],
        out_shape=(jax.ShapeDtypeStruct((t, d), f32),
                   jax.ShapeDtypeStruct((t * (d // 2 // LANES), LANES), jnp.uint32),
                   jax.ShapeDtypeStruct((t, ROUTE_LANES), f32),
                   jax.ShapeDtypeStruct((8, t), f32),
                   jax.ShapeDtypeStruct((1, ROUTE_LANES), f32)),
        scratch_shapes=[pltpu.VMEM((1, ROUTE_LANES), f32)],
        compiler_params=_cparams(("arbitrary",)),
        name="post",
    )(x2, a3, b2, mod3, g1, g2, wg, bg, wps, wpp, wout, wr, br)


def _row_copy(src_ref, dst_ref, src_row, dst_row, p, sem):
    src = src_ref.at[pl.ds(pl.multiple_of(src_row * p, p), p)]
    dst = dst_ref.at[pl.ds(pl.multiple_of(dst_row * p, p), p)]
    return pltpu.make_async_copy(src, dst, sem)


def _wait_rows(src_ref, dst_ref, n_rows, p, sem):
    pltpu.make_async_copy(src_ref.at[pl.ds(0, n_rows * p)], dst_ref.at[pl.ds(0, n_rows * p)],
                          sem).wait()


def _scatter_kernel(pos_ref, up_ref, xs_ref, sem, *, tm, p):
    def issue(g, c):
        r0 = g * ROW_DMA_UNROLL
        idx = [[pos_ref[0, k, r0 + u] for k in range(TOP_K_FINE)] for u in range(ROW_DMA_UNROLL)]
        for u in range(ROW_DMA_UNROLL):
            for k in range(TOP_K_FINE):
                _row_copy(up_ref, xs_ref, r0 + u, idx[u][k], p, sem).start()
        return c

    lax.fori_loop(0, tm // ROW_DMA_UNROLL, issue, 0)
    for _ in range(TOP_K_FINE):
        _wait_rows(up_ref, xs_ref, tm, p, sem)


def _scatter_call(up, pos3, n_rows, tm, p):
    n_steps = up.shape[0] // (tm * p)
    return pl.pallas_call(
        functools.partial(_scatter_kernel, tm=tm, p=p),
        grid=(n_steps,),
        in_specs=[pl.BlockSpec((1, TOP_K_FINE, tm), lambda i: (i, 0, 0), memory_space=pltpu.SMEM),
                  pl.BlockSpec((tm * p, LANES), lambda i: (i, 0))],
        out_specs=pl.BlockSpec(memory_space=pl.ANY),
        out_shape=jax.ShapeDtypeStruct((n_rows * p, LANES), up.dtype),
        scratch_shapes=[pltpu.SemaphoreType.DMA(())],
        compiler_params=_cparams(("arbitrary",)),
        name="scatter_rows",
    )(pos3, up)


def _experts_kernel(te_ref, nt_ref, xs_ref, wgu_ref, wd_ref, ys_ref, *, d_exp, tr, px):
    i = pl.program_id(0)

    @pl.when(i < nt_ref[0])
    def _():
        w = _load_panels(xs_ref, tr, px)
        hi = lax.bitcast_convert_type(w & jnp.uint32(0xFFFF0000), f32)
        lo = lax.bitcast_convert_type(w << 16, f32)
        x = jnp.concatenate([hi, lo], axis=1).astype(bf16)
        h = _dot(x, wgu_ref[...])
        h1 = h[:, 0:d_exp]
        act = h1 * jax.nn.sigmoid(h1) * h[:, d_exp:2 * d_exp]
        _store_panels(ys_ref, _dot(act.astype(bf16), wd_ref[...]), tr)


def _experts_call(tile_expert, n_tiles, xs, wgu, wd, tr):
    n_exp, d, two_de = wgu.shape
    d_exp = two_de // 2
    px = d // 2 // LANES
    py = d // LANES
    n_rows = xs.shape[0] // px
    nt_max = n_rows // tr

    def row_map(i, te, nt):
        return (jnp.minimum(i, nt[0] - 1), 0)

    def w_map(i, te, nt):
        return (te[jnp.minimum(i, nt[0] - 1)], 0, 0)

    grid_spec = pltpu.PrefetchScalarGridSpec(
        num_scalar_prefetch=2,
        grid=(nt_max,),
        in_specs=[pl.BlockSpec((tr * px, LANES), row_map),
                  pl.BlockSpec((None, d, two_de), w_map),
                  pl.BlockSpec((None, d_exp, d), w_map)],
        out_specs=pl.BlockSpec((tr * py, LANES), row_map),
    )
    return pl.pallas_call(
        functools.partial(_experts_kernel, d_exp=d_exp, tr=tr, px=px),
        grid_spec=grid_spec,
        out_shape=jax.ShapeDtypeStruct((n_rows * py, LANES), f32),
        compiler_params=_cparams(("arbitrary",)),
        name="experts",
    )(tile_expert, n_tiles, xs, wgu, wd)


def _combine_kernel(pos_ref, posn_ref, h_ref, meta_ref, mod_ref, gf_ref, ys_ref, o_ref,
                    g_ref, sem, *, d, tm, n_steps, final_norm):
    i = pl.program_id(0)
    slot = i % 2
    py = d // LANES

    def issue(p_ref, s):
        def body(g, c):
            r0 = g * ROW_DMA_UNROLL
            idx = [[p_ref[0, k, r0 + u] for k in range(TOP_K_FINE)] for u in range(ROW_DMA_UNROLL)]
            for u in range(ROW_DMA_UNROLL):
                for k in range(TOP_K_FINE):
                    _row_copy(ys_ref, g_ref.at[s, k], idx[u][k], r0 + u, py, sem.at[s]).start()
            return c

        lax.fori_loop(0, tm // ROW_DMA_UNROLL, body, 0)

    pl.when(i == 0)(lambda: issue(pos_ref, 0))
    pl.when(i + 1 < n_steps)(lambda: issue(posn_ref, 1 - slot))
    for k in range(TOP_K_FINE):
        _wait_rows(ys_ref, g_ref.at[slot, k], tm, py, sem.at[slot])

    gate2 = mod_ref[:, 5 * d:6 * d]
    w0 = meta_ref[:, 2:3]
    w1 = meta_ref[:, 3:4]
    y0 = _load_panels(g_ref.at[slot, 0], tm, py)
    y1 = _load_panels(g_ref.at[slot, 1], tm, py)
    h = h_ref[...] + gate2 * (w0 * y0 + w1 * y1)
    if final_norm:
        ms = jnp.mean(h * h, axis=-1, keepdims=True)
        h = h * lax.rsqrt(ms + EPS) * gf_ref[...]
    o_ref[...] = h


def _combine_call(pos3, h1, meta, mod3, gf, ys, seq, tm, final_norm):
    t, d = h1.shape
    per = seq // tm
    n_steps = t // tm
    pos_spec = lambda f: pl.BlockSpec((1, TOP_K_FINE, tm), f, memory_space=pltpu.SMEM)
    return pl.pallas_call(
        functools.partial(_combine_kernel, d=d, tm=tm, n_steps=n_steps, final_norm=final_norm),
        grid=(n_steps,),
        in_specs=[pos_spec(lambda i: (i, 0, 0)),
                  pos_spec(lambda i: (jnp.minimum(i + 1, n_steps - 1), 0, 0)),
                  pl.BlockSpec((tm, d), lambda i: (i, 0)),
                  pl.BlockSpec((tm, ROUTE_LANES), lambda i: (i, 0)),
                  pl.BlockSpec((None, 1, mod3.shape[2]), lambda i: (i // per, 0, 0)),
                  pl.BlockSpec((1, d), lambda i: (0, 0)),
                  pl.BlockSpec(memory_space=pl.ANY)],
        out_specs=pl.BlockSpec((tm, d), lambda i: (i, 0)),
        out_shape=jax.ShapeDtypeStruct((t, d), f32),
        scratch_shapes=[pltpu.VMEM((2, TOP_K_FINE, tm * (d // LANES), LANES), f32),
                        pltpu.SemaphoreType.DMA((2,))],
        compiler_params=_cparams(("arbitrary",)),
        name="combine",
    )(pos3, pos3, h1, meta, mod3, gf, ys)


def _row_tile(seq, target):
    tm = min(target, seq)
    assert seq % tm == 0 and tm % 8 == 0
    return tm


def _layer(h2, mod3, p, bsz, seq, final_gain, final_norm):
    t, d = h2.shape
    g, n_p, n_h = p["ssm_b_re"].shape
    d_ssm = g * n_h
    d_pool = p["pool_scale"].shape[-1]
    n_grp = p["router_coarse_w"].shape[-1]
    n_exp = p["router_fine_w"].shape[-1]
    assert n_grp + n_exp <= ROUTE_LANES // 2 and seq % SSM_SLABS == 0
    tm = _row_tile(seq, 512)

    pw, bbt_re, bbt_im = _ssm_prep_call(
        p["ssm_lam_re"], p["ssm_lam_im"], p["ssm_log_dt"],
        jnp.swapaxes(p["ssm_b_re"], 1, 2), jnp.swapaxes(p["ssm_b_im"], 1, 2), SSM_SLABS)
    bbd = jnp.concatenate([_block_diag(bbt_re), _block_diag(bbt_im)], axis=1).astype(bf16)
    cbd = jnp.concatenate([_block_diag(jnp.swapaxes(p["ssm_c_re"], 1, 2)),
                           -_block_diag(jnp.swapaxes(p["ssm_c_im"], 1, 2))], axis=0).astype(bf16)
    pw2 = pw.reshape(SSM_SLABS, 2 * g * n_p)

    w_in, b_in = p["w_in"], p["b_in"]
    n_sp = d_ssm + d_pool
    us, upool = _in_proj_call(h2, mod3, p["norm1_g"].reshape(1, d), w_in[:, :n_sp].astype(bf16),
                              b_in[:n_sp].reshape(1, n_sp), seq, tm, d_ssm)
    a3, b2 = _mixers_call(us, upool, bbd, cbd, pw2, p["ssm_d"].reshape(1, d_ssm),
                          p["ssm_w_glu"].astype(bf16), p["ssm_b_glu"].reshape(1, d_ssm),
                          p["pool_w"].astype(bf16), p["pool_scale"].reshape(1, d_pool),
                          bsz, seq)

    wr = jnp.concatenate([p["router_coarse_w"], p["router_fine_w"]], axis=1)
    wr_hi = wr.astype(bf16)
    wr_lo = (wr - wr_hi.astype(f32)).astype(bf16)
    padc = ROUTE_LANES // 2 - wr.shape[1]
    wr_cat = jnp.concatenate([jnp.pad(wr_hi, ((0, 0), (0, padc))),
                              jnp.pad(wr_lo, ((0, 0), (0, padc)))], axis=1)
    br = jnp.pad(jnp.concatenate([p["router_coarse_b"], p["router_fine_b"]]),
                 (0, ROUTE_LANES - wr.shape[1])).reshape(1, ROUTE_LANES)

    h1, up, meta, metat, cnt = _post_call(
        h2, a3, b2, mod3, p["norm1_g"].reshape(1, d), p["norm2_g"].reshape(1, d),
        w_in[:, n_sp:].astype(bf16), b_in[n_sp:].reshape(1, 2 * d),
        p["w_proj_ssm"].astype(bf16), p["w_proj_pool"].astype(bf16), p["w_out"].astype(bf16),
        wr_cat, br, seq, tm, n_grp, n_exp)

    tr = 512
    counts = cnt[0, n_grp:n_grp + n_exp].astype(jnp.int32)
    padded = ((counts + tr - 1) // tr) * tr
    ends = jnp.cumsum(padded)
    offs = ends - padded
    ids = jnp.arange(n_exp, dtype=jnp.int32)

    def sorted_pos(k):
        e = metat[k].astype(jnp.int32)
        off = jnp.sum(jnp.where(e[None, :] == ids[:, None], offs[:, None], 0), axis=0)
        return (off + metat[4 + k].astype(jnp.int32)).reshape(t // tm, tm)

    pos3 = jnp.stack([sorted_pos(k) for k in range(TOP_K_FINE)], axis=1)
    nt_max = (TOP_K_FINE * t) // tr + n_exp
    n_tiles = (ends[-1] // tr).astype(jnp.int32).reshape(1)
    tile_start = jnp.arange(nt_max, dtype=jnp.int32) * tr
    tile_expert = jnp.minimum(
        jnp.sum((ends[None, :] <= tile_start[:, None]).astype(jnp.int32), axis=1), n_exp - 1)

    xs = _scatter_call(up, pos3, nt_max * tr, tm, d // 2 // LANES)
    wgu = jnp.concatenate([p["moe_w_gate"], p["moe_w_up"]], axis=-1).astype(bf16)
    ys = _experts_call(tile_expert, n_tiles, xs, wgu, p["moe_w_down"].astype(bf16), tr)
    return _combine_call(pos3, h1, meta, mod3, final_gain.reshape(1, d), ys, seq, tm, final_norm)


def kernel(x, c, w_mod, b_mod, norm1_g, w_in, b_in, ssm_lam_re, ssm_lam_im, ssm_log_dt, ssm_b_re, ssm_b_im, ssm_c_re, ssm_c_im, ssm_d, ssm_w_glu, ssm_b_glu, pool_w, pool_scale, w_proj_ssm, w_proj_pool, w_out, norm2_g, router_coarse_w, router_coarse_b, router_fine_w, router_fine_b, moe_w_gate, moe_w_up, moe_w_down, norm_f_g):
    bsz, seq, d = x.shape
    depth = w_mod.shape[0]
    per_layer = dict(
        norm1_g=norm1_g, w_in=w_in, b_in=b_in, ssm_lam_re=ssm_lam_re, ssm_lam_im=ssm_lam_im,
        ssm_log_dt=ssm_log_dt, ssm_b_re=ssm_b_re, ssm_b_im=ssm_b_im, ssm_c_re=ssm_c_re,
        ssm_c_im=ssm_c_im, ssm_d=ssm_d, ssm_w_glu=ssm_w_glu, ssm_b_glu=ssm_b_glu, pool_w=pool_w,
        pool_scale=pool_scale, w_proj_ssm=w_proj_ssm, w_proj_pool=w_proj_pool, w_out=w_out,
        norm2_g=norm2_g, router_coarse_w=router_coarse_w, router_coarse_b=router_coarse_b,
        router_fine_w=router_fine_w, router_fine_b=router_fine_b, moe_w_gate=moe_w_gate,
        moe_w_up=moe_w_up, moe_w_down=moe_w_down)
    h2 = x.reshape(bsz * seq, d)
    for l in range(depth):
        p = {k: v[l] for k, v in per_layer.items()}
        mod3 = _mod_call(c, w_mod[l], b_mod[l]).reshape(bsz, 1, N_MOD * d)
        h2 = _layer(h2, mod3, p, bsz, seq, norm_f_g, final_norm=(l == depth - 1))
    return h2.reshape(bsz, seq, d)
```

```python
import functools
import math

import jax
import jax.numpy as jnp
from jax import lax
from jax.experimental import pallas as pl
from jax.experimental.pallas import tpu as pltpu

EPS = 1e-6
POOL_WINDOWS = (2, 4, 8, 16)
TOP_K_FINE = 2
N_MOD = 6
LANES = 128
SSM_SLABS = 8
ROUTE_LANES = 128
ROW_DMA_UNROLL = 8
VMEM_LIMIT = 56 * 1024 * 1024
NEG = -1e30

f32 = jnp.float32
bf16 = jnp.bfloat16


def _cparams(sem):
    return pltpu.CompilerParams(dimension_semantics=sem, vmem_limit_bytes=VMEM_LIMIT)


def _dot(a, b):
    return jnp.dot(a, b, preferred_element_type=f32)


def _store_panels(ref, val, rows, row0=0):
    p = val.shape[1] // LANES
    for j in range(p):
        ref[pl.ds(row0 * p + j, rows, stride=p), :] = val[:, j * LANES:(j + 1) * LANES]


def _load_panels(ref, rows, p):
    return jnp.concatenate([ref[pl.ds(j, rows, stride=p), :] for j in range(p)], axis=1)


def _split_bf16(v):
    hi = v.astype(bf16)
    return hi, (v - hi.astype(f32)).astype(bf16)


def _mod_kernel(c_ref, w_ref, b_ref, o_ref):
    c = c_ref[...]
    a_hi, a_lo = _split_bf16(c * jax.nn.sigmoid(c))
    w_hi, w_lo = _split_bf16(w_ref[...])
    o_ref[...] = _dot(a_hi, w_hi) + (_dot(a_lo, w_hi) + _dot(a_hi, w_lo)) + b_ref[...]


def _mod_call(c, w_mod, b_mod):
    bsz, d = c.shape
    n = w_mod.shape[1]
    tn = d
    return pl.pallas_call(
        _mod_kernel,
        grid=(n // tn,),
        in_specs=[pl.BlockSpec((bsz, d), lambda j: (0, 0)),
                  pl.BlockSpec((d, tn), lambda j: (0, j)),
                  pl.BlockSpec((1, tn), lambda j: (0, j))],
        out_specs=pl.BlockSpec((bsz, tn), lambda j: (0, j)),
        out_shape=jax.ShapeDtypeStruct((bsz, n), f32),
        compiler_params=_cparams(("parallel",)),
        name="mod",
    )(c, w_mod, b_mod.reshape(1, n))


def _ssm_prep_kernel(lr_ref, li_ref, ldt_ref, btr_ref, bti_ref, pw_ref, bbr_ref, bbi_ref,
                     *, n_pow, n_grp):
    lr = lr_ref[...]
    li = li_ref[...]
    dt = jnp.exp(ldt_ref[...])
    for k in range(1, n_pow + 1):
        mag = jnp.exp(lr * dt * float(k))
        ang = li * dt * float(k)
        pw_ref[k - 1, 0] = mag * jnp.cos(ang)
        pw_ref[k - 1, 1] = mag * jnp.sin(ang)
    lb_re = pw_ref[0, 0]
    lb_im = pw_ref[0, 1]
    den = lr * lr + li * li
    nr = lb_re - 1.0
    f_re = (nr * lr + lb_im * li) / den
    f_im = (lb_im * lr - nr * li) / den
    for g in range(n_grp):
        fr, fi = f_re[g:g + 1, :], f_im[g:g + 1, :]
        bbr_ref[g] = fr * btr_ref[g] - fi * bti_ref[g]
        bbi_ref[g] = fr * bti_ref[g] + fi * btr_ref[g]


def _ssm_prep_call(lam_re, lam_im, log_dt, bt_re, bt_im, n_pow):
    g, p = lam_re.shape
    return pl.pallas_call(
        functools.partial(_ssm_prep_kernel, n_pow=n_pow, n_grp=g),
        out_shape=(jax.ShapeDtypeStruct((n_pow, 2, g, p), f32),
                   jax.ShapeDtypeStruct(bt_re.shape, f32),
                   jax.ShapeDtypeStruct(bt_re.shape, f32)),
        name="ssm_prep",
    )(lam_re, lam_im, log_dt.reshape(g, 1), bt_re, bt_im)


def _block_diag(blocks):
    g, a, b = blocks.shape
    eye = jnp.eye(g, dtype=blocks.dtype)
    return (eye[:, None, :, None] * blocks[:, :, None, :]).reshape(g * a, g * b)


def _modulated_norm(x, gain, shift, scale):
    ms = jnp.mean(x * x, axis=-1, keepdims=True)
    y = x * lax.rsqrt(ms + EPS) * gain
    return y * (1.0 + scale) + shift


def _in_proj_kernel(x_ref, mod_ref, g_ref, w_ref, b_ref, us_ref, up_ref, *, d, d_ssm):
    u = _modulated_norm(x_ref[...], g_ref[...], mod_ref[:, 0:d], mod_ref[:, d:2 * d])
    r = _dot(u.astype(bf16), w_ref[...]) + b_ref[...]
    for j in range(d_ssm // LANES):
        us_ref[j] = r[:, j * LANES:(j + 1) * LANES]
    up_ref[...] = r[:, d_ssm:]


def _in_proj_call(x2, mod3, g1, w_sp, b_sp, seq, tm, d_ssm):
    t, d = x2.shape
    n = w_sp.shape[1]
    per = seq // tm
    nl = d_ssm // LANES
    return pl.pallas_call(
        functools.partial(_in_proj_kernel, d=d, d_ssm=d_ssm),
        grid=(t // tm,),
        in_specs=[pl.BlockSpec((tm, d), lambda i: (i, 0)),
                  pl.BlockSpec((None, 1, mod3.shape[2]), lambda i: (i // per, 0, 0)),
                  pl.BlockSpec((1, d), lambda i: (0, 0)),
                  pl.BlockSpec((d, n), lambda i: (0, 0)),
                  pl.BlockSpec((1, n), lambda i: (0, 0))],
        out_specs=[pl.BlockSpec((nl, tm, LANES), lambda i: (0, i, 0)),
                   pl.BlockSpec((tm, n - d_ssm), lambda i: (i, 0))],
        out_shape=(jax.ShapeDtypeStruct((nl, t, LANES), f32),
                   jax.ShapeDtypeStruct((t, n - d_ssm), f32)),
        compiler_params=_cparams(("parallel",)),
        name="in_proj",
    )(x2, mod3, g1, w_sp, b_sp)


def _cmul_add(ar, ai, xr, xi, br, bi):
    return ar * xr - ai * xi + br, ar * xi + ai * xr + bi


def _mixers_kernel(us_ref, up_ref, bbd_ref, cbd_ref, pw_ref, dsk_ref, wglu_ref, bglu_ref,
                   pw_pool_ref, psc_ref, a_ref, b_ref, e_ref, s_ref, y_ref,
                   *, seq, d_ssm, d_pool, n_state, slabs):
    nc = seq // slabs
    nl = d_ssm // LANES
    ns = n_state
    dsk = dsk_ref[...]
    l1r = pw_ref[0:1, 0:ns]
    l1i = pw_ref[0:1, ns:2 * ns]

    xr = xi = None
    for t in range(slabs):
        ut = jnp.concatenate([us_ref[j, pl.ds(t, nc, stride=slabs), :] for j in range(nl)], axis=1)
        bu = _dot(ut.astype(bf16), bbd_ref[...])
        br, bi = bu[:, 0:ns], bu[:, ns:2 * ns]
        if t == 0:
            xr, xi = br, bi
        else:
            xr, xi = _cmul_add(l1r, l1i, xr, xi, br, bi)
        xc = jnp.concatenate([xr, xi], axis=1).astype(bf16)
        y_ref[t] = _dot(xc, cbd_ref[...]) + dsk * ut
    e_ref[:, 0:ns] = xr
    e_ref[:, ns:2 * ns] = xi

    lLr = pw_ref[slabs - 1:slabs, 0:ns]
    lLi = pw_ref[slabs - 1:slabs, ns:2 * ns]

    def chunk_step(c, carry):
        sr, si = carry
        s_ref[pl.ds(c, 1), 0:ns] = sr
        s_ref[pl.ds(c, 1), ns:2 * ns] = si
        er = e_ref[pl.ds(c, 1), 0:ns]
        ei = e_ref[pl.ds(c, 1), ns:2 * ns]
        return _cmul_add(lLr, lLi, sr, si, er, ei)

    zero = jnp.zeros((1, ns), f32)
    lax.fori_loop(0, nc, chunk_step, (zero, zero))

    sr = s_ref[:, 0:ns]
    si = s_ref[:, ns:2 * ns]
    for t in range(slabs):
        pr = pw_ref[t:t + 1, 0:ns]
        pi = pw_ref[t:t + 1, ns:2 * ns]
        zr = pr * sr - pi * si
        zi = pr * si + pi * sr
        zc = jnp.concatenate([zr, zi], axis=1).astype(bf16)
        y = y_ref[t] + _dot(zc, cbd_ref[...])
        y = jax.nn.gelu(y)
        z = y * jax.nn.sigmoid(_dot(y.astype(bf16), wglu_ref[...]) + bglu_ref[...])
        for j in range(nl):
            a_ref[j, pl.ds(t, nc, stride=slabs), :] = z[:, j * LANES:(j + 1) * LANES]

    gc = d_pool // len(POOL_WINDOWS)
    row = lax.broadcasted_iota(jnp.int32, (seq, gc), 0)
    for gi, w in enumerate(POOL_WINDOWS):
        lo = gi * gc
        v = up_ref[:, lo:lo + gc]
        acc = v
        span = 1
        while span < w:
            acc = acc + jnp.where(row >= span, pltpu.roll(acc, span, axis=0), 0.0)
            span *= 2
        cnt = jnp.minimum(row + 1, w).astype(f32)
        m = acc / cnt - v
        yg = _dot(m.astype(bf16), pw_pool_ref[gi])
        b_ref[:, lo:lo + gc] = yg * psc_ref[:, lo:lo + gc]


def _mixers_call(us, up, bbd, cbd, pw, dsk, wglu, bglu, pool_w, pool_scale, bsz, seq):
    nl, t, _ = us.shape
    d_ssm = nl * LANES
    d_pool = up.shape[1]
    ns2 = bbd.shape[1]
    slabs = SSM_SLABS
    nc = seq // slabs
    kern = functools.partial(_mixers_kernel, seq=seq, d_ssm=d_ssm, d_pool=d_pool,
                             n_state=ns2 // 2, slabs=slabs)
    const2 = lambda b: (0, 0)
    return pl.pallas_call(
        kern,
        grid=(bsz,),
        in_specs=[pl.BlockSpec((nl, seq, LANES), lambda b: (0, b, 0)),
                  pl.BlockSpec((seq, d_pool), lambda b: (b, 0)),
                  pl.BlockSpec(bbd.shape, const2),
                  pl.BlockSpec(cbd.shape, const2),
                  pl.BlockSpec(pw.shape, const2),
                  pl.BlockSpec(dsk.shape, const2),
                  pl.BlockSpec(wglu.shape, const2),
                  pl.BlockSpec(bglu.shape, const2),
                  pl.BlockSpec(pool_w.shape, lambda b: (0, 0, 0)),
                  pl.BlockSpec(pool_scale.shape, const2)],
        out_specs=[pl.BlockSpec((nl, seq, LANES), lambda b: (0, b, 0)),
                   pl.BlockSpec((seq, d_pool), lambda b: (b, 0))],
        out_shape=(jax.ShapeDtypeStruct((nl, t, LANES), f32),
                   jax.ShapeDtypeStruct((t, d_pool), f32)),
        scratch_shapes=[pltpu.VMEM((nc, ns2), f32),
                        pltpu.VMEM((nc, ns2), f32),
                        pltpu.VMEM((slabs, nc, d_ssm), f32)],
        compiler_params=_cparams(("parallel",)),
        name="mixers",
    )(us, up, bbd, cbd, pw, dsk, wglu, bglu, pool_w, pool_scale)


def _post_kernel(x_ref, a_ref, b_ref, mod_ref, g1_ref, g2_ref, wg_ref, bg_ref, wps_ref, wpp_ref,
                 wout_ref, wr_ref, br_ref,
                 h_ref, up_ref, meta_ref, metat_ref, cnt_ref, carry_ref,
                 *, d, d_ssm, n_grp, n_exp, tm, ts):
    i = pl.program_id(0)

    @pl.when(i == 0)
    def _():
        carry_ref[...] = jnp.zeros_like(carry_ref)

    for r0 in range(0, tm, ts):
        _post_rows(x_ref, a_ref, b_ref, mod_ref, g1_ref, g2_ref, wg_ref, bg_ref, wps_ref, wpp_ref,
                   wout_ref, wr_ref, br_ref, h_ref, up_ref, meta_ref, metat_ref, carry_ref,
                   d=d, d_ssm=d_ssm, n_grp=n_grp, n_exp=n_exp, r0=r0, tm=ts)
    cnt_ref[...] = carry_ref[...]


def _post_rows(x_ref, a_ref, b_ref, mod_ref, g1_ref, g2_ref, wg_ref, bg_ref, wps_ref, wpp_ref,
               wout_ref, wr_ref, br_ref, h_ref, up_ref, meta_ref, metat_ref, carry_ref,
               *, d, d_ssm, n_grp, n_exp, r0, tm):
    rows = pl.ds(r0, tm)
    x = x_ref[rows, :]
    shift1, scale1, gate1 = mod_ref[:, 0:d], mod_ref[:, d:2 * d], mod_ref[:, 2 * d:3 * d]
    shift2, scale2 = mod_ref[:, 3 * d:4 * d], mod_ref[:, 4 * d:5 * d]
    u = _modulated_norm(x, g1_ref[...], shift1, scale1).astype(bf16)
    gates = _dot(u, wg_ref[...]) + bg_ref[...]
    a = jnp.concatenate([a_ref[j, rows, :] for j in range(d_ssm // LANES)], axis=1).astype(bf16)
    b = b_ref[rows, :].astype(bf16)
    merged = (jax.nn.sigmoid(gates[:, 0:d]) * _dot(a, wps_ref[...])
              + jax.nn.sigmoid(gates[:, d:2 * d]) * _dot(b, wpp_ref[...]))
    h = x + gate1 * _dot(merged.astype(bf16), wout_ref[...])
    h_ref[rows, :] = h

    u2 = _modulated_norm(h, g2_ref[...], shift2, scale2)

    half = d // 2
    hi_bits = lax.bitcast_convert_type(u2[:, 0:half].astype(bf16).astype(f32), jnp.uint32)
    lo_bits = lax.bitcast_convert_type(u2[:, half:d].astype(bf16).astype(f32), jnp.uint32)
    _store_panels(up_ref, hi_bits | (lo_bits >> 16), tm, r0)

    u_hi = u2.astype(bf16)
    u_lo = (u2 - u_hi.astype(f32)).astype(bf16)
    r1 = _dot(u_hi, wr_ref[...])
    r2 = _dot(u_lo, wr_ref[...])
    lg = r1 + pltpu.roll(r1, ROUTE_LANES // 2, axis=1) + r2 + br_ref[...]

    lane = lax.broadcasted_iota(jnp.int32, (tm, ROUTE_LANES), 1).astype(f32)
    big = float(ROUTE_LANES)
    epg = float(n_exp // n_grp)
    is_c = lane < n_grp
    cl = jnp.where(is_c, lg, NEG)
    cmax = jnp.max(cl, axis=-1, keepdims=True)
    grp = jnp.min(jnp.where(cl == cmax, lane, big), axis=-1, keepdims=True)
    p_grp = 1.0 / jnp.sum(jnp.where(is_c, jnp.exp(cl - cmax), 0.0), axis=-1, keepdims=True)

    f_lo = n_grp + grp * epg
    fl = jnp.where((lane >= f_lo) & (lane < f_lo + epg), lg, NEG)
    f1 = jnp.max(fl, axis=-1, keepdims=True)
    i1 = jnp.min(jnp.where(fl == f1, lane, big), axis=-1, keepdims=True)
    fl2 = jnp.where(lane == i1, NEG, fl)
    f2 = jnp.max(fl2, axis=-1, keepdims=True)
    i2 = jnp.min(jnp.where(fl2 == f2, lane, big), axis=-1, keepdims=True)
    t2 = jnp.exp(f2 - f1)
    w0 = p_grp / (1.0 + t2)
    w1 = p_grp * t2 / (1.0 + t2)

    sel0 = lane == i1
    sel1 = lane == i2
    oh = jnp.where(sel0 | sel1, 1.0, 0.0)
    r_i = lax.broadcasted_iota(jnp.int32, (tm, tm), 0)
    c_i = lax.broadcasted_iota(jnp.int32, (tm, tm), 1)
    tri = jnp.where(c_i < r_i, 1.0, 0.0).astype(bf16)
    before = carry_ref[...] + _dot(tri, oh.astype(bf16))
    rank0 = jnp.sum(jnp.where(sel0, before, 0.0), axis=-1, keepdims=True)
    rank1 = jnp.sum(jnp.where(sel1, before, 0.0), axis=-1, keepdims=True)
    carry_ref[...] = carry_ref[...] + jnp.sum(oh, axis=0, keepdims=True)

    e0 = i1 - n_grp
    e1 = i2 - n_grp
    meta = jnp.zeros((tm, ROUTE_LANES), f32)
    for k, val in enumerate((e0, e1, w0, w1, rank0, rank1)):
        meta = jnp.where(lane == float(k), val, meta)
    meta_ref[rows, :] = meta
    metat_ref[:, rows] = meta.T[0:8, :]


def _post_call(x2, a3, b2, mod3, g1, g2, wg, bg, wps, wpp, wout, wr, br, seq, tm, ts,
               n_grp, n_exp):
    t, d = x2.shape
    d_ssm = wps.shape[0]
    nl = a3.shape[0]
    per = seq // tm
    kern = functools.partial(_post_kernel, d=d, d_ssm=d_ssm, n_grp=n_grp, n_exp=n_exp,
                             tm=tm, ts=ts)
    const2 = lambda i: (0, 0)
    wspec = lambda w: pl.BlockSpec(w.shape, const2, pipeline_mode=pl.Buffered(1))
    return pl.pallas_call(
        kern,
        grid=(t // tm,),
        in_specs=[pl.BlockSpec((tm, d), lambda i: (i, 0)),
                  pl.BlockSpec((nl, tm, LANES), lambda i: (0, i, 0)),
                  pl.BlockSpec((tm, b2.shape[1]), lambda i: (i, 0)),
                  pl.BlockSpec((None, 1, mod3.shape[2]), lambda i: (i // per, 0, 0)),
                  pl.BlockSpec((1, d), const2),
                  pl.BlockSpec((1, d), const2),
                  wspec(wg),
                  pl.BlockSpec(bg.shape, const2),
                  wspec(wps),
                  wspec(wpp),
                  wspec(wout),
                  wspec(wr),
                  pl.BlockSpec(br.shape, const2)],
        out_specs=[pl.BlockSpec((tm, d), lambda i: (i, 0)),
                   pl.BlockSpec((tm * (d // 2 // LANES), LANES), lambda i: (i, 0)),
                   pl.BlockSpec((tm, ROUTE_LANES), lambda i: (i, 0)),
                   pl.BlockSpec((8, tm), lambda i: (0, i)),
                   pl.BlockSpec((1, ROUTE_LANES), const2)],
        out_shape=(jax.ShapeDtypeStruct((t, d), f32),
                   jax.ShapeDtypeStruct((t * (d // 2 // LANES), LANES), jnp.uint32),
                   jax.ShapeDtypeStruct((t, ROUTE_LANES), f32),
                   jax.ShapeDtypeStruct((8, t), f32),
                   jax.ShapeDtypeStruct((1, ROUTE_LANES), f32)),
        scratch_shapes=[pltpu.VMEM((1, ROUTE_LANES), f32)],
        compiler_params=_cparams(("arbitrary",)),
        name="post",
    )(x2, a3, b2, mod3, g1, g2, wg, bg, wps, wpp, wout, wr, br)


def _row_copy(src_ref, dst_ref, src_row, dst_row, p, sem):
    src = src_ref.at[pl.ds(pl.multiple_of(src_row * p, p), p)]
    dst = dst_ref.at[pl.ds(pl.multiple_of(dst_row * p, p), p)]
    return pltpu.make_async_copy(src, dst, sem)


def _wait_rows(src_ref, dst_ref, n_rows, p, sem):
    pltpu.make_async_copy(src_ref.at[pl.ds(0, n_rows * p)], dst_ref.at[pl.ds(0, n_rows * p)],
                          sem).wait()


def _scatter_kernel(pos_ref, up_ref, xs_ref, sem, *, tm, p):
    def issue(g, c):
        r0 = g * ROW_DMA_UNROLL
        idx = [[pos_ref[0, k, r0 + u] for k in range(TOP_K_FINE)] for u in range(ROW_DMA_UNROLL)]
        for u in range(ROW_DMA_UNROLL):
            for k in range(TOP_K_FINE):
                _row_copy(up_ref, xs_ref, r0 + u, idx[u][k], p, sem).start(priority=k % 2)
        return c

    lax.fori_loop(0, tm // ROW_DMA_UNROLL, issue, 0)
    for _ in range(TOP_K_FINE):
        _wait_rows(up_ref, xs_ref, tm, p, sem)


def _scatter_call(up, pos3, n_rows, tm, p):
    n_steps = up.shape[0] // (tm * p)
    return pl.pallas_call(
        functools.partial(_scatter_kernel, tm=tm, p=p),
        grid=(n_steps,),
        in_specs=[pl.BlockSpec((1, TOP_K_FINE, tm), lambda i: (i, 0, 0), memory_space=pltpu.SMEM),
                  pl.BlockSpec((tm * p, LANES), lambda i: (i, 0))],
        out_specs=pl.BlockSpec(memory_space=pl.ANY),
        out_shape=jax.ShapeDtypeStruct((n_rows * p, LANES), up.dtype),
        scratch_shapes=[pltpu.SemaphoreType.DMA(())],
        compiler_params=_cparams(("arbitrary",)),
        name="scatter_rows",
    )(pos3, up)


def _experts_kernel(te_ref, nt_ref, xs_ref, wgu_ref, wd_ref, ys_ref, *, d_exp, tr, px):
    i = pl.program_id(0)

    @pl.when(i < nt_ref[0])
    def _():
        w = _load_panels(xs_ref, tr, px)
        hi = lax.bitcast_convert_type(w & jnp.uint32(0xFFFF0000), f32)
        lo = lax.bitcast_convert_type(w << 16, f32)
        x = jnp.concatenate([hi, lo], axis=1).astype(bf16)
        h = _dot(x, wgu_ref[...])
        h1 = h[:, 0:d_exp]
        act = h1 * jax.nn.sigmoid(h1) * h[:, d_exp:2 * d_exp]
        _store_panels(ys_ref, _dot(act.astype(bf16), wd_ref[...]), tr)


def _experts_call(tile_expert, n_tiles, xs, wgu, wd, tr):
    n_exp, d, two_de = wgu.shape
    d_exp = two_de // 2
    px = d // 2 // LANES
    py = d // LANES
    n_rows = xs.shape[0] // px
    nt_max = n_rows // tr

    def row_map(i, te, nt):
        return (jnp.minimum(i, nt[0] - 1), 0)

    def w_map(i, te, nt):
        return (te[jnp.minimum(i, nt[0] - 1)], 0, 0)

    grid_spec = pltpu.PrefetchScalarGridSpec(
        num_scalar_prefetch=2,
        grid=(nt_max,),
        in_specs=[pl.BlockSpec((tr * px, LANES), row_map),
                  pl.BlockSpec((None, d, two_de), w_map),
                  pl.BlockSpec((None, d_exp, d), w_map)],
        out_specs=pl.BlockSpec((tr * py, LANES), row_map),
    )
    return pl.pallas_call(
        functools.partial(_experts_kernel, d_exp=d_exp, tr=tr, px=px),
        grid_spec=grid_spec,
        out_shape=jax.ShapeDtypeStruct((n_rows * py, LANES), f32),
        compiler_params=_cparams(("arbitrary",)),
        name="experts",
    )(tile_expert, n_tiles, xs, wgu, wd)


def _combine_kernel(pos_ref, posn_ref, h_ref, meta_ref, mod_ref, gf_ref, ys_ref, o_ref,
                    g_ref, sem, *, d, tm, n_steps, final_norm):
    i = pl.program_id(0)
    slot = i % 2
    py = d // LANES

    def issue(p_ref, s):
        def body(g, c):
            r0 = g * ROW_DMA_UNROLL
            idx = [[p_ref[0, k, r0 + u] for k in range(TOP_K_FINE)] for u in range(ROW_DMA_UNROLL)]
            for u in range(ROW_DMA_UNROLL):
                for k in range(TOP_K_FINE):
                    _row_copy(ys_ref, g_ref.at[s, k], idx[u][k], r0 + u, py,
                              sem.at[s]).start(priority=k % 2)
            return c

        lax.fori_loop(0, tm // ROW_DMA_UNROLL, body, 0)

    pl.when(i == 0)(lambda: issue(pos_ref, 0))
    pl.when(i + 1 < n_steps)(lambda: issue(posn_ref, 1 - slot))
    for k in range(TOP_K_FINE):
        _wait_rows(ys_ref, g_ref.at[slot, k], tm, py, sem.at[slot])

    gate2 = mod_ref[:, 5 * d:6 * d]
    w0 = meta_ref[:, 2:3]
    w1 = meta_ref[:, 3:4]
    y0 = _load_panels(g_ref.at[slot, 0], tm, py)
    y1 = _load_panels(g_ref.at[slot, 1], tm, py)
    h = h_ref[...] + gate2 * (w0 * y0 + w1 * y1)
    if final_norm:
        ms = jnp.mean(h * h, axis=-1, keepdims=True)
        h = h * lax.rsqrt(ms + EPS) * gf_ref[...]
    o_ref[...] = h


def _combine_call(pos3, h1, meta, mod3, gf, ys, seq, tm, final_norm):
    t, d = h1.shape
    per = seq // tm
    n_steps = t // tm
    pos_spec = lambda f: pl.BlockSpec((1, TOP_K_FINE, tm), f, memory_space=pltpu.SMEM)
    return pl.pallas_call(
        functools.partial(_combine_kernel, d=d, tm=tm, n_steps=n_steps, final_norm=final_norm),
        grid=(n_steps,),
        in_specs=[pos_spec(lambda i: (i, 0, 0)),
                  pos_spec(lambda i: (jnp.minimum(i + 1, n_steps - 1), 0, 0)),
                  pl.BlockSpec((tm, d), lambda i: (i, 0)),
                  pl.BlockSpec((tm, ROUTE_LANES), lambda i: (i, 0)),
                  pl.BlockSpec((None, 1, mod3.shape[2]), lambda i: (i // per, 0, 0)),
                  pl.BlockSpec((1, d), lambda i: (0, 0)),
                  pl.BlockSpec(memory_space=pl.ANY)],
        out_specs=pl.BlockSpec((tm, d), lambda i: (i, 0)),
        out_shape=jax.ShapeDtypeStruct((t, d), f32),
        scratch_shapes=[pltpu.VMEM((2, TOP_K_FINE, tm * (d // LANES), LANES), f32),
                        pltpu.SemaphoreType.DMA((2,))],
        compiler_params=_cparams(("arbitrary",)),
        name="combine",
    )(pos3, pos3, h1, meta, mod3, gf, ys)


def _row_tile(seq, target):
    tm = min(target, seq)
    assert seq % tm == 0 and tm % 8 == 0
    return tm


def _layer(h2, mod3, p, bsz, seq, final_gain, final_norm):
    t, d = h2.shape
    g, n_p, n_h = p["ssm_b_re"].shape
    d_ssm = g * n_h
    d_pool = p["pool_scale"].shape[-1]
    n_grp = p["router_coarse_w"].shape[-1]
    n_exp = p["router_fine_w"].shape[-1]
    assert n_grp + n_exp <= ROUTE_LANES // 2 and seq % SSM_SLABS == 0
    tm = _row_tile(seq, 512)

    pw, bbt_re, bbt_im = _ssm_prep_call(
        p["ssm_lam_re"], p["ssm_lam_im"], p["ssm_log_dt"],
        jnp.swapaxes(p["ssm_b_re"], 1, 2), jnp.swapaxes(p["ssm_b_im"], 1, 2), SSM_SLABS)
    bbd = jnp.concatenate([_block_diag(bbt_re), _block_diag(bbt_im)], axis=1).astype(bf16)
    cbd = jnp.concatenate([_block_diag(jnp.swapaxes(p["ssm_c_re"], 1, 2)),
                           -_block_diag(jnp.swapaxes(p["ssm_c_im"], 1, 2))], axis=0).astype(bf16)
    pw2 = pw.reshape(SSM_SLABS, 2 * g * n_p)

    w_in, b_in = p["w_in"], p["b_in"]
    n_sp = d_ssm + d_pool
    us, upool = _in_proj_call(h2, mod3, p["norm1_g"].reshape(1, d), w_in[:, :n_sp].astype(bf16),
                              b_in[:n_sp].reshape(1, n_sp), seq, tm, d_ssm)
    a3, b2 = _mixers_call(us, upool, bbd, cbd, pw2, p["ssm_d"].reshape(1, d_ssm),
                          p["ssm_w_glu"].astype(bf16), p["ssm_b_glu"].reshape(1, d_ssm),
                          p["pool_w"].astype(bf16), p["pool_scale"].reshape(1, d_pool),
                          bsz, seq)

    wr = jnp.concatenate([p["router_coarse_w"], p["router_fine_w"]], axis=1)
    wr_hi = wr.astype(bf16)
    wr_lo = (wr - wr_hi.astype(f32)).astype(bf16)
    padc = ROUTE_LANES // 2 - wr.shape[1]
    wr_cat = jnp.concatenate([jnp.pad(wr_hi, ((0, 0), (0, padc))),
                              jnp.pad(wr_lo, ((0, 0), (0, padc)))], axis=1)
    br = jnp.pad(jnp.concatenate([p["router_coarse_b"], p["router_fine_b"]]),
                 (0, ROUTE_LANES - wr.shape[1])).reshape(1, ROUTE_LANES)

    h1, up, meta, metat, cnt = _post_call(
        h2, a3, b2, mod3, p["norm1_g"].reshape(1, d), p["norm2_g"].reshape(1, d),
        w_in[:, n_sp:].astype(bf16), b_in[n_sp:].reshape(1, 2 * d),
        p["w_proj_ssm"].astype(bf16), p["w_proj_pool"].astype(bf16), p["w_out"].astype(bf16),
        wr_cat, br, seq, tm, tm, n_grp, n_exp)

    tr = 512
    counts = cnt[0, n_grp:n_grp + n_exp].astype(jnp.int32)
    padded = ((counts + tr - 1) // tr) * tr
    ends = jnp.cumsum(padded)
    offs = ends - padded
    ids = jnp.arange(n_exp, dtype=jnp.int32)

    def sorted_pos(k):
        e = metat[k].astype(jnp.int32)
        off = jnp.sum(jnp.where(e[None, :] == ids[:, None], offs[:, None], 0), axis=0)
        return (off + metat[4 + k].astype(jnp.int32)).reshape(t // tm, tm)

    pos3 = jnp.stack([sorted_pos(k) for k in range(TOP_K_FINE)], axis=1)
    nt_max = (TOP_K_FINE * t) // tr + n_exp
    n_tiles = (ends[-1] // tr).astype(jnp.int32).reshape(1)
    tile_start = jnp.arange(nt_max, dtype=jnp.int32) * tr
    tile_expert = jnp.minimum(
        jnp.sum((ends[None, :] <= tile_start[:, None]).astype(jnp.int32), axis=1), n_exp - 1)

    xs = _scatter_call(up, pos3, nt_max * tr, tm, d // 2 // LANES)
    wgu = jnp.concatenate([p["moe_w_gate"], p["moe_w_up"]], axis=-1).astype(bf16)
    ys = _experts_call(tile_expert, n_tiles, xs, wgu, p["moe_w_down"].astype(bf16), tr)
    return _combine_call(pos3, h1, meta, mod3, final_gain.reshape(1, d), ys, seq, tm, final_norm)


def kernel(x, c, w_mod, b_mod, norm1_g, w_in, b_in, ssm_lam_re, ssm_lam_im, ssm_log_dt, ssm_b_re, ssm_b_im, ssm_c_re, ssm_c_im, ssm_d, ssm_w_glu, ssm_b_glu, pool_w, pool_scale, w_proj_ssm, w_proj_pool, w_out, norm2_g, router_coarse_w, router_coarse_b, router_fine_w, router_fine_b, moe_w_gate, moe_w_up, moe_w_down, norm_f_g):
    bsz, seq, d = x.shape
    depth = w_mod.shape[0]
    per_layer = dict(
        norm1_g=norm1_g, w_in=w_in, b_in=b_in, ssm_lam_re=ssm_lam_re, ssm_lam_im=ssm_lam_im,
        ssm_log_dt=ssm_log_dt, ssm_b_re=ssm_b_re, ssm_b_im=ssm_b_im, ssm_c_re=ssm_c_re,
        ssm_c_im=ssm_c_im, ssm_d=ssm_d, ssm_w_glu=ssm_w_glu, ssm_b_glu=ssm_b_glu, pool_w=pool_w,
        pool_scale=pool_scale, w_proj_ssm=w_proj_ssm, w_proj_pool=w_proj_pool, w_out=w_out,
        norm2_g=norm2_g, router_coarse_w=router_coarse_w, router_coarse_b=router_coarse_b,
        router_fine_w=router_fine_w, router_fine_b=router_fine_b, moe_w_gate=moe_w_gate,
        moe_w_up=moe_w_up, moe_w_down=moe_w_down)
    h2 = x.reshape(bsz * seq, d)
    for l in range(depth):
        p = {k: v[l] for k, v in per_layer.items()}
        mod3 = _mod_call(c, w_mod[l], b_mod[l]).reshape(bsz, 1, N_MOD * d)
        h2 = _layer(h2, mod3, p, bsz, seq, norm_f_g, final_norm=(l == depth - 1))
    return h2.reshape(bsz, seq, d)
```

```python
import functools
import math

import jax
import jax.numpy as jnp
from jax import lax
from jax.experimental import pallas as pl
from jax.experimental.pallas import tpu as pltpu

EPS = 1e-6
POOL_WINDOWS = (2, 4, 8, 16)
TOP_K_FINE = 2
N_MOD = 6
LANES = 128
SSM_SLABS = 8
ROUTE_LANES = 128
ROW_DMA_UNROLL = 8
ROW_PANELS = 8
EXPERT_TILE_ROWS = 256
VMEM_LIMIT = 56 * 1024 * 1024
NEG = -1e30

f32 = jnp.float32
bf16 = jnp.bfloat16


def _cparams(sem):
    return pltpu.CompilerParams(dimension_semantics=sem, vmem_limit_bytes=VMEM_LIMIT)


def _dot(a, b):
    return jnp.dot(a, b, preferred_element_type=f32)


def _store_panels(ref, val, rows, row0=0):
    p = val.shape[1] // LANES
    for j in range(p):
        ref[pl.ds(row0 * p + j, rows, stride=p), :] = val[:, j * LANES:(j + 1) * LANES]


def _load_panels(ref, rows, p):
    return jnp.concatenate([ref[pl.ds(j, rows, stride=p), :] for j in range(p)], axis=1)


def _split_bf16(v):
    hi = v.astype(bf16)
    return hi, (v - hi.astype(f32)).astype(bf16)


def _mod_kernel(c_ref, w_ref, b_ref, o_ref):
    c = c_ref[...]
    a_hi, a_lo = _split_bf16(c * jax.nn.sigmoid(c))
    w_hi, w_lo = _split_bf16(w_ref[...])
    o_ref[...] = _dot(a_hi, w_hi) + (_dot(a_lo, w_hi) + _dot(a_hi, w_lo)) + b_ref[...]


def _mod_call(c, w_mod, b_mod):
    bsz, d = c.shape
    n = w_mod.shape[1]
    tn = d
    return pl.pallas_call(
        _mod_kernel,
        grid=(n // tn,),
        in_specs=[pl.BlockSpec((bsz, d), lambda j: (0, 0)),
                  pl.BlockSpec((d, tn), lambda j: (0, j)),
                  pl.BlockSpec((1, tn), lambda j: (0, j))],
        out_specs=pl.BlockSpec((bsz, tn), lambda j: (0, j)),
        out_shape=jax.ShapeDtypeStruct((bsz, n), f32),
        compiler_params=_cparams(("parallel",)),
        name="mod",
    )(c, w_mod, b_mod.reshape(1, n))


def _ssm_prep_kernel(lr_ref, li_ref, ldt_ref, btr_ref, bti_ref, pw_ref, bbr_ref, bbi_ref,
                     *, n_pow, n_grp):
    lr = lr_ref[...]
    li = li_ref[...]
    dt = jnp.exp(ldt_ref[...])
    for k in range(1, n_pow + 1):
        mag = jnp.exp(lr * dt * float(k))
        ang = li * dt * float(k)
        pw_ref[k - 1, 0] = mag * jnp.cos(ang)
        pw_ref[k - 1, 1] = mag * jnp.sin(ang)
    lb_re = pw_ref[0, 0]
    lb_im = pw_ref[0, 1]
    den = lr * lr + li * li
    nr = lb_re - 1.0
    f_re = (nr * lr + lb_im * li) / den
    f_im = (lb_im * lr - nr * li) / den
    for g in range(n_grp):
        fr, fi = f_re[g:g + 1, :], f_im[g:g + 1, :]
        bbr_ref[g] = fr * btr_ref[g] - fi * bti_ref[g]
        bbi_ref[g] = fr * bti_ref[g] + fi * btr_ref[g]


def _ssm_prep_call(lam_re, lam_im, log_dt, bt_re, bt_im, n_pow):
    g, p = lam_re.shape
    return pl.pallas_call(
        functools.partial(_ssm_prep_kernel, n_pow=n_pow, n_grp=g),
        out_shape=(jax.ShapeDtypeStruct((n_pow, 2, g, p), f32),
                   jax.ShapeDtypeStruct(bt_re.shape, f32),
                   jax.ShapeDtypeStruct(bt_re.shape, f32)),
        name="ssm_prep",
    )(lam_re, lam_im, log_dt.reshape(g, 1), bt_re, bt_im)


def _block_diag(blocks):
    g, a, b = blocks.shape
    eye = jnp.eye(g, dtype=blocks.dtype)
    return (eye[:, None, :, None] * blocks[:, :, None, :]).reshape(g * a, g * b)


def _modulated_norm(x, gain, shift, scale):
    ms = jnp.mean(x * x, axis=-1, keepdims=True)
    y = x * lax.rsqrt(ms + EPS) * gain
    return y * (1.0 + scale) + shift


def _in_proj_kernel(x_ref, mod_ref, g_ref, w_ref, b_ref, us_ref, up_ref, *, d, d_ssm):
    u = _modulated_norm(x_ref[...], g_ref[...], mod_ref[:, 0:d], mod_ref[:, d:2 * d])
    r = _dot(u.astype(bf16), w_ref[...]) + b_ref[...]
    for j in range(d_ssm // LANES):
        us_ref[j] = r[:, j * LANES:(j + 1) * LANES]
    up_ref[...] = r[:, d_ssm:]


def _in_proj_call(x2, mod3, g1, w_sp, b_sp, seq, tm, d_ssm):
    t, d = x2.shape
    n = w_sp.shape[1]
    per = seq // tm
    nl = d_ssm // LANES
    return pl.pallas_call(
        functools.partial(_in_proj_kernel, d=d, d_ssm=d_ssm),
        grid=(t // tm,),
        in_specs=[pl.BlockSpec((tm, d), lambda i: (i, 0)),
                  pl.BlockSpec((None, 1, mod3.shape[2]), lambda i: (i // per, 0, 0)),
                  pl.BlockSpec((1, d), lambda i: (0, 0)),
                  pl.BlockSpec((d, n), lambda i: (0, 0)),
                  pl.BlockSpec((1, n), lambda i: (0, 0))],
        out_specs=[pl.BlockSpec((nl, tm, LANES), lambda i: (0, i, 0)),
                   pl.BlockSpec((tm, n - d_ssm), lambda i: (i, 0))],
        out_shape=(jax.ShapeDtypeStruct((nl, t, LANES), f32),
                   jax.ShapeDtypeStruct((t, n - d_ssm), f32)),
        compiler_params=_cparams(("parallel",)),
        name="in_proj",
    )(x2, mod3, g1, w_sp, b_sp)


def _cmul_add(ar, ai, xr, xi, br, bi):
    return ar * xr - ai * xi + br, ar * xi + ai * xr + bi


def _mixers_kernel(us_ref, up_ref, bbd_ref, cbd_ref, pw_ref, dsk_ref, wglu_ref, bglu_ref,
                   pw_pool_ref, psc_ref, a_ref, b_ref, e_ref, s_ref, y_ref,
                   *, seq, d_ssm, d_pool, n_state, slabs):
    nc = seq // slabs
    nl = d_ssm // LANES
    ns = n_state
    dsk = dsk_ref[...]
    l1r = pw_ref[0:1, 0:ns]
    l1i = pw_ref[0:1, ns:2 * ns]

    xr = xi = None
    for t in range(slabs):
        ut = jnp.concatenate([us_ref[j, pl.ds(t, nc, stride=slabs), :] for j in range(nl)], axis=1)
        bu = _dot(ut.astype(bf16), bbd_ref[...])
        br, bi = bu[:, 0:ns], bu[:, ns:2 * ns]
        if t == 0:
            xr, xi = br, bi
        else:
            xr, xi = _cmul_add(l1r, l1i, xr, xi, br, bi)
        xc = jnp.concatenate([xr, xi], axis=1).astype(bf16)
        y_ref[t] = _dot(xc, cbd_ref[...]) + dsk * ut
    e_ref[:, 0:ns] = xr
    e_ref[:, ns:2 * ns] = xi

    lLr = pw_ref[slabs - 1:slabs, 0:ns]
    lLi = pw_ref[slabs - 1:slabs, ns:2 * ns]

    def chunk_step(c, carry):
        sr, si = carry
        s_ref[pl.ds(c, 1), 0:ns] = sr
        s_ref[pl.ds(c, 1), ns:2 * ns] = si
        er = e_ref[pl.ds(c, 1), 0:ns]
        ei = e_ref[pl.ds(c, 1), ns:2 * ns]
        return _cmul_add(lLr, lLi, sr, si, er, ei)

    zero = jnp.zeros((1, ns), f32)
    lax.fori_loop(0, nc, chunk_step, (zero, zero))

    sr = s_ref[:, 0:ns]
    si = s_ref[:, ns:2 * ns]
    for t in range(slabs):
        pr = pw_ref[t:t + 1, 0:ns]
        pi = pw_ref[t:t + 1, ns:2 * ns]
        zr = pr * sr - pi * si
        zi = pr * si + pi * sr
        zc = jnp.concatenate([zr, zi], axis=1).astype(bf16)
        y = y_ref[t] + _dot(zc, cbd_ref[...])
        y = jax.nn.gelu(y)
        z = y * jax.nn.sigmoid(_dot(y.astype(bf16), wglu_ref[...]) + bglu_ref[...])
        for j in range(nl):
            a_ref[j, pl.ds(t, nc, stride=slabs), :] = z[:, j * LANES:(j + 1) * LANES]

    gc = d_pool // len(POOL_WINDOWS)
    row = lax.broadcasted_iota(jnp.int32, (seq, gc), 0)
    for gi, w in enumerate(POOL_WINDOWS):
        lo = gi * gc
        v = up_ref[:, lo:lo + gc]
        acc = v
        span = 1
        while span < w:
            acc = acc + jnp.where(row >= span, pltpu.roll(acc, span, axis=0), 0.0)
            span *= 2
        cnt = jnp.minimum(row + 1, w).astype(f32)
        m = acc / cnt - v
        yg = _dot(m.astype(bf16), pw_pool_ref[gi])
        b_ref[:, lo:lo + gc] = yg * psc_ref[:, lo:lo + gc]


def _mixers_call(us, up, bbd, cbd, pw, dsk, wglu, bglu, pool_w, pool_scale, bsz, seq):
    nl, t, _ = us.shape
    d_ssm = nl * LANES
    d_pool = up.shape[1]
    ns2 = bbd.shape[1]
    slabs = SSM_SLABS
    nc = seq // slabs
    kern = functools.partial(_mixers_kernel, seq=seq, d_ssm=d_ssm, d_pool=d_pool,
                             n_state=ns2 // 2, slabs=slabs)
    const2 = lambda b: (0, 0)
    return pl.pallas_call(
        kern,
        grid=(bsz,),
        in_specs=[pl.BlockSpec((nl, seq, LANES), lambda b: (0, b, 0)),
                  pl.BlockSpec((seq, d_pool), lambda b: (b, 0)),
                  pl.BlockSpec(bbd.shape, const2),
                  pl.BlockSpec(cbd.shape, const2),
                  pl.BlockSpec(pw.shape, const2),
                  pl.BlockSpec(dsk.shape, const2),
                  pl.BlockSpec(wglu.shape, const2),
                  pl.BlockSpec(bglu.shape, const2),
                  pl.BlockSpec(pool_w.shape, lambda b: (0, 0, 0)),
                  pl.BlockSpec(pool_scale.shape, const2)],
        out_specs=[pl.BlockSpec((nl, seq, LANES), lambda b: (0, b, 0)),
                   pl.BlockSpec((seq, d_pool), lambda b: (b, 0))],
        out_shape=(jax.ShapeDtypeStruct((nl, t, LANES), f32),
                   jax.ShapeDtypeStruct((t, d_pool), f32)),
        scratch_shapes=[pltpu.VMEM((nc, ns2), f32),
                        pltpu.VMEM((nc, ns2), f32),
                        pltpu.VMEM((slabs, nc, d_ssm), f32)],
        compiler_params=_cparams(("parallel",)),
        name="mixers",
    )(us, up, bbd, cbd, pw, dsk, wglu, bglu, pool_w, pool_scale)


def _post_kernel(x_ref, a_ref, b_ref, mod_ref, g1_ref, g2_ref, wg_ref, bg_ref, wps_ref, wpp_ref,
                 wout_ref, wr_ref, br_ref,
                 h_ref, up_ref, metat_ref, cnt_ref, carry_ref,
                 *, d, d_ssm, n_grp, n_exp, tm, ts):
    i = pl.program_id(0)

    @pl.when(i == 0)
    def _():
        carry_ref[...] = jnp.zeros_like(carry_ref)

    for r0 in range(0, tm, ts):
        _post_rows(x_ref, a_ref, b_ref, mod_ref, g1_ref, g2_ref, wg_ref, bg_ref, wps_ref, wpp_ref,
                   wout_ref, wr_ref, br_ref, h_ref, up_ref, metat_ref, carry_ref,
                   d=d, d_ssm=d_ssm, n_grp=n_grp, n_exp=n_exp, r0=r0, tm=ts)
    cnt_ref[...] = carry_ref[...]


def _post_rows(x_ref, a_ref, b_ref, mod_ref, g1_ref, g2_ref, wg_ref, bg_ref, wps_ref, wpp_ref,
               wout_ref, wr_ref, br_ref, h_ref, up_ref, metat_ref, carry_ref,
               *, d, d_ssm, n_grp, n_exp, r0, tm):
    rows = pl.ds(r0, tm)
    x = x_ref[rows, :]
    shift1, scale1, gate1 = mod_ref[:, 0:d], mod_ref[:, d:2 * d], mod_ref[:, 2 * d:3 * d]
    shift2, scale2 = mod_ref[:, 3 * d:4 * d], mod_ref[:, 4 * d:5 * d]
    u = _modulated_norm(x, g1_ref[...], shift1, scale1).astype(bf16)
    gates = _dot(u, wg_ref[...]) + bg_ref[...]
    a = jnp.concatenate([a_ref[j, rows, :] for j in range(d_ssm // LANES)], axis=1).astype(bf16)
    b = b_ref[rows, :].astype(bf16)
    merged = (jax.nn.sigmoid(gates[:, 0:d]) * _dot(a, wps_ref[...])
              + jax.nn.sigmoid(gates[:, d:2 * d]) * _dot(b, wpp_ref[...]))
    h = x + gate1 * _dot(merged.astype(bf16), wout_ref[...])
    h_ref[rows, :] = h

    u2 = _modulated_norm(h, g2_ref[...], shift2, scale2)

    half = d // 2
    hi_bits = lax.bitcast_convert_type(u2[:, 0:half].astype(bf16).astype(f32), jnp.uint32)
    lo_bits = lax.bitcast_convert_type(u2[:, half:d].astype(bf16).astype(f32), jnp.uint32)
    words = hi_bits | (lo_bits >> 16)

    u_hi = u2.astype(bf16)
    u_lo = (u2 - u_hi.astype(f32)).astype(bf16)
    r1 = _dot(u_hi, wr_ref[...])
    r2 = _dot(u_lo, wr_ref[...])
    lg = r1 + pltpu.roll(r1, ROUTE_LANES // 2, axis=1) + r2 + br_ref[...]

    lane = lax.broadcasted_iota(jnp.int32, (tm, ROUTE_LANES), 1).astype(f32)
    big = float(ROUTE_LANES)
    epg = float(n_exp // n_grp)
    is_c = lane < n_grp
    cl = jnp.where(is_c, lg, NEG)
    cmax = jnp.max(cl, axis=-1, keepdims=True)
    grp = jnp.min(jnp.where(cl == cmax, lane, big), axis=-1, keepdims=True)
    p_grp = 1.0 / jnp.sum(jnp.where(is_c, jnp.exp(cl - cmax), 0.0), axis=-1, keepdims=True)

    f_lo = n_grp + grp * epg
    fl = jnp.where((lane >= f_lo) & (lane < f_lo + epg), lg, NEG)
    f1 = jnp.max(fl, axis=-1, keepdims=True)
    i1 = jnp.min(jnp.where(fl == f1, lane, big), axis=-1, keepdims=True)
    fl2 = jnp.where(lane == i1, NEG, fl)
    f2 = jnp.max(fl2, axis=-1, keepdims=True)
    i2 = jnp.min(jnp.where(fl2 == f2, lane, big), axis=-1, keepdims=True)
    t2 = jnp.exp(f2 - f1)
    w0 = p_grp / (1.0 + t2)
    w1 = p_grp * t2 / (1.0 + t2)

    j0 = i1 - f_lo
    j1 = i2 - f_lo
    first = j0 < j1
    ja = jnp.minimum(j0, j1)
    jb = jnp.maximum(j0, j1)
    wa = jnp.where(first, w0, w1)
    wb = jnp.where(first, w1, w0)
    n_pair = epg * (epg - 1.0) * 0.5
    cls = grp * n_pair + ja * (2.0 * epg - ja - 1.0) * 0.5 + (jb - ja - 1.0)

    sel = lane == cls
    oh = jnp.where(sel, 1.0, 0.0)
    r_i = lax.broadcasted_iota(jnp.int32, (tm, tm), 0)
    c_i = lax.broadcasted_iota(jnp.int32, (tm, tm), 1)
    tri = jnp.where(c_i < r_i, 1.0, 0.0).astype(bf16)
    before = carry_ref[...] + _dot(tri, oh.astype(bf16))
    rank = jnp.sum(jnp.where(sel, before, 0.0), axis=-1, keepdims=True)
    carry_ref[...] = carry_ref[...] + jnp.sum(oh, axis=0, keepdims=True)

    meta = jnp.zeros((tm, ROUTE_LANES), f32)
    for k, val in enumerate((cls, rank, wa, wb)):
        meta = jnp.where(lane == float(k), val, meta)

    wts = jnp.where(lane == 0.0, wa, jnp.where(lane == 1.0, wb, 0.0))
    pad = jnp.zeros((tm, (ROW_PANELS - 1) * LANES - half), jnp.uint32)
    row = jnp.concatenate([words, lax.bitcast_convert_type(wts, jnp.uint32), pad], axis=1)
    _store_panels(up_ref, row, tm, r0)
    metat_ref[:, rows] = meta.T[0:8, :]


def _post_call(x2, a3, b2, mod3, g1, g2, wg, bg, wps, wpp, wout, wr, br, seq, tm, ts,
               n_grp, n_exp):
    t, d = x2.shape
    d_ssm = wps.shape[0]
    nl = a3.shape[0]
    per = seq // tm
    kern = functools.partial(_post_kernel, d=d, d_ssm=d_ssm, n_grp=n_grp, n_exp=n_exp,
                             tm=tm, ts=ts)
    const2 = lambda i: (0, 0)
    wspec = lambda w: pl.BlockSpec(w.shape, const2, pipeline_mode=pl.Buffered(1))
    return pl.pallas_call(
        kern,
        grid=(t // tm,),
        in_specs=[pl.BlockSpec((tm, d), lambda i: (i, 0)),
                  pl.BlockSpec((nl, tm, LANES), lambda i: (0, i, 0)),
                  pl.BlockSpec((tm, b2.shape[1]), lambda i: (i, 0)),
                  pl.BlockSpec((None, 1, mod3.shape[2]), lambda i: (i // per, 0, 0)),
                  pl.BlockSpec((1, d), const2),
                  pl.BlockSpec((1, d), const2),
                  wspec(wg),
                  pl.BlockSpec(bg.shape, const2),
                  wspec(wps),
                  wspec(wpp),
                  wspec(wout),
                  wspec(wr),
                  pl.BlockSpec(br.shape, const2)],
        out_specs=[pl.BlockSpec((tm, d), lambda i: (i, 0)),
                   pl.BlockSpec((tm * ROW_PANELS, LANES), lambda i: (i, 0)),
                   pl.BlockSpec((8, tm), lambda i: (0, i)),
                   pl.BlockSpec((1, ROUTE_LANES), const2)],
        out_shape=(jax.ShapeDtypeStruct((t, d), f32),
                   jax.ShapeDtypeStruct((t * ROW_PANELS, LANES), jnp.uint32),
                   jax.ShapeDtypeStruct((8, t), f32),
                   jax.ShapeDtypeStruct((1, ROUTE_LANES), f32)),
        scratch_shapes=[pltpu.VMEM((1, ROUTE_LANES), f32)],
        compiler_params=_cparams(("arbitrary",)),
        name="post",
    )(x2, a3, b2, mod3, g1, g2, wg, bg, wps, wpp, wout, wr, br)


def _row_copy(src_ref, dst_ref, src_row, dst_row, p, sem):
    src = src_ref.at[pl.ds(pl.multiple_of(src_row * p, p), p)]
    dst = dst_ref.at[pl.ds(pl.multiple_of(dst_row * p, p), p)]
    return pltpu.make_async_copy(src, dst, sem)


def _wait_rows(src_ref, dst_ref, n_rows, p, sem):
    pltpu.make_async_copy(src_ref.at[pl.ds(0, n_rows * p)], dst_ref.at[pl.ds(0, n_rows * p)],
                          sem).wait()


def _scatter_kernel(pos_ref, up_ref, xs_ref, sem, *, tm, p):
    def issue(g, c):
        r0 = g * ROW_DMA_UNROLL
        idx = [pos_ref[0, 0, r0 + u] for u in range(ROW_DMA_UNROLL)]
        for u in range(ROW_DMA_UNROLL):
            _row_copy(up_ref, xs_ref, r0 + u, idx[u], p, sem).start(priority=u % 2)
        return c

    lax.fori_loop(0, tm // ROW_DMA_UNROLL, issue, 0)
    _wait_rows(up_ref, xs_ref, tm, p, sem)


def _scatter_call(up, pos3, n_rows, tm, p):
    n_steps = up.shape[0] // (tm * p)
    return pl.pallas_call(
        functools.partial(_scatter_kernel, tm=tm, p=p),
        grid=(n_steps,),
        in_specs=[pl.BlockSpec((1, 1, tm), lambda i: (i, 0, 0), memory_space=pltpu.SMEM),
                  pl.BlockSpec((tm * p, LANES), lambda i: (i, 0))],
        out_specs=pl.BlockSpec(memory_space=pl.ANY),
        out_shape=jax.ShapeDtypeStruct((n_rows * p, LANES), up.dtype),
        scratch_shapes=[pltpu.SemaphoreType.DMA(())],
        compiler_params=_cparams(("arbitrary",)),
        name="scatter_rows",
    )(pos3, up)


def _experts_kernel(ea_ref, eb_ref, nt_ref, xs_ref, wgu_a_ref, wgu_b_ref, wd_a_ref, wd_b_ref,
                    ys_ref, *, d, d_exp, tr):
    i = pl.program_id(0)

    @pl.when(i < nt_ref[0])
    def _():
        px = d // 2 // LANES
        w = jnp.concatenate([xs_ref[pl.ds(j, tr, stride=ROW_PANELS), :] for j in range(px)],
                            axis=1)
        wts = lax.bitcast_convert_type(xs_ref[pl.ds(px, tr, stride=ROW_PANELS), :], f32)
        hi = lax.bitcast_convert_type(w & jnp.uint32(0xFFFF0000), f32)
        lo = lax.bitcast_convert_type(w << 16, f32)
        x = jnp.concatenate([hi, lo], axis=1).astype(bf16)

        def ffn(wgu_ref, wd_ref, wt):
            h = _dot(x, wgu_ref[...])
            h1 = h[:, 0:d_exp]
            act = h1 * jax.nn.sigmoid(h1) * h[:, d_exp:2 * d_exp]
            return wt * _dot(act.astype(bf16), wd_ref[...])

        y = ffn(wgu_a_ref, wd_a_ref, wts[:, 0:1]) + ffn(wgu_b_ref, wd_b_ref, wts[:, 1:2])
        _store_panels(ys_ref, y, tr)


def _experts_call(tile_ea, tile_eb, n_tiles, xs, wgu, wd, tr):
    n_exp, d, two_de = wgu.shape
    d_exp = two_de // 2
    assert d // LANES == ROW_PANELS
    n_rows = xs.shape[0] // ROW_PANELS
    nt_max = n_rows // tr

    def row_map(i, ea, eb, nt):
        return (jnp.minimum(i, nt[0] - 1), 0)

    def wa_map(i, ea, eb, nt):
        return (ea[jnp.minimum(i, nt[0] - 1)], 0, 0)

    def wb_map(i, ea, eb, nt):
        return (eb[jnp.minimum(i, nt[0] - 1)], 0, 0)

    grid_spec = pltpu.PrefetchScalarGridSpec(
        num_scalar_prefetch=3,
        grid=(nt_max,),
        in_specs=[pl.BlockSpec((tr * ROW_PANELS, LANES), row_map),
                  pl.BlockSpec((None, d, two_de), wa_map),
                  pl.BlockSpec((None, d, two_de), wb_map),
                  pl.BlockSpec((None, d_exp, d), wa_map),
                  pl.BlockSpec((None, d_exp, d), wb_map)],
        out_specs=pl.BlockSpec((tr * ROW_PANELS, LANES), row_map),
    )
    return pl.pallas_call(
        functools.partial(_experts_kernel, d=d, d_exp=d_exp, tr=tr),
        grid_spec=grid_spec,
        out_shape=jax.ShapeDtypeStruct((n_rows * ROW_PANELS, LANES), f32),
        compiler_params=_cparams(("arbitrary",)),
        name="experts",
    )(tile_ea, tile_eb, n_tiles, xs, wgu, wgu, wd, wd)


def _combine_kernel(pos_ref, posn_ref, h_ref, mod_ref, gf_ref, ys_ref, o_ref,
                    g_ref, sem, *, d, tm, n_steps, final_norm):
    i = pl.program_id(0)
    slot = i % 2
    py = d // LANES

    def issue(p_ref, s):
        def body(g, c):
            r0 = g * ROW_DMA_UNROLL
            idx = [p_ref[0, 0, r0 + u] for u in range(ROW_DMA_UNROLL)]
            for u in range(ROW_DMA_UNROLL):
                _row_copy(ys_ref, g_ref.at[s], idx[u], r0 + u, py,
                          sem.at[s]).start(priority=u % 2)
            return c

        lax.fori_loop(0, tm // ROW_DMA_UNROLL, body, 0)

    pl.when(i == 0)(lambda: issue(pos_ref, 0))
    pl.when(i + 1 < n_steps)(lambda: issue(posn_ref, 1 - slot))
    _wait_rows(ys_ref, g_ref.at[slot], tm, py, sem.at[slot])

    gate2 = mod_ref[:, 5 * d:6 * d]
    h = h_ref[...] + gate2 * _load_panels(g_ref.at[slot], tm, py)
    if final_norm:
        ms = jnp.mean(h * h, axis=-1, keepdims=True)
        h = h * lax.rsqrt(ms + EPS) * gf_ref[...]
    o_ref[...] = h


def _combine_call(pos3, h1, mod3, gf, ys, seq, tm, final_norm):
    t, d = h1.shape
    per = seq // tm
    n_steps = t // tm
    pos_spec = lambda f: pl.BlockSpec((1, 1, tm), f, memory_space=pltpu.SMEM)
    return pl.pallas_call(
        functools.partial(_combine_kernel, d=d, tm=tm, n_steps=n_steps, final_norm=final_norm),
        grid=(n_steps,),
        in_specs=[pos_spec(lambda i: (i, 0, 0)),
                  pos_spec(lambda i: (jnp.minimum(i + 1, n_steps - 1), 0, 0)),
                  pl.BlockSpec((tm, d), lambda i: (i, 0)),
                  pl.BlockSpec((None, 1, mod3.shape[2]), lambda i: (i // per, 0, 0)),
                  pl.BlockSpec((1, d), lambda i: (0, 0)),
                  pl.BlockSpec(memory_space=pl.ANY)],
        out_specs=pl.BlockSpec((tm, d), lambda i: (i, 0)),
        out_shape=jax.ShapeDtypeStruct((t, d), f32),
        scratch_shapes=[pltpu.VMEM((2, tm * (d // LANES), LANES), f32),
                        pltpu.SemaphoreType.DMA((2,))],
        compiler_params=_cparams(("arbitrary",)),
        name="combine",
    )(pos3, pos3, h1, mod3, gf, ys)


def _row_tile(seq, target):
    tm = min(target, seq)
    assert seq % tm == 0 and tm % 8 == 0
    return tm


def _layer(h2, mod3, p, bsz, seq, final_gain, final_norm):
    t, d = h2.shape
    g, n_p, n_h = p["ssm_b_re"].shape
    d_ssm = g * n_h
    d_pool = p["pool_scale"].shape[-1]
    n_grp = p["router_coarse_w"].shape[-1]
    n_exp = p["router_fine_w"].shape[-1]
    assert n_grp + n_exp <= ROUTE_LANES // 2 and seq % SSM_SLABS == 0
    tm = _row_tile(seq, 512)

    pw, bbt_re, bbt_im = _ssm_prep_call(
        p["ssm_lam_re"], p["ssm_lam_im"], p["ssm_log_dt"],
        jnp.swapaxes(p["ssm_b_re"], 1, 2), jnp.swapaxes(p["ssm_b_im"], 1, 2), SSM_SLABS)
    bbd = jnp.concatenate([_block_diag(bbt_re), _block_diag(bbt_im)], axis=1).astype(bf16)
    cbd = jnp.concatenate([_block_diag(jnp.swapaxes(p["ssm_c_re"], 1, 2)),
                           -_block_diag(jnp.swapaxes(p["ssm_c_im"], 1, 2))], axis=0).astype(bf16)
    pw2 = pw.reshape(SSM_SLABS, 2 * g * n_p)

    w_in, b_in = p["w_in"], p["b_in"]
    n_sp = d_ssm + d_pool
    us, upool = _in_proj_call(h2, mod3, p["norm1_g"].reshape(1, d), w_in[:, :n_sp].astype(bf16),
                              b_in[:n_sp].reshape(1, n_sp), seq, tm, d_ssm)
    a3, b2 = _mixers_call(us, upool, bbd, cbd, pw2, p["ssm_d"].reshape(1, d_ssm),
                          p["ssm_w_glu"].astype(bf16), p["ssm_b_glu"].reshape(1, d_ssm),
                          p["pool_w"].astype(bf16), p["pool_scale"].reshape(1, d_pool),
                          bsz, seq)

    wr = jnp.concatenate([p["router_coarse_w"], p["router_fine_w"]], axis=1)
    wr_hi = wr.astype(bf16)
    wr_lo = (wr - wr_hi.astype(f32)).astype(bf16)
    padc = ROUTE_LANES // 2 - wr.shape[1]
    wr_cat = jnp.concatenate([jnp.pad(wr_hi, ((0, 0), (0, padc))),
                              jnp.pad(wr_lo, ((0, 0), (0, padc)))], axis=1)
    br = jnp.pad(jnp.concatenate([p["router_coarse_b"], p["router_fine_b"]]),
                 (0, ROUTE_LANES - wr.shape[1])).reshape(1, ROUTE_LANES)

    h1, up, metat, cnt = _post_call(
        h2, a3, b2, mod3, p["norm1_g"].reshape(1, d), p["norm2_g"].reshape(1, d),
        w_in[:, n_sp:].astype(bf16), b_in[n_sp:].reshape(1, 2 * d),
        p["w_proj_ssm"].astype(bf16), p["w_proj_pool"].astype(bf16), p["w_out"].astype(bf16),
        wr_cat, br, seq, tm, tm, n_grp, n_exp)

    tr = EXPERT_TILE_ROWS
    epg = n_exp // n_grp
    pairs = [(a, b) for a in range(epg) for b in range(a + 1, epg)]
    n_cls = n_grp * len(pairs)
    assert n_cls <= ROUTE_LANES
    cls_ea = jnp.asarray([gi * epg + a for gi in range(n_grp) for a, _ in pairs], jnp.int32)
    cls_eb = jnp.asarray([gi * epg + b for gi in range(n_grp) for _, b in pairs], jnp.int32)
    counts = cnt[0, 0:n_cls].astype(jnp.int32)
    padded = ((counts + tr - 1) // tr) * tr
    ends = jnp.cumsum(padded)
    offs = ends - padded
    ids = jnp.arange(n_cls, dtype=jnp.int32)
    cls = metat[0].astype(jnp.int32)
    off = jnp.sum(jnp.where(cls[None, :] == ids[:, None], offs[:, None], 0), axis=0)
    pos3 = (off + metat[1].astype(jnp.int32)).reshape(t // tm, 1, tm)
    nt_max = t // tr + n_cls
    n_tiles = (ends[-1] // tr).astype(jnp.int32).reshape(1)
    tile_start = jnp.arange(nt_max, dtype=jnp.int32) * tr
    tile_cls = jnp.minimum(
        jnp.sum((ends[None, :] <= tile_start[:, None]).astype(jnp.int32), axis=1), n_cls - 1)

    xs = _scatter_call(up, pos3, nt_max * tr, tm, ROW_PANELS)
    wgu = jnp.concatenate([p["moe_w_gate"], p["moe_w_up"]], axis=-1).astype(bf16)
    ys = _experts_call(cls_ea[tile_cls], cls_eb[tile_cls], n_tiles, xs, wgu,
                       p["moe_w_down"].astype(bf16), tr)
    return _combine_call(pos3, h1, mod3, final_gain.reshape(1, d), ys, seq, tm, final_norm)


def kernel(x, c, w_mod, b_mod, norm1_g, w_in, b_in, ssm_lam_re, ssm_lam_im, ssm_log_dt, ssm_b_re, ssm_b_im, ssm_c_re, ssm_c_im, ssm_d, ssm_w_glu, ssm_b_glu, pool_w, pool_scale, w_proj_ssm, w_proj_pool, w_out, norm2_g, router_coarse_w, router_coarse_b, router_fine_w, router_fine_b, moe_w_gate, moe_w_up, moe_w_down, norm_f_g):
    bsz, seq, d = x.shape
    depth = w_mod.shape[0]
    per_layer = dict(
        norm1_g=norm1_g, w_in=w_in, b_in=b_in, ssm_lam_re=ssm_lam_re, ssm_lam_im=ssm_lam_im,
        ssm_log_dt=ssm_log_dt, ssm_b_re=ssm_b_re, ssm_b_im=ssm_b_im, ssm_c_re=ssm_c_re,
        ssm_c_im=ssm_c_im, ssm_d=ssm_d, ssm_w_glu=ssm_w_glu, ssm_b_glu=ssm_b_glu, pool_w=pool_w,
        pool_scale=pool_scale, w_proj_ssm=w_proj_ssm, w_proj_pool=w_proj_pool, w_out=w_out,
        norm2_g=norm2_g, router_coarse_w=router_coarse_w, router_coarse_b=router_coarse_b,
        router_fine_w=router_fine_w, router_fine_b=router_fine_b, moe_w_gate=moe_w_gate,
        moe_w_up=moe_w_up, moe_w_down=moe_w_down)
    h2 = x.reshape(bsz * seq, d)
    for l in range(depth):
        p = {k: v[l] for k, v in per_layer.items()}
        mod3 = _mod_call(c, w_mod[l], b_mod[l]).reshape(bsz, 1, N_MOD * d)
        h2 = _layer(h2, mod3, p, bsz, seq, norm_f_g, final_norm=(l == depth - 1))
    return h2.reshape(bsz, seq, d)
```

```python
import functools
import math

import jax
import jax.numpy as jnp
from jax import lax
from jax.experimental import pallas as pl
from jax.experimental.pallas import tpu as pltpu

EPS = 1e-6
POOL_WINDOWS = (2, 4, 8, 16)
TOP_K_FINE = 2
N_MOD = 6
LANES = 128
SSM_SLABS = 8
ROUTE_LANES = 128
ROW_DMA_UNROLL = 8
ROW_PANELS = 8
EXPERT_TILE_ROWS = 256
VMEM_LIMIT = 56 * 1024 * 1024
NEG = -1e30

f32 = jnp.float32
bf16 = jnp.bfloat16


def _cparams(sem):
    return pltpu.CompilerParams(dimension_semantics=sem, vmem_limit_bytes=VMEM_LIMIT)


def _dot(a, b):
    return jnp.dot(a, b, preferred_element_type=f32)


def _store_panels(ref, val, rows, row0=0):
    p = val.shape[1] // LANES
    for j in range(p):
        ref[pl.ds(row0 * p + j, rows, stride=p), :] = val[:, j * LANES:(j + 1) * LANES]


def _load_panels(ref, rows, p):
    return jnp.concatenate([ref[pl.ds(j, rows, stride=p), :] for j in range(p)], axis=1)


def _split_bf16(v):
    hi = v.astype(bf16)
    return hi, (v - hi.astype(f32)).astype(bf16)


def _mod_kernel(c_ref, w_ref, b_ref, o_ref):
    c = c_ref[...]
    a_hi, a_lo = _split_bf16(c * jax.nn.sigmoid(c))
    w_hi, w_lo = _split_bf16(w_ref[...])
    o_ref[...] = _dot(a_hi, w_hi) + (_dot(a_lo, w_hi) + _dot(a_hi, w_lo)) + b_ref[...]


def _mod_call(c, w_mod, b_mod):
    bsz, d = c.shape
    n = w_mod.shape[1]
    tn = d
    return pl.pallas_call(
        _mod_kernel,
        grid=(n // tn,),
        in_specs=[pl.BlockSpec((bsz, d), lambda j: (0, 0)),
                  pl.BlockSpec((d, tn), lambda j: (0, j)),
                  pl.BlockSpec((1, tn), lambda j: (0, j))],
        out_specs=pl.BlockSpec((bsz, tn), lambda j: (0, j)),
        out_shape=jax.ShapeDtypeStruct((bsz, n), f32),
        compiler_params=_cparams(("parallel",)),
        name="mod",
    )(c, w_mod, b_mod.reshape(1, n))


def _ssm_prep_kernel(lr_ref, li_ref, ldt_ref, btr_ref, bti_ref, ctr_ref, cti_ref,
                     pw_ref, bbd_ref, cbd_ref, *, n_pow, n_grp):
    n_h, n_p = btr_ref.shape[1], btr_ref.shape[2]
    gp = n_grp * n_p
    lr = lr_ref[...]
    li = li_ref[...]
    dt = jnp.exp(ldt_ref[...])
    for k in range(1, n_pow + 1):
        mag = jnp.exp(lr * dt * float(k))
        ang = li * dt * float(k)
        pw_ref[k - 1, 0] = mag * jnp.cos(ang)
        pw_ref[k - 1, 1] = mag * jnp.sin(ang)
    lb_re = pw_ref[0, 0]
    lb_im = pw_ref[0, 1]
    den = lr * lr + li * li
    nr = lb_re - 1.0
    f_re = (nr * lr + lb_im * li) / den
    f_im = (lb_im * lr - nr * li) / den
    bbd_ref[...] = jnp.zeros_like(bbd_ref)
    cbd_ref[...] = jnp.zeros_like(cbd_ref)
    for g in range(n_grp):
        fr, fi = f_re[g:g + 1, :], f_im[g:g + 1, :]
        hs, ps = pl.ds(g * n_h, n_h), pl.ds(g * n_p, n_p)
        bbd_ref[hs, ps] = fr * btr_ref[g] - fi * bti_ref[g]
        bbd_ref[hs, pl.ds(gp + g * n_p, n_p)] = fr * bti_ref[g] + fi * btr_ref[g]
        cbd_ref[ps, hs] = ctr_ref[g]
        cbd_ref[pl.ds(gp + g * n_p, n_p), hs] = -cti_ref[g]


def _ssm_prep_call(lam_re, lam_im, log_dt, bt_re, bt_im, ct_re, ct_im, n_pow):
    g, p = lam_re.shape
    h = bt_re.shape[1]
    return pl.pallas_call(
        functools.partial(_ssm_prep_kernel, n_pow=n_pow, n_grp=g),
        out_shape=(jax.ShapeDtypeStruct((n_pow, 2, g, p), f32),
                   jax.ShapeDtypeStruct((g * h, 2 * g * p), f32),
                   jax.ShapeDtypeStruct((2 * g * p, g * h), f32)),
        name="ssm_prep",
    )(lam_re, lam_im, log_dt.reshape(g, 1), bt_re, bt_im, ct_re, ct_im)


def _modulated_norm(x, gain, shift, scale):
    ms = jnp.mean(x * x, axis=-1, keepdims=True)
    y = x * lax.rsqrt(ms + EPS) * gain
    return y * (1.0 + scale) + shift


def _cmul_add(ar, ai, xr, xi, br, bi):
    return ar * xr - ai * xi + br, ar * xi + ai * xr + bi


def _mixers_kernel(x_ref, mod_ref, g1_ref, wsp_ref, bsp_ref,
                   bbd_ref, cbd_ref, pw_ref, dsk_ref, wglu_ref, bglu_ref, pw_pool_ref, psc_ref,
                   a_ref, b_ref, us_ref, up_ref, e_ref, s_ref, y_ref,
                   *, seq, d, d_ssm, d_pool, n_state, slabs, ts):
    nc = seq // slabs
    nl = d_ssm // LANES
    ns = n_state

    for r0 in range(0, seq, ts):
        rows = pl.ds(r0, ts)
        u = _modulated_norm(x_ref[rows, :], g1_ref[...], mod_ref[:, 0:d], mod_ref[:, d:2 * d])
        r = _dot(u.astype(bf16), wsp_ref[...]) + bsp_ref[...]
        for j in range(nl):
            us_ref[j, rows, :] = r[:, j * LANES:(j + 1) * LANES]
        up_ref[rows, :] = r[:, d_ssm:]

    dsk = dsk_ref[...]
    l1r = pw_ref[0:1, 0:ns]
    l1i = pw_ref[0:1, ns:2 * ns]

    xr = xi = None
    for t in range(slabs):
        ut = jnp.concatenate([us_ref[j, pl.ds(t, nc, stride=slabs), :] for j in range(nl)], axis=1)
        bu = _dot(ut.astype(bf16), bbd_ref[...])
        br, bi = bu[:, 0:ns], bu[:, ns:2 * ns]
        if t == 0:
            xr, xi = br, bi
        else:
            xr, xi = _cmul_add(l1r, l1i, xr, xi, br, bi)
        xc = jnp.concatenate([xr, xi], axis=1).astype(bf16)
        y_ref[t] = _dot(xc, cbd_ref[...]) + dsk * ut
    e_ref[:, 0:ns] = xr
    e_ref[:, ns:2 * ns] = xi

    lLr = pw_ref[slabs - 1:slabs, 0:ns]
    lLi = pw_ref[slabs - 1:slabs, ns:2 * ns]

    def chunk_step(c, carry):
        sr, si = carry
        s_ref[pl.ds(c, 1), 0:ns] = sr
        s_ref[pl.ds(c, 1), ns:2 * ns] = si
        er = e_ref[pl.ds(c, 1), 0:ns]
        ei = e_ref[pl.ds(c, 1), ns:2 * ns]
        return _cmul_add(lLr, lLi, sr, si, er, ei)

    zero = jnp.zeros((1, ns), f32)
    lax.fori_loop(0, nc, chunk_step, (zero, zero))

    sr = s_ref[:, 0:ns]
    si = s_ref[:, ns:2 * ns]
    for t in range(slabs):
        pr = pw_ref[t:t + 1, 0:ns]
        pi = pw_ref[t:t + 1, ns:2 * ns]
        zr = pr * sr - pi * si
        zi = pr * si + pi * sr
        zc = jnp.concatenate([zr, zi], axis=1).astype(bf16)
        y = y_ref[t] + _dot(zc, cbd_ref[...])
        y = jax.nn.gelu(y)
        z = y * jax.nn.sigmoid(_dot(y.astype(bf16), wglu_ref[...]) + bglu_ref[...])
        for j in range(nl):
            a_ref[j, pl.ds(t, nc, stride=slabs), :] = z[:, j * LANES:(j + 1) * LANES]

    gc = d_pool // len(POOL_WINDOWS)
    row = lax.broadcasted_iota(jnp.int32, (seq, gc), 0)
    for gi, w in enumerate(POOL_WINDOWS):
        lo = gi * gc
        v = up_ref[:, lo:lo + gc]
        acc = v
        span = 1
        while span < w:
            acc = acc + jnp.where(row >= span, pltpu.roll(acc, span, axis=0), 0.0)
            span *= 2
        cnt = jnp.minimum(row + 1, w).astype(f32)
        m = acc / cnt - v
        yg = _dot(m.astype(bf16), pw_pool_ref[gi])
        b_ref[:, lo:lo + gc] = yg * psc_ref[:, lo:lo + gc]


def _mixers_call(x2, mod3, g1, w_sp, b_sp, bbd, cbd, pw, dsk, wglu, bglu, pool_w, pool_scale,
                 bsz, seq, ts):
    t, d = x2.shape
    d_ssm = dsk.shape[1]
    d_pool = pool_scale.shape[1]
    nl = d_ssm // LANES
    ns2 = bbd.shape[1]
    slabs = SSM_SLABS
    nc = seq // slabs
    kern = functools.partial(_mixers_kernel, seq=seq, d=d, d_ssm=d_ssm, d_pool=d_pool,
                             n_state=ns2 // 2, slabs=slabs, ts=ts)
    const2 = lambda b: (0, 0)
    return pl.pallas_call(
        kern,
        grid=(bsz,),
        in_specs=[pl.BlockSpec((seq, d), lambda b: (b, 0)),
                  pl.BlockSpec((None, 1, mod3.shape[2]), lambda b: (b, 0, 0)),
                  pl.BlockSpec((1, d), const2),
                  pl.BlockSpec(w_sp.shape, const2),
                  pl.BlockSpec(b_sp.shape, const2),
                  pl.BlockSpec(bbd.shape, const2),
                  pl.BlockSpec(cbd.shape, const2),
                  pl.BlockSpec(pw.shape, const2),
                  pl.BlockSpec(dsk.shape, const2),
                  pl.BlockSpec(wglu.shape, const2),
                  pl.BlockSpec(bglu.shape, const2),
                  pl.BlockSpec(pool_w.shape, lambda b: (0, 0, 0)),
                  pl.BlockSpec(pool_scale.shape, const2)],
        out_specs=[pl.BlockSpec((nl, seq, LANES), lambda b: (0, b, 0)),
                   pl.BlockSpec((seq, d_pool), lambda b: (b, 0))],
        out_shape=(jax.ShapeDtypeStruct((nl, t, LANES), f32),
                   jax.ShapeDtypeStruct((t, d_pool), f32)),
        scratch_shapes=[pltpu.VMEM((nl, seq, LANES), f32),
                        pltpu.VMEM((seq, d_pool), f32),
                        pltpu.VMEM((nc, ns2), f32),
                        pltpu.VMEM((nc, ns2), f32),
                        pltpu.VMEM((slabs, nc, d_ssm), f32)],
        compiler_params=_cparams(("parallel",)),
        name="mixers",
    )(x2, mod3, g1, w_sp, b_sp, bbd, cbd, pw, dsk, wglu, bglu, pool_w, pool_scale)


def _post_kernel(x_ref, a_ref, b_ref, mod_ref, g1_ref, g2_ref, wg_ref, bg_ref, wps_ref, wpp_ref,
                 wout_ref, wr_ref, br_ref,
                 h_ref, up_ref, metat_ref, cnt_ref, carry_ref,
                 *, d, d_ssm, n_grp, n_exp, tm, ts):
    i = pl.program_id(0)

    @pl.when(i == 0)
    def _():
        carry_ref[...] = jnp.zeros_like(carry_ref)

    for r0 in range(0, tm, ts):
        _post_rows(x_ref, a_ref, b_ref, mod_ref, g1_ref, g2_ref, wg_ref, bg_ref, wps_ref, wpp_ref,
                   wout_ref, wr_ref, br_ref, h_ref, up_ref, metat_ref, carry_ref,
                   d=d, d_ssm=d_ssm, n_grp=n_grp, n_exp=n_exp, r0=r0, tm=ts)
    cnt_ref[...] = carry_ref[...]


def _post_rows(x_ref, a_ref, b_ref, mod_ref, g1_ref, g2_ref, wg_ref, bg_ref, wps_ref, wpp_ref,
               wout_ref, wr_ref, br_ref, h_ref, up_ref, metat_ref, carry_ref,
               *, d, d_ssm, n_grp, n_exp, r0, tm):
    rows = pl.ds(r0, tm)
    x = x_ref[rows, :]
    shift1, scale1, gate1 = mod_ref[:, 0:d], mod_ref[:, d:2 * d], mod_ref[:, 2 * d:3 * d]
    shift2, scale2 = mod_ref[:, 3 * d:4 * d], mod_ref[:, 4 * d:5 * d]
    u = _modulated_norm(x, g1_ref[...], shift1, scale1).astype(bf16)
    gates = _dot(u, wg_ref[...]) + bg_ref[...]
    a = jnp.concatenate([a_ref[j, rows, :] for j in range(d_ssm // LANES)], axis=1).astype(bf16)
    b = b_ref[rows, :].astype(bf16)
    merged = (jax.nn.sigmoid(gates[:, 0:d]) * _dot(a, wps_ref[...])
              + jax.nn.sigmoid(gates[:, d:2 * d]) * _dot(b, wpp_ref[...]))
    h = x + gate1 * _dot(merged.astype(bf16), wout_ref[...])
    h_ref[rows, :] = h

    u2 = _modulated_norm(h, g2_ref[...], shift2, scale2)

    half = d // 2
    hi_bits = lax.bitcast_convert_type(u2[:, 0:half].astype(bf16).astype(f32), jnp.uint32)
    lo_bits = lax.bitcast_convert_type(u2[:, half:d].astype(bf16).astype(f32), jnp.uint32)
    words = hi_bits | (lo_bits >> 16)

    u_hi = u2.astype(bf16)
    u_lo = (u2 - u_hi.astype(f32)).astype(bf16)
    r1 = _dot(u_hi, wr_ref[...])
    r2 = _dot(u_lo, wr_ref[...])
    lg = r1 + pltpu.roll(r1, ROUTE_LANES // 2, axis=1) + r2 + br_ref[...]

    lane = lax.broadcasted_iota(jnp.int32, (tm, ROUTE_LANES), 1).astype(f32)
    big = float(ROUTE_LANES)
    epg = float(n_exp // n_grp)
    is_c = lane < n_grp
    cl = jnp.where(is_c, lg, NEG)
    cmax = jnp.max(cl, axis=-1, keepdims=True)
    grp = jnp.min(jnp.where(cl == cmax, lane, big), axis=-1, keepdims=True)
    p_grp = 1.0 / jnp.sum(jnp.where(is_c, jnp.exp(cl - cmax), 0.0), axis=-1, keepdims=True)

    f_lo = n_grp + grp * epg
    fl = jnp.where((lane >= f_lo) & (lane < f_lo + epg), lg, NEG)
    f1 = jnp.max(fl, axis=-1, keepdims=True)
    i1 = jnp.min(jnp.where(fl == f1, lane, big), axis=-1, keepdims=True)
    fl2 = jnp.where(lane == i1, NEG, fl)
    f2 = jnp.max(fl2, axis=-1, keepdims=True)
    i2 = jnp.min(jnp.where(fl2 == f2, lane, big), axis=-1, keepdims=True)
    t2 = jnp.exp(f2 - f1)
    w0 = p_grp / (1.0 + t2)
    w1 = p_grp * t2 / (1.0 + t2)

    j0 = i1 - f_lo
    j1 = i2 - f_lo
    first = j0 < j1
    ja = jnp.minimum(j0, j1)
    jb = jnp.maximum(j0, j1)
    wa = jnp.where(first, w0, w1)
    wb = jnp.where(first, w1, w0)
    n_pair = epg * (epg - 1.0) * 0.5
    cls = grp * n_pair + ja * (2.0 * epg - ja - 1.0) * 0.5 + (jb - ja - 1.0)

    sel = lane == cls
    oh = jnp.where(sel, 1.0, 0.0)
    r_i = lax.broadcasted_iota(jnp.int32, (tm, tm), 0)
    c_i = lax.broadcasted_iota(jnp.int32, (tm, tm), 1)
    tri = jnp.where(c_i < r_i, 1.0, 0.0).astype(bf16)
    before = carry_ref[...] + _dot(tri, oh.astype(bf16))
    rank = jnp.sum(jnp.where(sel, before, 0.0), axis=-1, keepdims=True)
    carry_ref[...] = carry_ref[...] + jnp.sum(oh, axis=0, keepdims=True)

    meta = jnp.zeros((tm, ROUTE_LANES), f32)
    for k, val in enumerate((cls, rank, wa, wb)):
        meta = jnp.where(lane == float(k), val, meta)

    wts = jnp.where(lane == 0.0, wa, jnp.where(lane == 1.0, wb, 0.0))
    pad = jnp.zeros((tm, (ROW_PANELS - 1) * LANES - half), jnp.uint32)
    row = jnp.concatenate([words, lax.bitcast_convert_type(wts, jnp.uint32), pad], axis=1)
    _store_panels(up_ref, row, tm, r0)
    metat_ref[:, rows] = meta.T[0:8, :]


def _post_call(x2, a3, b2, mod3, g1, g2, wg, bg, wps, wpp, wout, wr, br, seq, tm, ts,
               n_grp, n_exp):
    t, d = x2.shape
    d_ssm = wps.shape[0]
    nl = a3.shape[0]
    per = seq // tm
    kern = functools.partial(_post_kernel, d=d, d_ssm=d_ssm, n_grp=n_grp, n_exp=n_exp,
                             tm=tm, ts=ts)
    const2 = lambda i: (0, 0)
    wspec = lambda w: pl.BlockSpec(w.shape, const2, pipeline_mode=pl.Buffered(1))
    return pl.pallas_call(
        kern,
        grid=(t // tm,),
        in_specs=[pl.BlockSpec((tm, d), lambda i: (i, 0)),
                  pl.BlockSpec((nl, tm, LANES), lambda i: (0, i, 0)),
                  pl.BlockSpec((tm, b2.shape[1]), lambda i: (i, 0)),
                  pl.BlockSpec((None, 1, mod3.shape[2]), lambda i: (i // per, 0, 0)),
                  pl.BlockSpec((1, d), const2),
                  pl.BlockSpec((1, d), const2),
                  wspec(wg),
                  pl.BlockSpec(bg.shape, const2),
                  wspec(wps),
                  wspec(wpp),
                  wspec(wout),
                  wspec(wr),
                  pl.BlockSpec(br.shape, const2)],
        out_specs=[pl.BlockSpec((tm, d), lambda i: (i, 0)),
                   pl.BlockSpec((tm * ROW_PANELS, LANES), lambda i: (i, 0)),
                   pl.BlockSpec((8, tm), lambda i: (0, i)),
                   pl.BlockSpec((1, ROUTE_LANES), const2)],
        out_shape=(jax.ShapeDtypeStruct((t, d), f32),
                   jax.ShapeDtypeStruct((t * ROW_PANELS, LANES), jnp.uint32),
                   jax.ShapeDtypeStruct((8, t), f32),
                   jax.ShapeDtypeStruct((1, ROUTE_LANES), f32)),
        scratch_shapes=[pltpu.VMEM((1, ROUTE_LANES), f32)],
        compiler_params=_cparams(("arbitrary",)),
        name="post",
    )(x2, a3, b2, mod3, g1, g2, wg, bg, wps, wpp, wout, wr, br)


def _row_copy(src_ref, dst_ref, src_row, dst_row, p, sem):
    src = src_ref.at[pl.ds(pl.multiple_of(src_row * p, p), p)]
    dst = dst_ref.at[pl.ds(pl.multiple_of(dst_row * p, p), p)]
    return pltpu.make_async_copy(src, dst, sem)


def _wait_rows(src_ref, dst_ref, n_rows, p, sem):
    pltpu.make_async_copy(src_ref.at[pl.ds(0, n_rows * p)], dst_ref.at[pl.ds(0, n_rows * p)],
                          sem).wait()


def _scatter_kernel(pos_ref, up_ref, xs_ref, sem, *, tm, p):
    def issue(g, c):
        r0 = g * ROW_DMA_UNROLL
        idx = [pos_ref[0, 0, r0 + u] for u in range(ROW_DMA_UNROLL)]
        for u in range(ROW_DMA_UNROLL):
            _row_copy(up_ref, xs_ref, r0 + u, idx[u], p, sem).start(priority=u % 2)
        return c

    lax.fori_loop(0, tm // ROW_DMA_UNROLL, issue, 0)
    _wait_rows(up_ref, xs_ref, tm, p, sem)


def _scatter_call(up, pos3, n_rows, tm, p):
    n_steps = up.shape[0] // (tm * p)
    return pl.pallas_call(
        functools.partial(_scatter_kernel, tm=tm, p=p),
        grid=(n_steps,),
        in_specs=[pl.BlockSpec((1, 1, tm), lambda i: (i, 0, 0), memory_space=pltpu.SMEM),
                  pl.BlockSpec((tm * p, LANES), lambda i: (i, 0))],
        out_specs=pl.BlockSpec(memory_space=pl.ANY),
        out_shape=jax.ShapeDtypeStruct((n_rows * p, LANES), up.dtype),
        scratch_shapes=[pltpu.SemaphoreType.DMA(())],
        compiler_params=_cparams(("arbitrary",)),
        name="scatter_rows",
    )(pos3, up)


def _experts_kernel(ea_ref, eb_ref, nt_ref, xs_ref, wgu_a_ref, wgu_b_ref, wd_a_ref, wd_b_ref,
                    ys_ref, *, d, d_exp, tr):
    i = pl.program_id(0)

    @pl.when(i < nt_ref[0])
    def _():
        px = d // 2 // LANES
        w = jnp.concatenate([xs_ref[pl.ds(j, tr, stride=ROW_PANELS), :] for j in range(px)],
                            axis=1)
        wts = lax.bitcast_convert_type(xs_ref[pl.ds(px, tr, stride=ROW_PANELS), :], f32)
        hi = lax.bitcast_convert_type(w & jnp.uint32(0xFFFF0000), f32)
        lo = lax.bitcast_convert_type(w << 16, f32)
        x = jnp.concatenate([hi, lo], axis=1).astype(bf16)

        def ffn(wgu_ref, wd_ref, wt):
            h = _dot(x, wgu_ref[...])
            h1 = h[:, 0:d_exp]
            act = h1 * jax.nn.sigmoid(h1) * h[:, d_exp:2 * d_exp]
            return wt * _dot(act.astype(bf16), wd_ref[...])

        y = ffn(wgu_a_ref, wd_a_ref, wts[:, 0:1]) + ffn(wgu_b_ref, wd_b_ref, wts[:, 1:2])
        _store_panels(ys_ref, y, tr)


def _experts_call(tile_ea, tile_eb, n_tiles, xs, wgu, wd, tr):
    n_exp, d, two_de = wgu.shape
    d_exp = two_de // 2
    assert d // LANES == ROW_PANELS
    n_rows = xs.shape[0] // ROW_PANELS
    nt_max = n_rows // tr

    def row_map(i, ea, eb, nt):
        return (jnp.minimum(i, nt[0] - 1), 0)

    def wa_map(i, ea, eb, nt):
        return (ea[jnp.minimum(i, nt[0] - 1)], 0, 0)

    def wb_map(i, ea, eb, nt):
        return (eb[jnp.minimum(i, nt[0] - 1)], 0, 0)

    grid_spec = pltpu.PrefetchScalarGridSpec(
        num_scalar_prefetch=3,
        grid=(nt_max,),
        in_specs=[pl.BlockSpec((tr * ROW_PANELS, LANES), row_map),
                  pl.BlockSpec((None, d, two_de), wa_map),
                  pl.BlockSpec((None, d, two_de), wb_map),
                  pl.BlockSpec((None, d_exp, d), wa_map),
                  pl.BlockSpec((None, d_exp, d), wb_map)],
        out_specs=pl.BlockSpec((tr * ROW_PANELS, LANES), row_map),
    )
    return pl.pallas_call(
        functools.partial(_experts_kernel, d=d, d_exp=d_exp, tr=tr),
        grid_spec=grid_spec,
        out_shape=jax.ShapeDtypeStruct((n_rows * ROW_PANELS, LANES), f32),
        compiler_params=_cparams(("arbitrary",)),
        name="experts",
    )(tile_ea, tile_eb, n_tiles, xs, wgu, wgu, wd, wd)


def _combine_kernel(pos_ref, posn_ref, h_ref, mod_ref, gf_ref, ys_ref, o_ref,
                    g_ref, sem, *, d, tm, n_steps, final_norm):
    i = pl.program_id(0)
    slot = i % 2
    py = d // LANES

    def issue(p_ref, s):
        def body(g, c):
            r0 = g * ROW_DMA_UNROLL
            idx = [p_ref[0, 0, r0 + u] for u in range(ROW_DMA_UNROLL)]
            for u in range(ROW_DMA_UNROLL):
                _row_copy(ys_ref, g_ref.at[s], idx[u], r0 + u, py,
                          sem.at[s]).start(priority=u % 2)
            return c

        lax.fori_loop(0, tm // ROW_DMA_UNROLL, body, 0)

    pl.when(i == 0)(lambda: issue(pos_ref, 0))
    pl.when(i + 1 < n_steps)(lambda: issue(posn_ref, 1 - slot))
    _wait_rows(ys_ref, g_ref.at[slot], tm, py, sem.at[slot])

    gate2 = mod_ref[:, 5 * d:6 * d]
    h = h_ref[...] + gate2 * _load_panels(g_ref.at[slot], tm, py)
    if final_norm:
        ms = jnp.mean(h * h, axis=-1, keepdims=True)
        h = h * lax.rsqrt(ms + EPS) * gf_ref[...]
    o_ref[...] = h


def _combine_call(pos3, h1, mod3, gf, ys, seq, tm, final_norm):
    t, d = h1.shape
    per = seq // tm
    n_steps = t // tm
    pos_spec = lambda f: pl.BlockSpec((1, 1, tm), f, memory_space=pltpu.SMEM)
    return pl.pallas_call(
        functools.partial(_combine_kernel, d=d, tm=tm, n_steps=n_steps, final_norm=final_norm),
        grid=(n_steps,),
        in_specs=[pos_spec(lambda i: (i, 0, 0)),
                  pos_spec(lambda i: (jnp.minimum(i + 1, n_steps - 1), 0, 0)),
                  pl.BlockSpec((tm, d), lambda i: (i, 0)),
                  pl.BlockSpec((None, 1, mod3.shape[2]), lambda i: (i // per, 0, 0)),
                  pl.BlockSpec((1, d), lambda i: (0, 0)),
                  pl.BlockSpec(memory_space=pl.ANY)],
        out_specs=pl.BlockSpec((tm, d), lambda i: (i, 0)),
        out_shape=jax.ShapeDtypeStruct((t, d), f32),
        scratch_shapes=[pltpu.VMEM((2, tm * (d // LANES), LANES), f32),
                        pltpu.SemaphoreType.DMA((2,))],
        compiler_params=_cparams(("arbitrary",)),
        name="combine",
    )(pos3, pos3, h1, mod3, gf, ys)


def _row_tile(seq, target):
    tm = min(target, seq)
    assert seq % tm == 0 and tm % 8 == 0
    return tm


def _layer(h2, mod3, p, bsz, seq, final_gain, final_norm):
    t, d = h2.shape
    g, n_p, n_h = p["ssm_b_re"].shape
    d_ssm = g * n_h
    d_pool = p["pool_scale"].shape[-1]
    n_grp = p["router_coarse_w"].shape[-1]
    n_exp = p["router_fine_w"].shape[-1]
    assert n_grp + n_exp <= ROUTE_LANES // 2 and seq % SSM_SLABS == 0
    tm = _row_tile(seq, 512)

    swap = lambda v: jnp.swapaxes(v, 1, 2)
    pw, bbd, cbd = _ssm_prep_call(
        p["ssm_lam_re"], p["ssm_lam_im"], p["ssm_log_dt"], swap(p["ssm_b_re"]),
        swap(p["ssm_b_im"]), swap(p["ssm_c_re"]), swap(p["ssm_c_im"]), SSM_SLABS)
    bbd, cbd = bbd.astype(bf16), cbd.astype(bf16)
    pw2 = pw.reshape(SSM_SLABS, 2 * g * n_p)

    w_in, b_in = p["w_in"], p["b_in"]
    n_sp = d_ssm + d_pool
    a3, b2 = _mixers_call(h2, mod3, p["norm1_g"].reshape(1, d), w_in[:, :n_sp].astype(bf16),
                          b_in[:n_sp].reshape(1, n_sp), bbd, cbd, pw2,
                          p["ssm_d"].reshape(1, d_ssm),
                          p["ssm_w_glu"].astype(bf16), p["ssm_b_glu"].reshape(1, d_ssm),
                          p["pool_w"].astype(bf16), p["pool_scale"].reshape(1, d_pool),
                          bsz, seq, tm)

    wr = jnp.concatenate([p["router_coarse_w"], p["router_fine_w"]], axis=1)
    wr_hi = wr.astype(bf16)
    wr_lo = (wr - wr_hi.astype(f32)).astype(bf16)
    padc = ROUTE_LANES // 2 - wr.shape[1]
    wr_cat = jnp.concatenate([jnp.pad(wr_hi, ((0, 0), (0, padc))),
                              jnp.pad(wr_lo, ((0, 0), (0, padc)))], axis=1)
    br = jnp.pad(jnp.concatenate([p["router_coarse_b"], p["router_fine_b"]]),
                 (0, ROUTE_LANES - wr.shape[1])).reshape(1, ROUTE_LANES)

    h1, up, metat, cnt = _post_call(
        h2, a3, b2, mod3, p["norm1_g"].reshape(1, d), p["norm2_g"].reshape(1, d),
        w_in[:, n_sp:].astype(bf16), b_in[n_sp:].reshape(1, 2 * d),
        p["w_proj_ssm"].astype(bf16), p["w_proj_pool"].astype(bf16), p["w_out"].astype(bf16),
        wr_cat, br, seq, tm, tm, n_grp, n_exp)

    tr = EXPERT_TILE_ROWS
    epg = n_exp // n_grp
    pairs = [(a, b) for a in range(epg) for b in range(a + 1, epg)]
    n_cls = n_grp * len(pairs)
    assert n_cls <= ROUTE_LANES
    cls_ea = jnp.asarray([gi * epg + a for gi in range(n_grp) for a, _ in pairs], jnp.int32)
    cls_eb = jnp.asarray([gi * epg + b for gi in range(n_grp) for _, b in pairs], jnp.int32)
    counts = cnt[0, 0:n_cls].astype(jnp.int32)
    padded = ((counts + tr - 1) // tr) * tr
    ends = jnp.cumsum(padded)
    offs = ends - padded
    ids = jnp.arange(n_cls, dtype=jnp.int32)
    cls = metat[0].astype(jnp.int32)
    off = jnp.sum(jnp.where(cls[None, :] == ids[:, None], offs[:, None], 0), axis=0)
    pos3 = (off + metat[1].astype(jnp.int32)).reshape(t // tm, 1, tm)
    nt_max = t // tr + n_cls
    n_tiles = (ends[-1] // tr).astype(jnp.int32).reshape(1)
    tile_start = jnp.arange(nt_max, dtype=jnp.int32) * tr
    tile_cls = jnp.minimum(
        jnp.sum((ends[None, :] <= tile_start[:, None]).astype(jnp.int32), axis=1), n_cls - 1)

    xs = _scatter_call(up, pos3, nt_max * tr, tm, ROW_PANELS)
    wgu = jnp.concatenate([p["moe_w_gate"], p["moe_w_up"]], axis=-1).astype(bf16)
    ys = _experts_call(cls_ea[tile_cls], cls_eb[tile_cls], n_tiles, xs, wgu,
                       p["moe_w_down"].astype(bf16), tr)
    return _combine_call(pos3, h1, mod3, final_gain.reshape(1, d), ys, seq, tm, final_norm)


def kernel(x, c, w_mod, b_mod, norm1_g, w_in, b_in, ssm_lam_re, ssm_lam_im, ssm_log_dt, ssm_b_re, ssm_b_im, ssm_c_re, ssm_c_im, ssm_d, ssm_w_glu, ssm_b_glu, pool_w, pool_scale, w_proj_ssm, w_proj_pool, w_out, norm2_g, router_coarse_w, router_coarse_b, router_fine_w, router_fine_b, moe_w_gate, moe_w_up, moe_w_down, norm_f_g):
    bsz, seq, d = x.shape
    depth = w_mod.shape[0]
    per_layer = dict(
        norm1_g=norm1_g, w_in=w_in, b_in=b_in, ssm_lam_re=ssm_lam_re, ssm_lam_im=ssm_lam_im,
        ssm_log_dt=ssm_log_dt, ssm_b_re=ssm_b_re, ssm_b_im=ssm_b_im, ssm_c_re=ssm_c_re,
        ssm_c_im=ssm_c_im, ssm_d=ssm_d, ssm_w_glu=ssm_w_glu, ssm_b_glu=ssm_b_glu, pool_w=pool_w,
        pool_scale=pool_scale, w_proj_ssm=w_proj_ssm, w_proj_pool=w_proj_pool, w_out=w_out,
        norm2_g=norm2_g, router_coarse_w=router_coarse_w, router_coarse_b=router_coarse_b,
        router_fine_w=router_fine_w, router_fine_b=router_fine_b, moe_w_gate=moe_w_gate,
        moe_w_up=moe_w_up, moe_w_down=moe_w_down)
    h2 = x.reshape(bsz * seq, d)
    for l in range(depth):
        p = {k: v[l] for k, v in per_layer.items()}
        mod3 = _mod_call(c, w_mod[l], b_mod[l]).reshape(bsz, 1, N_MOD * d)
        h2 = _layer(h2, mod3, p, bsz, seq, norm_f_g, final_norm=(l == depth - 1))
    return h2.reshape(bsz, seq, d)
```

```python
import functools
import math

import jax
import jax.numpy as jnp
from jax import lax
from jax.experimental import pallas as pl
from jax.experimental.pallas import tpu as pltpu

EPS = 1e-6
POOL_WINDOWS = (2, 4, 8, 16)
TOP_K_FINE = 2
N_MOD = 6
LANES = 128
SSM_SLABS = 8
ROUTE_LANES = 128
ROW_DMA_UNROLL = 8
ROW_PANELS = 8
EXPERT_TILE_ROWS = 256
VMEM_LIMIT = 56 * 1024 * 1024
NEG = -1e30

f32 = jnp.float32
bf16 = jnp.bfloat16


def _cparams(sem):
    return pltpu.CompilerParams(dimension_semantics=sem, vmem_limit_bytes=VMEM_LIMIT)


def _dot(a, b):
    return jnp.dot(a, b, preferred_element_type=f32)


def _dot_split_rows(a, b):
    h = a.shape[0] // 2
    return jnp.concatenate([_dot(a[:h], b), _dot(a[h:], b)], axis=0)


def _store_panels(ref, val, rows, row0=0):
    p = val.shape[1] // LANES
    for j in range(p):
        ref[pl.ds(row0 * p + j, rows, stride=p), :] = val[:, j * LANES:(j + 1) * LANES]


def _load_panels(ref, rows, p):
    return jnp.concatenate([ref[pl.ds(j, rows, stride=p), :] for j in range(p)], axis=1)


def _split_bf16(v):
    hi = v.astype(bf16)
    return hi, (v - hi.astype(f32)).astype(bf16)


def _mod_kernel(c_ref, w_ref, b_ref, o_ref):
    c = c_ref[...]
    a_hi, a_lo = _split_bf16(c * jax.nn.sigmoid(c))
    w_hi, w_lo = _split_bf16(w_ref[...])
    o_ref[...] = _dot(a_hi, w_hi) + (_dot(a_lo, w_hi) + _dot(a_hi, w_lo)) + b_ref[...]


def _mod_call(c, w_mod, b_mod):
    bsz, d = c.shape
    n = w_mod.shape[1]
    tn = d
    return pl.pallas_call(
        _mod_kernel,
        grid=(n // tn,),
        in_specs=[pl.BlockSpec((bsz, d), lambda j: (0, 0)),
                  pl.BlockSpec((d, tn), lambda j: (0, j)),
                  pl.BlockSpec((1, tn), lambda j: (0, j))],
        out_specs=pl.BlockSpec((bsz, tn), lambda j: (0, j)),
        out_shape=jax.ShapeDtypeStruct((bsz, n), f32),
        compiler_params=_cparams(("parallel",)),
        name="mod",
    )(c, w_mod, b_mod.reshape(1, n))


def _ssm_prep_kernel(lr_ref, li_ref, ldt_ref, btr_ref, bti_ref, ctr_ref, cti_ref,
                     pw_ref, bbd_ref, cbd_ref, *, n_pow, n_grp):
    n_h, n_p = btr_ref.shape[1], btr_ref.shape[2]
    gp = n_grp * n_p
    lr = lr_ref[...]
    li = li_ref[...]
    dt = jnp.exp(ldt_ref[...])
    for k in range(1, n_pow + 1):
        mag = jnp.exp(lr * dt * float(k))
        ang = li * dt * float(k)
        pw_ref[k - 1, 0] = mag * jnp.cos(ang)
        pw_ref[k - 1, 1] = mag * jnp.sin(ang)
    lb_re = pw_ref[0, 0]
    lb_im = pw_ref[0, 1]
    den = lr * lr + li * li
    nr = lb_re - 1.0
    f_re = (nr * lr + lb_im * li) / den
    f_im = (lb_im * lr - nr * li) / den
    bbd_ref[...] = jnp.zeros_like(bbd_ref)
    cbd_ref[...] = jnp.zeros_like(cbd_ref)
    for g in range(n_grp):
        fr, fi = f_re[g:g + 1, :], f_im[g:g + 1, :]
        hs, ps = pl.ds(g * n_h, n_h), pl.ds(g * n_p, n_p)
        bbd_ref[hs, ps] = fr * btr_ref[g] - fi * bti_ref[g]
        bbd_ref[hs, pl.ds(gp + g * n_p, n_p)] = fr * bti_ref[g] + fi * btr_ref[g]
        cbd_ref[ps, hs] = ctr_ref[g]
        cbd_ref[pl.ds(gp + g * n_p, n_p), hs] = -cti_ref[g]


def _ssm_prep_call(lam_re, lam_im, log_dt, bt_re, bt_im, ct_re, ct_im, n_pow):
    g, p = lam_re.shape
    h = bt_re.shape[1]
    return pl.pallas_call(
        functools.partial(_ssm_prep_kernel, n_pow=n_pow, n_grp=g),
        out_shape=(jax.ShapeDtypeStruct((n_pow, 2, g, p), f32),
                   jax.ShapeDtypeStruct((g * h, 2 * g * p), f32),
                   jax.ShapeDtypeStruct((2 * g * p, g * h), f32)),
        name="ssm_prep",
    )(lam_re, lam_im, log_dt.reshape(g, 1), bt_re, bt_im, ct_re, ct_im)


def _modulated_norm(x, gain, shift, scale):
    ms = jnp.mean(x * x, axis=-1, keepdims=True)
    y = x * lax.rsqrt(ms + EPS) * gain
    return y * (1.0 + scale) + shift


def _cmul_add(ar, ai, xr, xi, br, bi):
    return ar * xr - ai * xi + br, ar * xi + ai * xr + bi


def _mixers_kernel(x_ref, mod_ref, g1_ref, wsp_ref, bsp_ref,
                   bbd_ref, cbd_ref, pw_ref, dsk_ref, wglu_ref, bglu_ref, pw_pool_ref, psc_ref,
                   a_ref, b_ref, us_ref, up_ref, e_ref, s_ref, y_ref,
                   *, seq, d, d_ssm, d_pool, n_state, slabs, ts):
    nc = seq // slabs
    nl = d_ssm // LANES
    ns = n_state

    for r0 in range(0, seq, ts):
        rows = pl.ds(r0, ts)
        u = _modulated_norm(x_ref[rows, :], g1_ref[...], mod_ref[:, 0:d], mod_ref[:, d:2 * d])
        r = _dot(u.astype(bf16), wsp_ref[...]) + bsp_ref[...]
        for j in range(nl):
            us_ref[j, rows, :] = r[:, j * LANES:(j + 1) * LANES]
        up_ref[rows, :] = r[:, d_ssm:]

    dsk = dsk_ref[...]
    l1r = pw_ref[0:1, 0:ns]
    l1i = pw_ref[0:1, ns:2 * ns]

    xr = xi = None
    for t in range(slabs):
        ut = jnp.concatenate([us_ref[j, pl.ds(t, nc, stride=slabs), :] for j in range(nl)], axis=1)
        bu = _dot(ut.astype(bf16), bbd_ref[...])
        br, bi = bu[:, 0:ns], bu[:, ns:2 * ns]
        if t == 0:
            xr, xi = br, bi
        else:
            xr, xi = _cmul_add(l1r, l1i, xr, xi, br, bi)
        xc = jnp.concatenate([xr, xi], axis=1).astype(bf16)
        y_ref[t] = _dot_split_rows(xc, cbd_ref[...]) + dsk * ut
    e_ref[:, 0:ns] = xr
    e_ref[:, ns:2 * ns] = xi

    lLr = pw_ref[slabs - 1:slabs, 0:ns]
    lLi = pw_ref[slabs - 1:slabs, ns:2 * ns]

    def chunk_step(c, carry):
        sr, si = carry
        s_ref[pl.ds(c, 1), 0:ns] = sr
        s_ref[pl.ds(c, 1), ns:2 * ns] = si
        er = e_ref[pl.ds(c, 1), 0:ns]
        ei = e_ref[pl.ds(c, 1), ns:2 * ns]
        return _cmul_add(lLr, lLi, sr, si, er, ei)

    zero = jnp.zeros((1, ns), f32)
    lax.fori_loop(0, nc, chunk_step, (zero, zero))

    sr = s_ref[:, 0:ns]
    si = s_ref[:, ns:2 * ns]
    for t in range(slabs):
        pr = pw_ref[t:t + 1, 0:ns]
        pi = pw_ref[t:t + 1, ns:2 * ns]
        zr = pr * sr - pi * si
        zi = pr * si + pi * sr
        zc = jnp.concatenate([zr, zi], axis=1).astype(bf16)
        y = y_ref[t] + _dot_split_rows(zc, cbd_ref[...])
        y = jax.nn.gelu(y)
        z = y * jax.nn.sigmoid(_dot(y.astype(bf16), wglu_ref[...]) + bglu_ref[...])
        for j in range(nl):
            a_ref[j, pl.ds(t, nc, stride=slabs), :] = z[:, j * LANES:(j + 1) * LANES]

    gc = d_pool // len(POOL_WINDOWS)
    row = lax.broadcasted_iota(jnp.int32, (seq, gc), 0)
    for gi, w in enumerate(POOL_WINDOWS):
        lo = gi * gc
        v = up_ref[:, lo:lo + gc]
        acc = v
        span = 1
        while span < w:
            acc = acc + jnp.where(row >= span, pltpu.roll(acc, span, axis=0), 0.0)
            span *= 2
        cnt = jnp.minimum(row + 1, w).astype(f32)
        m = acc / cnt - v
        yg = _dot(m.astype(bf16), pw_pool_ref[gi])
        b_ref[:, lo:lo + gc] = yg * psc_ref[:, lo:lo + gc]


def _mixers_call(x2, mod3, g1, w_sp, b_sp, bbd, cbd, pw, dsk, wglu, bglu, pool_w, pool_scale,
                 bsz, seq, ts):
    t, d = x2.shape
    d_ssm = dsk.shape[1]
    d_pool = pool_scale.shape[1]
    nl = d_ssm // LANES
    ns2 = bbd.shape[1]
    slabs = SSM_SLABS
    nc = seq // slabs
    kern = functools.partial(_mixers_kernel, seq=seq, d=d, d_ssm=d_ssm, d_pool=d_pool,
                             n_state=ns2 // 2, slabs=slabs, ts=ts)
    const2 = lambda b: (0, 0)
    return pl.pallas_call(
        kern,
        grid=(bsz,),
        in_specs=[pl.BlockSpec((seq, d), lambda b: (b, 0)),
                  pl.BlockSpec((None, 1, mod3.shape[2]), lambda b: (b, 0, 0)),
                  pl.BlockSpec((1, d), const2),
                  pl.BlockSpec(w_sp.shape, const2),
                  pl.BlockSpec(b_sp.shape, const2),
                  pl.BlockSpec(bbd.shape, const2),
                  pl.BlockSpec(cbd.shape, const2),
                  pl.BlockSpec(pw.shape, const2),
                  pl.BlockSpec(dsk.shape, const2),
                  pl.BlockSpec(wglu.shape, const2),
                  pl.BlockSpec(bglu.shape, const2),
                  pl.BlockSpec(pool_w.shape, lambda b: (0, 0, 0)),
                  pl.BlockSpec(pool_scale.shape, const2)],
        out_specs=[pl.BlockSpec((nl, seq, LANES), lambda b: (0, b, 0)),
                   pl.BlockSpec((seq, d_pool), lambda b: (b, 0))],
        out_shape=(jax.ShapeDtypeStruct((nl, t, LANES), f32),
                   jax.ShapeDtypeStruct((t, d_pool), f32)),
        scratch_shapes=[pltpu.VMEM((nl, seq, LANES), f32),
                        pltpu.VMEM((seq, d_pool), f32),
                        pltpu.VMEM((nc, ns2), f32),
                        pltpu.VMEM((nc, ns2), f32),
                        pltpu.VMEM((slabs, nc, d_ssm), f32)],
        compiler_params=_cparams(("parallel",)),
        name="mixers",
    )(x2, mod3, g1, w_sp, b_sp, bbd, cbd, pw, dsk, wglu, bglu, pool_w, pool_scale)


def _post_kernel(x_ref, a_ref, b_ref, mod_ref, g1_ref, g2_ref, wg_ref, bg_ref, wps_ref, wpp_ref,
                 wout_ref, wr_ref, br_ref,
                 h_ref, up_ref, metat_ref, cnt_ref, carry_ref,
                 *, d, d_ssm, n_grp, n_exp, tm, ts):
    i = pl.program_id(0)

    @pl.when(i == 0)
    def _():
        carry_ref[...] = jnp.zeros_like(carry_ref)

    for r0 in range(0, tm, ts):
        _post_rows(x_ref, a_ref, b_ref, mod_ref, g1_ref, g2_ref, wg_ref, bg_ref, wps_ref, wpp_ref,
                   wout_ref, wr_ref, br_ref, h_ref, up_ref, metat_ref, carry_ref,
                   d=d, d_ssm=d_ssm, n_grp=n_grp, n_exp=n_exp, r0=r0, tm=ts)
    cnt_ref[...] = carry_ref[...]


def _post_rows(x_ref, a_ref, b_ref, mod_ref, g1_ref, g2_ref, wg_ref, bg_ref, wps_ref, wpp_ref,
               wout_ref, wr_ref, br_ref, h_ref, up_ref, metat_ref, carry_ref,
               *, d, d_ssm, n_grp, n_exp, r0, tm):
    rows = pl.ds(r0, tm)
    x = x_ref[rows, :]
    shift1, scale1, gate1 = mod_ref[:, 0:d], mod_ref[:, d:2 * d], mod_ref[:, 2 * d:3 * d]
    shift2, scale2 = mod_ref[:, 3 * d:4 * d], mod_ref[:, 4 * d:5 * d]
    u = _modulated_norm(x, g1_ref[...], shift1, scale1).astype(bf16)
    gates = _dot(u, wg_ref[...]) + bg_ref[...]
    a = jnp.concatenate([a_ref[j, rows, :] for j in range(d_ssm // LANES)], axis=1).astype(bf16)
    b = b_ref[rows, :].astype(bf16)
    merged = (jax.nn.sigmoid(gates[:, 0:d]) * _dot(a, wps_ref[...])
              + jax.nn.sigmoid(gates[:, d:2 * d]) * _dot(b, wpp_ref[...]))
    h = x + gate1 * _dot(merged.astype(bf16), wout_ref[...])
    h_ref[rows, :] = h

    u2 = _modulated_norm(h, g2_ref[...], shift2, scale2)

    half = d // 2
    hi_bits = lax.bitcast_convert_type(u2[:, 0:half].astype(bf16).astype(f32), jnp.uint32)
    lo_bits = lax.bitcast_convert_type(u2[:, half:d].astype(bf16).astype(f32), jnp.uint32)
    words = hi_bits | (lo_bits >> 16)

    u_hi = u2.astype(bf16)
    u_lo = (u2 - u_hi.astype(f32)).astype(bf16)
    r1 = _dot_split_rows(u_hi, wr_ref[...])
    r2 = _dot_split_rows(u_lo, wr_ref[...])
    lg = r1 + pltpu.roll(r1, ROUTE_LANES // 2, axis=1) + r2 + br_ref[...]

    lane = lax.broadcasted_iota(jnp.int32, (tm, ROUTE_LANES), 1).astype(f32)
    big = float(ROUTE_LANES)
    epg = float(n_exp // n_grp)
    is_c = lane < n_grp
    cl = jnp.where(is_c, lg, NEG)
    cmax = jnp.max(cl, axis=-1, keepdims=True)
    grp = jnp.min(jnp.where(cl == cmax, lane, big), axis=-1, keepdims=True)
    p_grp = 1.0 / jnp.sum(jnp.where(is_c, jnp.exp(cl - cmax), 0.0), axis=-1, keepdims=True)

    f_lo = n_grp + grp * epg
    fl = jnp.where((lane >= f_lo) & (lane < f_lo + epg), lg, NEG)
    f1 = jnp.max(fl, axis=-1, keepdims=True)
    i1 = jnp.min(jnp.where(fl == f1, lane, big), axis=-1, keepdims=True)
    fl2 = jnp.where(lane == i1, NEG, fl)
    f2 = jnp.max(fl2, axis=-1, keepdims=True)
    i2 = jnp.min(jnp.where(fl2 == f2, lane, big), axis=-1, keepdims=True)
    t2 = jnp.exp(f2 - f1)
    w0 = p_grp / (1.0 + t2)
    w1 = p_grp * t2 / (1.0 + t2)

    j0 = i1 - f_lo
    j1 = i2 - f_lo
    first = j0 < j1
    ja = jnp.minimum(j0, j1)
    jb = jnp.maximum(j0, j1)
    wa = jnp.where(first, w0, w1)
    wb = jnp.where(first, w1, w0)
    n_pair = epg * (epg - 1.0) * 0.5
    cls = grp * n_pair + ja * (2.0 * epg - ja - 1.0) * 0.5 + (jb - ja - 1.0)

    sel = lane == cls
    oh = jnp.where(sel, 1.0, 0.0)
    r_i = lax.broadcasted_iota(jnp.int32, (tm, tm), 0)
    c_i = lax.broadcasted_iota(jnp.int32, (tm, tm), 1)
    tri = jnp.where(c_i < r_i, 1.0, 0.0).astype(bf16)
    before = carry_ref[...] + _dot_split_rows(tri, oh.astype(bf16))
    rank = jnp.sum(jnp.where(sel, before, 0.0), axis=-1, keepdims=True)
    carry_ref[...] = carry_ref[...] + jnp.sum(oh, axis=0, keepdims=True)

    meta = jnp.zeros((tm, ROUTE_LANES), f32)
    for k, val in enumerate((cls, rank, wa, wb)):
        meta = jnp.where(lane == float(k), val, meta)

    wts = jnp.where(lane == 0.0, wa, jnp.where(lane == 1.0, wb, 0.0))
    pad = jnp.zeros((tm, (ROW_PANELS - 1) * LANES - half), jnp.uint32)
    row = jnp.concatenate([words, lax.bitcast_convert_type(wts, jnp.uint32), pad], axis=1)
    _store_panels(up_ref, row, tm, r0)
    metat_ref[:, rows] = meta.T[0:8, :]


def _post_call(x2, a3, b2, mod3, g1, g2, wg, bg, wps, wpp, wout, wr, br, seq, tm, ts,
               n_grp, n_exp):
    t, d = x2.shape
    d_ssm = wps.shape[0]
    nl = a3.shape[0]
    per = seq // tm
    kern = functools.partial(_post_kernel, d=d, d_ssm=d_ssm, n_grp=n_grp, n_exp=n_exp,
                             tm=tm, ts=ts)
    const2 = lambda i: (0, 0)
    wspec = lambda w: pl.BlockSpec(w.shape, const2, pipeline_mode=pl.Buffered(1))
    return pl.pallas_call(
        kern,
        grid=(t // tm,),
        in_specs=[pl.BlockSpec((tm, d), lambda i: (i, 0)),
                  pl.BlockSpec((nl, tm, LANES), lambda i: (0, i, 0)),
                  pl.BlockSpec((tm, b2.shape[1]), lambda i: (i, 0)),
                  pl.BlockSpec((None, 1, mod3.shape[2]), lambda i: (i // per, 0, 0)),
                  pl.BlockSpec((1, d), const2),
                  pl.BlockSpec((1, d), const2),
                  wspec(wg),
                  pl.BlockSpec(bg.shape, const2),
                  wspec(wps),
                  wspec(wpp),
                  wspec(wout),
                  wspec(wr),
                  pl.BlockSpec(br.shape, const2)],
        out_specs=[pl.BlockSpec((tm, d), lambda i: (i, 0)),
                   pl.BlockSpec((tm * ROW_PANELS, LANES), lambda i: (i, 0)),
                   pl.BlockSpec((8, tm), lambda i: (0, i)),
                   pl.BlockSpec((1, ROUTE_LANES), const2)],
        out_shape=(jax.ShapeDtypeStruct((t, d), f32),
                   jax.ShapeDtypeStruct((t * ROW_PANELS, LANES), jnp.uint32),
                   jax.ShapeDtypeStruct((8, t), f32),
                   jax.ShapeDtypeStruct((1, ROUTE_LANES), f32)),
        scratch_shapes=[pltpu.VMEM((1, ROUTE_LANES), f32)],
        compiler_params=_cparams(("arbitrary",)),
        name="post",
    )(x2, a3, b2, mod3, g1, g2, wg, bg, wps, wpp, wout, wr, br)


def _row_copy(src_ref, dst_ref, src_row, dst_row, p, sem):
    src = src_ref.at[pl.ds(pl.multiple_of(src_row * p, p), p)]
    dst = dst_ref.at[pl.ds(pl.multiple_of(dst_row * p, p), p)]
    return pltpu.make_async_copy(src, dst, sem)


def _wait_rows(src_ref, dst_ref, n_rows, p, sem):
    pltpu.make_async_copy(src_ref.at[pl.ds(0, n_rows * p)], dst_ref.at[pl.ds(0, n_rows * p)],
                          sem).wait()


def _scatter_kernel(pos_ref, up_ref, xs_ref, sem, *, tm, p):
    def issue(g, c):
        r0 = g * ROW_DMA_UNROLL
        idx = [pos_ref[0, 0, r0 + u] for u in range(ROW_DMA_UNROLL)]
        for u in range(ROW_DMA_UNROLL):
            _row_copy(up_ref, xs_ref, r0 + u, idx[u], p, sem).start(priority=u % 2)
        return c

    lax.fori_loop(0, tm // ROW_DMA_UNROLL, issue, 0)
    _wait_rows(up_ref, xs_ref, tm, p, sem)


def _scatter_call(up, pos3, n_rows, tm, p):
    n_steps = up.shape[0] // (tm * p)
    return pl.pallas_call(
        functools.partial(_scatter_kernel, tm=tm, p=p),
        grid=(n_steps,),
        in_specs=[pl.BlockSpec((1, 1, tm), lambda i: (i, 0, 0), memory_space=pltpu.SMEM),
                  pl.BlockSpec((tm * p, LANES), lambda i: (i, 0))],
        out_specs=pl.BlockSpec(memory_space=pl.ANY),
        out_shape=jax.ShapeDtypeStruct((n_rows * p, LANES), up.dtype),
        scratch_shapes=[pltpu.SemaphoreType.DMA(())],
        compiler_params=_cparams(("arbitrary",)),
        name="scatter_rows",
    )(pos3, up)


def _experts_kernel(ea_ref, eb_ref, nt_ref, xs_ref, wg_a_ref, wg_b_ref, wu_a_ref, wu_b_ref,
                    wd_a_ref, wd_b_ref, ys_ref, *, d, tr):
    i = pl.program_id(0)

    @pl.when(i < nt_ref[0])
    def _():
        px = d // 2 // LANES
        w = jnp.concatenate([xs_ref[pl.ds(j, tr, stride=ROW_PANELS), :] for j in range(px)],
                            axis=1)
        wts = lax.bitcast_convert_type(xs_ref[pl.ds(px, tr, stride=ROW_PANELS), :], f32)
        hi = lax.bitcast_convert_type(w & jnp.uint32(0xFFFF0000), f32)
        lo = lax.bitcast_convert_type(w << 16, f32)
        x = jnp.concatenate([hi, lo], axis=1).astype(bf16)

        def ffn(wg_ref, wu_ref, wd_ref, wt):
            h1 = _dot(x, wg_ref[...])
            act = h1 * jax.nn.sigmoid(h1) * _dot(x, wu_ref[...])
            return wt * _dot(act.astype(bf16), wd_ref[...])

        y = (ffn(wg_a_ref, wu_a_ref, wd_a_ref, wts[:, 0:1])
             + ffn(wg_b_ref, wu_b_ref, wd_b_ref, wts[:, 1:2]))
        _store_panels(ys_ref, y, tr)


def _experts_call(tile_ea, tile_eb, n_tiles, xs, wg, wu, wd, tr):
    n_exp, d, d_exp = wg.shape
    assert d // LANES == ROW_PANELS
    n_rows = xs.shape[0] // ROW_PANELS
    nt_max = n_rows // tr

    def row_map(i, ea, eb, nt):
        return (jnp.minimum(i, nt[0] - 1), 0)

    def wa_map(i, ea, eb, nt):
        return (ea[jnp.minimum(i, nt[0] - 1)], 0, 0)

    def wb_map(i, ea, eb, nt):
        return (eb[jnp.minimum(i, nt[0] - 1)], 0, 0)

    wspec = pl.BlockSpec
    grid_spec = pltpu.PrefetchScalarGridSpec(
        num_scalar_prefetch=3,
        grid=(nt_max,),
        in_specs=[pl.BlockSpec((tr * ROW_PANELS, LANES), row_map),
                  wspec((None, d, d_exp), wa_map),
                  wspec((None, d, d_exp), wb_map),
                  wspec((None, d, d_exp), wa_map),
                  wspec((None, d, d_exp), wb_map),
                  wspec((None, d_exp, d), wa_map),
                  wspec((None, d_exp, d), wb_map)],
        out_specs=pl.BlockSpec((tr * ROW_PANELS, LANES), row_map),
    )
    return pl.pallas_call(
        functools.partial(_experts_kernel, d=d, tr=tr),
        grid_spec=grid_spec,
        out_shape=jax.ShapeDtypeStruct((n_rows * ROW_PANELS, LANES), f32),
        compiler_params=_cparams(("arbitrary",)),
        name="experts",
    )(tile_ea, tile_eb, n_tiles, xs, wg, wg, wu, wu, wd, wd)


def _combine_kernel(pos_ref, posn_ref, h_ref, mod_ref, gf_ref, ys_ref, o_ref,
                    g_ref, sem, *, d, tm, n_steps, final_norm):
    i = pl.program_id(0)
    slot = i % 2
    py = d // LANES

    def issue(p_ref, s):
        def body(g, c):
            r0 = g * ROW_DMA_UNROLL
            idx = [p_ref[0, 0, r0 + u] for u in range(ROW_DMA_UNROLL)]
            for u in range(ROW_DMA_UNROLL):
                _row_copy(ys_ref, g_ref.at[s], idx[u], r0 + u, py,
                          sem.at[s]).start(priority=u % 2)
            return c

        lax.fori_loop(0, tm // ROW_DMA_UNROLL, body, 0)

    pl.when(i == 0)(lambda: issue(pos_ref, 0))
    pl.when(i + 1 < n_steps)(lambda: issue(posn_ref, 1 - slot))
    _wait_rows(ys_ref, g_ref.at[slot], tm, py, sem.at[slot])

    gate2 = mod_ref[:, 5 * d:6 * d]
    h = h_ref[...] + gate2 * _load_panels(g_ref.at[slot], tm, py)
    if final_norm:
        ms = jnp.mean(h * h, axis=-1, keepdims=True)
        h = h * lax.rsqrt(ms + EPS) * gf_ref[...]
    o_ref[...] = h


def _combine_call(pos3, h1, mod3, gf, ys, seq, tm, final_norm):
    t, d = h1.shape
    per = seq // tm
    n_steps = t // tm
    pos_spec = lambda f: pl.BlockSpec((1, 1, tm), f, memory_space=pltpu.SMEM)
    return pl.pallas_call(
        functools.partial(_combine_kernel, d=d, tm=tm, n_steps=n_steps, final_norm=final_norm),
        grid=(n_steps,),
        in_specs=[pos_spec(lambda i: (i, 0, 0)),
                  pos_spec(lambda i: (jnp.minimum(i + 1, n_steps - 1), 0, 0)),
                  pl.BlockSpec((tm, d), lambda i: (i, 0)),
                  pl.BlockSpec((None, 1, mod3.shape[2]), lambda i: (i // per, 0, 0)),
                  pl.BlockSpec((1, d), lambda i: (0, 0)),
                  pl.BlockSpec(memory_space=pl.ANY)],
        out_specs=pl.BlockSpec((tm, d), lambda i: (i, 0)),
        out_shape=jax.ShapeDtypeStruct((t, d), f32),
        scratch_shapes=[pltpu.VMEM((2, tm * (d // LANES), LANES), f32),
                        pltpu.SemaphoreType.DMA((2,))],
        compiler_params=_cparams(("arbitrary",)),
        name="combine",
    )(pos3, pos3, h1, mod3, gf, ys)


def _row_tile(seq, target):
    tm = min(target, seq)
    assert seq % tm == 0 and tm % 8 == 0
    return tm


def _layer(h2, mod3, p, bsz, seq, final_gain, final_norm):
    t, d = h2.shape
    g, n_p, n_h = p["ssm_b_re"].shape
    d_ssm = g * n_h
    d_pool = p["pool_scale"].shape[-1]
    n_grp = p["router_coarse_w"].shape[-1]
    n_exp = p["router_fine_w"].shape[-1]
    assert n_grp + n_exp <= ROUTE_LANES // 2 and seq % SSM_SLABS == 0
    tm = _row_tile(seq, 512)

    swap = lambda v: jnp.swapaxes(v, 1, 2)
    pw, bbd, cbd = _ssm_prep_call(
        p["ssm_lam_re"], p["ssm_lam_im"], p["ssm_log_dt"], swap(p["ssm_b_re"]),
        swap(p["ssm_b_im"]), swap(p["ssm_c_re"]), swap(p["ssm_c_im"]), SSM_SLABS)
    bbd, cbd = bbd.astype(bf16), cbd.astype(bf16)
    pw2 = pw.reshape(SSM_SLABS, 2 * g * n_p)

    w_in, b_in = p["w_in"], p["b_in"]
    n_sp = d_ssm + d_pool
    a3, b2 = _mixers_call(h2, mod3, p["norm1_g"].reshape(1, d), w_in[:, :n_sp].astype(bf16),
                          b_in[:n_sp].reshape(1, n_sp), bbd, cbd, pw2,
                          p["ssm_d"].reshape(1, d_ssm),
                          p["ssm_w_glu"].astype(bf16), p["ssm_b_glu"].reshape(1, d_ssm),
                          p["pool_w"].astype(bf16), p["pool_scale"].reshape(1, d_pool),
                          bsz, seq, tm)

    wr = jnp.concatenate([p["router_coarse_w"], p["router_fine_w"]], axis=1)
    wr_hi = wr.astype(bf16)
    wr_lo = (wr - wr_hi.astype(f32)).astype(bf16)
    padc = ROUTE_LANES // 2 - wr.shape[1]
    wr_cat = jnp.concatenate([jnp.pad(wr_hi, ((0, 0), (0, padc))),
                              jnp.pad(wr_lo, ((0, 0), (0, padc)))], axis=1)
    br = jnp.pad(jnp.concatenate([p["router_coarse_b"], p["router_fine_b"]]),
                 (0, ROUTE_LANES - wr.shape[1])).reshape(1, ROUTE_LANES)

    h1, up, metat, cnt = _post_call(
        h2, a3, b2, mod3, p["norm1_g"].reshape(1, d), p["norm2_g"].reshape(1, d),
        w_in[:, n_sp:].astype(bf16), b_in[n_sp:].reshape(1, 2 * d),
        p["w_proj_ssm"].astype(bf16), p["w_proj_pool"].astype(bf16), p["w_out"].astype(bf16),
        wr_cat, br, seq, tm, tm, n_grp, n_exp)

    tr = EXPERT_TILE_ROWS
    epg = n_exp // n_grp
    pairs = [(a, b) for a in range(epg) for b in range(a + 1, epg)]
    n_cls = n_grp * len(pairs)
    assert n_cls <= ROUTE_LANES
    cls_ea = jnp.asarray([gi * epg + a for gi in range(n_grp) for a, _ in pairs], jnp.int32)
    cls_eb = jnp.asarray([gi * epg + b for gi in range(n_grp) for _, b in pairs], jnp.int32)
    counts = cnt[0, 0:n_cls].astype(jnp.int32)
    padded = ((counts + tr - 1) // tr) * tr
    ends = jnp.cumsum(padded)
    offs = ends - padded
    ids = jnp.arange(n_cls, dtype=jnp.int32)
    cls = metat[0].astype(jnp.int32)
    off = jnp.sum(jnp.where(cls[None, :] == ids[:, None], offs[:, None], 0), axis=0)
    pos = off + metat[1].astype(jnp.int32)
    pos3 = pos.reshape(t // tm, 1, tm)
    tms = _row_tile(seq, 2 * tm)
    nt_max = t // tr + n_cls
    n_tiles = (ends[-1] // tr).astype(jnp.int32).reshape(1)
    tile_start = jnp.arange(nt_max, dtype=jnp.int32) * tr
    tile_cls = jnp.minimum(
        jnp.sum((ends[None, :] <= tile_start[:, None]).astype(jnp.int32), axis=1), n_cls - 1)

    xs = _scatter_call(up, pos.reshape(t // tms, 1, tms), nt_max * tr, tms, ROW_PANELS)
    ys = _experts_call(cls_ea[tile_cls], cls_eb[tile_cls], n_tiles, xs,
                       p["moe_w_gate"].astype(bf16), p["moe_w_up"].astype(bf16),
                       p["moe_w_down"].astype(bf16), tr)
    return _combine_call(pos3, h1, mod3, final_gain.reshape(1, d), ys, seq, tm, final_norm)


def kernel(x, c, w_mod, b_mod, norm1_g, w_in, b_in, ssm_lam_re, ssm_lam_im, ssm_log_dt, ssm_b_re, ssm_b_im, ssm_c_re, ssm_c_im, ssm_d, ssm_w_glu, ssm_b_glu, pool_w, pool_scale, w_proj_ssm, w_proj_pool, w_out, norm2_g, router_coarse_w, router_coarse_b, router_fine_w, router_fine_b, moe_w_gate, moe_w_up, moe_w_down, norm_f_g):
    bsz, seq, d = x.shape
    depth = w_mod.shape[0]
    per_layer = dict(
        norm1_g=norm1_g, w_in=w_in, b_in=b_in, ssm_lam_re=ssm_lam_re, ssm_lam_im=ssm_lam_im,
        ssm_log_dt=ssm_log_dt, ssm_b_re=ssm_b_re, ssm_b_im=ssm_b_im, ssm_c_re=ssm_c_re,
        ssm_c_im=ssm_c_im, ssm_d=ssm_d, ssm_w_glu=ssm_w_glu, ssm_b_glu=ssm_b_glu, pool_w=pool_w,
        pool_scale=pool_scale, w_proj_ssm=w_proj_ssm, w_proj_pool=w_proj_pool, w_out=w_out,
        norm2_g=norm2_g, router_coarse_w=router_coarse_w, router_coarse_b=router_coarse_b,
        router_fine_w=router_fine_w, router_fine_b=router_fine_b, moe_w_gate=moe_w_gate,
        moe_w_up=moe_w_up, moe_w_down=moe_w_down)
    h2 = x.reshape(bsz * seq, d)
    for l in range(depth):
        p = {k: v[l] for k, v in per_layer.items()}
        mod3 = _mod_call(c, w_mod[l], b_mod[l]).reshape(bsz, 1, N_MOD * d)
        h2 = _layer(h2, mod3, p, bsz, seq, norm_f_g, final_norm=(l == depth - 1))
    return h2.reshape(bsz, seq, d)
```

```python
import functools
import math

import jax
import jax.numpy as jnp
from jax import lax
from jax.experimental import pallas as pl
from jax.experimental.pallas import tpu as pltpu

EPS = 1e-6
POOL_WINDOWS = (2, 4, 8, 16)
TOP_K_FINE = 2
N_MOD = 6
LANES = 128
SSM_SLABS = 8
ROUTE_LANES = 128
ROW_DMA_UNROLL = 16
ROW_PANELS = 8
EXPERT_TILE_ROWS = 256
VMEM_LIMIT = 56 * 1024 * 1024
NEG = -1e30

f32 = jnp.float32
bf16 = jnp.bfloat16


def _cparams(sem):
    return pltpu.CompilerParams(dimension_semantics=sem, vmem_limit_bytes=VMEM_LIMIT)


def _dot(a, b):
    return jnp.dot(a, b, preferred_element_type=f32)


def _sigmoid(v):
    return 0.5 * jnp.tanh(0.5 * v) + 0.5


def _dot_split_rows(a, b):
    h = a.shape[0] // 2
    return jnp.concatenate([_dot(a[:h], b), _dot(a[h:], b)], axis=0)


def _store_panels(ref, val, rows, row0=0):
    p = val.shape[1] // LANES
    for j in range(p):
        ref[pl.ds(row0 * p + j, rows, stride=p), :] = val[:, j * LANES:(j + 1) * LANES]


def _load_panels(ref, rows, p):
    return jnp.concatenate([ref[pl.ds(j, rows, stride=p), :] for j in range(p)], axis=1)


def _split_bf16(v):
    hi = v.astype(bf16)
    return hi, (v - hi.astype(f32)).astype(bf16)


def _mod_kernel(c_ref, w_ref, b_ref, o_ref):
    c = c_ref[...]
    a_hi, a_lo = _split_bf16(c * _sigmoid(c))
    w_hi, w_lo = _split_bf16(w_ref[...])
    o_ref[...] = _dot(a_hi, w_hi) + (_dot(a_lo, w_hi) + _dot(a_hi, w_lo)) + b_ref[...]


def _mod_call(c, w_mod, b_mod):
    bsz, d = c.shape
    n = w_mod.shape[1]
    tn = d
    return pl.pallas_call(
        _mod_kernel,
        grid=(n // tn,),
        in_specs=[pl.BlockSpec((bsz, d), lambda j: (0, 0)),
                  pl.BlockSpec((d, tn), lambda j: (0, j)),
                  pl.BlockSpec((1, tn), lambda j: (0, j))],
        out_specs=pl.BlockSpec((bsz, tn), lambda j: (0, j)),
        out_shape=jax.ShapeDtypeStruct((bsz, n), f32),
        compiler_params=_cparams(("parallel",)),
        name="mod",
    )(c, w_mod, b_mod.reshape(1, n))


def _ssm_prep_kernel(lr_ref, li_ref, ldt_ref, btr_ref, bti_ref, ctr_ref, cti_ref,
                     pw_ref, bbd_ref, cbd_ref, *, n_pow, n_grp):
    n_h, n_p = btr_ref.shape[1], btr_ref.shape[2]
    gp = n_grp * n_p
    lr = lr_ref[...]
    li = li_ref[...]
    dt = jnp.exp(ldt_ref[...])
    for k in range(1, n_pow + 1):
        mag = jnp.exp(lr * dt * float(k))
        ang = li * dt * float(k)
        pw_ref[k - 1, 0] = mag * jnp.cos(ang)
        pw_ref[k - 1, 1] = mag * jnp.sin(ang)
    lb_re = pw_ref[0, 0]
    lb_im = pw_ref[0, 1]
    den = lr * lr + li * li
    nr = lb_re - 1.0
    f_re = (nr * lr + lb_im * li) / den
    f_im = (lb_im * lr - nr * li) / den
    bbd_ref[...] = jnp.zeros_like(bbd_ref)
    cbd_ref[...] = jnp.zeros_like(cbd_ref)
    for g in range(n_grp):
        fr, fi = f_re[g:g + 1, :], f_im[g:g + 1, :]
        hs, ps = pl.ds(g * n_h, n_h), pl.ds(g * n_p, n_p)
        bbd_ref[hs, ps] = fr * btr_ref[g] - fi * bti_ref[g]
        bbd_ref[hs, pl.ds(gp + g * n_p, n_p)] = fr * bti_ref[g] + fi * btr_ref[g]
        cbd_ref[ps, hs] = ctr_ref[g]
        cbd_ref[pl.ds(gp + g * n_p, n_p), hs] = -cti_ref[g]


def _ssm_prep_call(lam_re, lam_im, log_dt, bt_re, bt_im, ct_re, ct_im, n_pow):
    g, p = lam_re.shape
    h = bt_re.shape[1]
    return pl.pallas_call(
        functools.partial(_ssm_prep_kernel, n_pow=n_pow, n_grp=g),
        out_shape=(jax.ShapeDtypeStruct((n_pow, 2, g, p), f32),
                   jax.ShapeDtypeStruct((g * h, 2 * g * p), f32),
                   jax.ShapeDtypeStruct((2 * g * p, g * h), f32)),
        name="ssm_prep",
    )(lam_re, lam_im, log_dt.reshape(g, 1), bt_re, bt_im, ct_re, ct_im)


def _modulated_norm(x, gain, shift, scale):
    ms = jnp.mean(x * x, axis=-1, keepdims=True)
    y = x * lax.rsqrt(ms + EPS) * gain
    return y * (1.0 + scale) + shift


def _cmul_add(ar, ai, xr, xi, br, bi):
    return ar * xr - ai * xi + br, ar * xi + ai * xr + bi


def _mixers_kernel(x_ref, mod_ref, g1_ref, wsp_ref, bsp_ref,
                   bbd_ref, cbd_ref, pw_ref, dsk_ref, wglu_ref, bglu_ref, pw_pool_ref, psc_ref,
                   a_ref, b_ref, u_ref, us_ref, up_ref, e_ref, s_ref, y_ref,
                   *, seq, d, d_ssm, d_pool, n_state, slabs, ts):
    nc = seq // slabs
    nl = d_ssm // LANES
    ns = n_state

    for r0 in range(0, seq, ts):
        rows = pl.ds(r0, ts)
        u = _modulated_norm(x_ref[rows, :], g1_ref[...], mod_ref[:, 0:d], mod_ref[:, d:2 * d])
        u = u.astype(bf16)
        u_ref[rows, :] = u
        r = _dot(u, wsp_ref[...]) + bsp_ref[...]
        for j in range(nl):
            us_ref[j, rows, :] = r[:, j * LANES:(j + 1) * LANES]
        up_ref[rows, :] = r[:, d_ssm:]

    dsk = dsk_ref[...]
    l1r = pw_ref[0:1, 0:ns]
    l1i = pw_ref[0:1, ns:2 * ns]

    xr = xi = None
    for t in range(slabs):
        ut = jnp.concatenate([us_ref[j, pl.ds(t, nc, stride=slabs), :] for j in range(nl)], axis=1)
        bu = _dot(ut.astype(bf16), bbd_ref[...])
        br, bi = bu[:, 0:ns], bu[:, ns:2 * ns]
        if t == 0:
            xr, xi = br, bi
        else:
            xr, xi = _cmul_add(l1r, l1i, xr, xi, br, bi)
        xc = jnp.concatenate([xr, xi], axis=1).astype(bf16)
        y_ref[t] = _dot_split_rows(xc, cbd_ref[...]) + dsk * ut
    e_ref[:, 0:ns] = xr
    e_ref[:, ns:2 * ns] = xi

    lLr = pw_ref[slabs - 1:slabs, 0:ns]
    lLi = pw_ref[slabs - 1:slabs, ns:2 * ns]

    def chunk_step(c, carry):
        sr, si = carry
        s_ref[pl.ds(c, 1), 0:ns] = sr
        s_ref[pl.ds(c, 1), ns:2 * ns] = si
        er = e_ref[pl.ds(c, 1), 0:ns]
        ei = e_ref[pl.ds(c, 1), ns:2 * ns]
        return _cmul_add(lLr, lLi, sr, si, er, ei)

    zero = jnp.zeros((1, ns), f32)
    lax.fori_loop(0, nc, chunk_step, (zero, zero))

    sr = s_ref[:, 0:ns]
    si = s_ref[:, ns:2 * ns]
    for t in range(slabs):
        pr = pw_ref[t:t + 1, 0:ns]
        pi = pw_ref[t:t + 1, ns:2 * ns]
        zr = pr * sr - pi * si
        zi = pr * si + pi * sr
        zc = jnp.concatenate([zr, zi], axis=1).astype(bf16)
        y = y_ref[t] + _dot_split_rows(zc, cbd_ref[...])
        y = jax.nn.gelu(y)
        z = y * _sigmoid(_dot(y.astype(bf16), wglu_ref[...]) + bglu_ref[...])
        for j in range(nl):
            a_ref[j, pl.ds(t, nc, stride=slabs), :] = z[:, j * LANES:(j + 1) * LANES]

    gc = d_pool // len(POOL_WINDOWS)
    row = lax.broadcasted_iota(jnp.int32, (seq, gc), 0)
    for gi, w in enumerate(POOL_WINDOWS):
        lo = gi * gc
        v = up_ref[:, lo:lo + gc]
        acc = v
        span = 1
        while span < w:
            acc = acc + jnp.where(row >= span, pltpu.roll(acc, span, axis=0), 0.0)
            span *= 2
        cnt = jnp.minimum(row + 1, w).astype(f32)
        m = acc / cnt - v
        yg = _dot(m.astype(bf16), pw_pool_ref[gi])
        b_ref[:, lo:lo + gc] = (yg * psc_ref[:, lo:lo + gc]).astype(bf16)


def _mixers_call(x2, mod3, g1, w_sp, b_sp, bbd, cbd, pw, dsk, wglu, bglu, pool_w, pool_scale,
                 bsz, seq, ts):
    t, d = x2.shape
    d_ssm = dsk.shape[1]
    d_pool = pool_scale.shape[1]
    nl = d_ssm // LANES
    ns2 = bbd.shape[1]
    slabs = SSM_SLABS
    nc = seq // slabs
    kern = functools.partial(_mixers_kernel, seq=seq, d=d, d_ssm=d_ssm, d_pool=d_pool,
                             n_state=ns2 // 2, slabs=slabs, ts=ts)
    const2 = lambda b: (0, 0)
    return pl.pallas_call(
        kern,
        grid=(bsz,),
        in_specs=[pl.BlockSpec((seq, d), lambda b: (b, 0)),
                  pl.BlockSpec((None, 1, mod3.shape[2]), lambda b: (b, 0, 0)),
                  pl.BlockSpec((1, d), const2),
                  pl.BlockSpec(w_sp.shape, const2),
                  pl.BlockSpec(b_sp.shape, const2),
                  pl.BlockSpec(bbd.shape, const2),
                  pl.BlockSpec(cbd.shape, const2),
                  pl.BlockSpec(pw.shape, const2),
                  pl.BlockSpec(dsk.shape, const2),
                  pl.BlockSpec(wglu.shape, const2),
                  pl.BlockSpec(bglu.shape, const2),
                  pl.BlockSpec(pool_w.shape, lambda b: (0, 0, 0)),
                  pl.BlockSpec(pool_scale.shape, const2)],
        out_specs=[pl.BlockSpec((nl, seq, LANES), lambda b: (0, b, 0)),
                   pl.BlockSpec((seq, d_pool), lambda b: (b, 0)),
                   pl.BlockSpec((seq, d), lambda b: (b, 0))],
        out_shape=(jax.ShapeDtypeStruct((nl, t, LANES), f32),
                   jax.ShapeDtypeStruct((t, d_pool), bf16),
                   jax.ShapeDtypeStruct((t, d), bf16)),
        scratch_shapes=[pltpu.VMEM((nl, seq, LANES), f32),
                        pltpu.VMEM((seq, d_pool), f32),
                        pltpu.VMEM((nc, ns2), f32),
                        pltpu.VMEM((nc, ns2), f32),
                        pltpu.VMEM((slabs, nc, d_ssm), f32)],
        compiler_params=_cparams(("parallel",)),
        name="mixers",
    )(x2, mod3, g1, w_sp, b_sp, bbd, cbd, pw, dsk, wglu, bglu, pool_w, pool_scale)


def _post_kernel(x_ref, a_ref, b_ref, mod_ref, u_ref, g2_ref, wg_ref, bg_ref, wps_ref, wpp_ref,
                 wout_ref, wr_ref, br_ref,
                 h_ref, up_ref, metat_ref, cnt_ref, carry_ref,
                 *, d, d_ssm, n_grp, n_exp, tm, ts):
    i = pl.program_id(0)

    @pl.when(i == 0)
    def _():
        carry_ref[...] = jnp.zeros_like(carry_ref)

    for r0 in range(0, tm, ts):
        _post_rows(x_ref, a_ref, b_ref, mod_ref, u_ref, g2_ref, wg_ref, bg_ref, wps_ref, wpp_ref,
                   wout_ref, wr_ref, br_ref, h_ref, up_ref, metat_ref, carry_ref,
                   d=d, d_ssm=d_ssm, n_grp=n_grp, n_exp=n_exp, r0=r0, tm=ts)
    cnt_ref[...] = carry_ref[...]


def _post_rows(x_ref, a_ref, b_ref, mod_ref, u_ref, g2_ref, wg_ref, bg_ref, wps_ref, wpp_ref,
               wout_ref, wr_ref, br_ref, h_ref, up_ref, metat_ref, carry_ref,
               *, d, d_ssm, n_grp, n_exp, r0, tm):
    rows = pl.ds(r0, tm)
    x = x_ref[rows, :]
    gate1 = mod_ref[:, 2 * d:3 * d]
    shift2, scale2 = mod_ref[:, 3 * d:4 * d], mod_ref[:, 4 * d:5 * d]
    gates = _dot(u_ref[rows, :], wg_ref[...]) + bg_ref[...]
    a = jnp.concatenate([a_ref[j, rows, :] for j in range(d_ssm // LANES)], axis=1).astype(bf16)
    b = b_ref[rows, :]
    merged = (_sigmoid(gates[:, 0:d]) * _dot(a, wps_ref[...])
              + _sigmoid(gates[:, d:2 * d]) * _dot(b, wpp_ref[...]))
    h = x + gate1 * _dot(merged.astype(bf16), wout_ref[...])
    h_ref[rows, :] = h

    u2 = _modulated_norm(h, g2_ref[...], shift2, scale2)

    half = d // 2
    hi_bits = lax.bitcast_convert_type(u2[:, 0:half].astype(bf16).astype(f32), jnp.uint32)
    lo_bits = lax.bitcast_convert_type(u2[:, half:d].astype(bf16).astype(f32), jnp.uint32)
    words = hi_bits | (lo_bits >> 16)

    u_hi = u2.astype(bf16)
    u_lo = (u2 - u_hi.astype(f32)).astype(bf16)
    r1 = _dot_split_rows(u_hi, wr_ref[...])
    r2 = _dot_split_rows(u_lo, wr_ref[...])
    lg = r1 + pltpu.roll(r1, ROUTE_LANES // 2, axis=1) + r2 + br_ref[...]

    lane = lax.broadcasted_iota(jnp.int32, (tm, ROUTE_LANES), 1).astype(f32)
    big = float(ROUTE_LANES)
    epg = float(n_exp // n_grp)
    is_c = lane < n_grp
    cl = jnp.where(is_c, lg, NEG)
    cmax = jnp.max(cl, axis=-1, keepdims=True)
    grp = jnp.min(jnp.where(cl == cmax, lane, big), axis=-1, keepdims=True)
    p_grp = 1.0 / jnp.sum(jnp.where(is_c, jnp.exp(cl - cmax), 0.0), axis=-1, keepdims=True)

    f_lo = n_grp + grp * epg
    fl = jnp.where((lane >= f_lo) & (lane < f_lo + epg), lg, NEG)
    f1 = jnp.max(fl, axis=-1, keepdims=True)
    i1 = jnp.min(jnp.where(fl == f1, lane, big), axis=-1, keepdims=True)
    fl2 = jnp.where(lane == i1, NEG, fl)
    f2 = jnp.max(fl2, axis=-1, keepdims=True)
    i2 = jnp.min(jnp.where(fl2 == f2, lane, big), axis=-1, keepdims=True)
    t2 = jnp.exp(f2 - f1)
    w0 = p_grp / (1.0 + t2)
    w1 = p_grp * t2 / (1.0 + t2)

    j0 = i1 - f_lo
    j1 = i2 - f_lo
    first = j0 < j1
    ja = jnp.minimum(j0, j1)
    jb = jnp.maximum(j0, j1)
    wa = jnp.where(first, w0, w1)
    wb = jnp.where(first, w1, w0)
    n_pair = epg * (epg - 1.0) * 0.5
    cls = grp * n_pair + ja * (2.0 * epg - ja - 1.0) * 0.5 + (jb - ja - 1.0)

    sel = lane == cls
    oh = jnp.where(sel, 1.0, 0.0)
    r_i = lax.broadcasted_iota(jnp.int32, (tm, tm), 0)
    c_i = lax.broadcasted_iota(jnp.int32, (tm, tm), 1)
    tri = jnp.where(c_i < r_i, 1.0, 0.0).astype(bf16)
    before = carry_ref[...] + _dot_split_rows(tri, oh.astype(bf16))
    rank = jnp.sum(jnp.where(sel, before, 0.0), axis=-1, keepdims=True)
    carry_ref[...] = carry_ref[...] + jnp.sum(oh, axis=0, keepdims=True)

    meta = jnp.zeros((tm, ROUTE_LANES), f32)
    for k, val in enumerate((cls, rank, wa, wb)):
        meta = jnp.where(lane == float(k), val, meta)

    wts = jnp.where(lane == 0.0, wa, jnp.where(lane == 1.0, wb, 0.0))
    pad = jnp.zeros((tm, (ROW_PANELS - 1) * LANES - half), jnp.uint32)
    row = jnp.concatenate([words, lax.bitcast_convert_type(wts, jnp.uint32), pad], axis=1)
    _store_panels(up_ref, row, tm, r0)
    metat_ref[:, rows] = meta.T[0:8, :]


def _post_call(x2, a3, b2, mod3, u2d, g2, wg, bg, wps, wpp, wout, wr, br, seq, tm, ts,
               n_grp, n_exp):
    t, d = x2.shape
    d_ssm = wps.shape[0]
    nl = a3.shape[0]
    per = seq // tm
    kern = functools.partial(_post_kernel, d=d, d_ssm=d_ssm, n_grp=n_grp, n_exp=n_exp,
                             tm=tm, ts=ts)
    const2 = lambda i: (0, 0)
    wspec = lambda w: pl.BlockSpec(w.shape, const2, pipeline_mode=pl.Buffered(1))
    return pl.pallas_call(
        kern,
        grid=(t // tm,),
        in_specs=[pl.BlockSpec((tm, d), lambda i: (i, 0)),
                  pl.BlockSpec((nl, tm, LANES), lambda i: (0, i, 0)),
                  pl.BlockSpec((tm, b2.shape[1]), lambda i: (i, 0)),
                  pl.BlockSpec((None, 1, mod3.shape[2]), lambda i: (i // per, 0, 0)),
                  pl.BlockSpec((tm, d), lambda i: (i, 0)),
                  pl.BlockSpec((1, d), const2),
                  wspec(wg),
                  pl.BlockSpec(bg.shape, const2),
                  wspec(wps),
                  wspec(wpp),
                  wspec(wout),
                  wspec(wr),
                  pl.BlockSpec(br.shape, const2)],
        out_specs=[pl.BlockSpec((tm, d), lambda i: (i, 0)),
                   pl.BlockSpec((tm * ROW_PANELS, LANES), lambda i: (i, 0)),
                   pl.BlockSpec((8, tm), lambda i: (0, i)),
                   pl.BlockSpec((1, ROUTE_LANES), const2)],
        out_shape=(jax.ShapeDtypeStruct((t, d), f32),
                   jax.ShapeDtypeStruct((t * ROW_PANELS, LANES), jnp.uint32),
                   jax.ShapeDtypeStruct((8, t), f32),
                   jax.ShapeDtypeStruct((1, ROUTE_LANES), f32)),
        scratch_shapes=[pltpu.VMEM((1, ROUTE_LANES), f32)],
        compiler_params=_cparams(("arbitrary",)),
        name="post",
    )(x2, a3, b2, mod3, u2d, g2, wg, bg, wps, wpp, wout, wr, br)


def _row_copy(src_ref, dst_ref, src_row, dst_row, p, sem):
    src = src_ref.at[pl.ds(pl.multiple_of(src_row * p, p), p)]
    dst = dst_ref.at[pl.ds(pl.multiple_of(dst_row * p, p), p)]
    return pltpu.make_async_copy(src, dst, sem)


def _wait_rows(src_ref, dst_ref, n_rows, p, sem):
    pltpu.make_async_copy(src_ref.at[pl.ds(0, n_rows * p)], dst_ref.at[pl.ds(0, n_rows * p)],
                          sem).wait()


def _scatter_kernel(pos_ref, up_ref, xs_ref, sem, *, tm, p):
    def issue(g, c):
        r0 = g * ROW_DMA_UNROLL
        idx = [pos_ref[0, 0, r0 + u] for u in range(ROW_DMA_UNROLL)]
        for u in range(ROW_DMA_UNROLL):
            _row_copy(up_ref, xs_ref, r0 + u, idx[u], p, sem).start(priority=u % 2)
        return c

    lax.fori_loop(0, tm // ROW_DMA_UNROLL, issue, 0)
    _wait_rows(up_ref, xs_ref, tm, p, sem)


def _scatter_call(up, pos3, n_rows, tm, p):
    n_steps = up.shape[0] // (tm * p)
    return pl.pallas_call(
        functools.partial(_scatter_kernel, tm=tm, p=p),
        grid=(n_steps,),
        in_specs=[pl.BlockSpec((1, 1, tm), lambda i: (i, 0, 0), memory_space=pltpu.SMEM),
                  pl.BlockSpec((tm * p, LANES), lambda i: (i, 0))],
        out_specs=pl.BlockSpec(memory_space=pl.ANY),
        out_shape=jax.ShapeDtypeStruct((n_rows * p, LANES), up.dtype),
        scratch_shapes=[pltpu.SemaphoreType.DMA(())],
        compiler_params=_cparams(("arbitrary",)),
        name="scatter_rows",
    )(pos3, up)


def _experts_kernel(ea_ref, eb_ref, nt_ref, xs_ref, wg_a_ref, wg_b_ref, wu_a_ref, wu_b_ref,
                    wd_a_ref, wd_b_ref, ys_ref, *, d, tr):
    i = pl.program_id(0)

    @pl.when(i < nt_ref[0])
    def _():
        px = d // 2 // LANES
        w = jnp.concatenate([xs_ref[pl.ds(j, tr, stride=ROW_PANELS), :] for j in range(px)],
                            axis=1)
        wts = lax.bitcast_convert_type(xs_ref[pl.ds(px, tr, stride=ROW_PANELS), :], f32)
        hi = lax.bitcast_convert_type(w & jnp.uint32(0xFFFF0000), f32)
        lo = lax.bitcast_convert_type(w << 16, f32)
        x = jnp.concatenate([hi, lo], axis=1).astype(bf16)

        def ffn(wg_ref, wu_ref, wd_ref, wt):
            h1 = _dot(x, wg_ref[...])
            act = h1 * _sigmoid(h1) * _dot(x, wu_ref[...])
            return wt * _dot(act.astype(bf16), wd_ref[...])

        y = (ffn(wg_a_ref, wu_a_ref, wd_a_ref, wts[:, 0:1])
             + ffn(wg_b_ref, wu_b_ref, wd_b_ref, wts[:, 1:2]))
        _store_panels(ys_ref, y, tr)


def _experts_call(tile_ea, tile_eb, n_tiles, xs, wg, wu, wd, tr):
    n_exp, d, d_exp = wg.shape
    assert d // LANES == ROW_PANELS
    n_rows = xs.shape[0] // ROW_PANELS
    nt_max = n_rows // tr

    def row_map(i, ea, eb, nt):
        return (jnp.minimum(i, nt[0] - 1), 0)

    def wa_map(i, ea, eb, nt):
        return (ea[jnp.minimum(i, nt[0] - 1)], 0, 0)

    def wb_map(i, ea, eb, nt):
        return (eb[jnp.minimum(i, nt[0] - 1)], 0, 0)

    wspec = pl.BlockSpec
    grid_spec = pltpu.PrefetchScalarGridSpec(
        num_scalar_prefetch=3,
        grid=(nt_max,),
        in_specs=[pl.BlockSpec((tr * ROW_PANELS, LANES), row_map),
                  wspec((None, d, d_exp), wa_map),
                  wspec((None, d, d_exp), wb_map),
                  wspec((None, d, d_exp), wa_map),
                  wspec((None, d, d_exp), wb_map),
                  wspec((None, d_exp, d), wa_map),
                  wspec((None, d_exp, d), wb_map)],
        out_specs=pl.BlockSpec((tr * ROW_PANELS, LANES), row_map),
    )
    return pl.pallas_call(
        functools.partial(_experts_kernel, d=d, tr=tr),
        grid_spec=grid_spec,
        out_shape=jax.ShapeDtypeStruct((n_rows * ROW_PANELS, LANES), f32),
        compiler_params=_cparams(("arbitrary",)),
        name="experts",
    )(tile_ea, tile_eb, n_tiles, xs, wg, wg, wu, wu, wd, wd)


def _combine_kernel(pos_ref, posn_ref, h_ref, mod_ref, gf_ref, ys_ref, o_ref,
                    g_ref, sem, *, d, tm, n_steps, final_norm):
    i = pl.program_id(0)
    slot = i % 2
    py = d // LANES

    def issue(p_ref, s):
        def body(g, c):
            r0 = g * ROW_DMA_UNROLL
            idx = [p_ref[0, 0, r0 + u] for u in range(ROW_DMA_UNROLL)]
            for u in range(ROW_DMA_UNROLL):
                _row_copy(ys_ref, g_ref.at[s], idx[u], r0 + u, py,
                          sem.at[s]).start(priority=u % 2)
            return c

        lax.fori_loop(0, tm // ROW_DMA_UNROLL, body, 0)

    pl.when(i == 0)(lambda: issue(pos_ref, 0))
    pl.when(i + 1 < n_steps)(lambda: issue(posn_ref, 1 - slot))
    _wait_rows(ys_ref, g_ref.at[slot], tm, py, sem.at[slot])

    gate2 = mod_ref[:, 5 * d:6 * d]
    h = h_ref[...] + gate2 * _load_panels(g_ref.at[slot], tm, py)
    if final_norm:
        ms = jnp.mean(h * h, axis=-1, keepdims=True)
        h = h * lax.rsqrt(ms + EPS) * gf_ref[...]
    o_ref[...] = h


def _combine_call(pos3, h1, mod3, gf, ys, seq, tm, final_norm):
    t, d = h1.shape
    per = seq // tm
    n_steps = t // tm
    pos_spec = lambda f: pl.BlockSpec((1, 1, tm), f, memory_space=pltpu.SMEM)
    return pl.pallas_call(
        functools.partial(_combine_kernel, d=d, tm=tm, n_steps=n_steps, final_norm=final_norm),
        grid=(n_steps,),
        in_specs=[pos_spec(lambda i: (i, 0, 0)),
                  pos_spec(lambda i: (jnp.minimum(i + 1, n_steps - 1), 0, 0)),
                  pl.BlockSpec((tm, d), lambda i: (i, 0)),
                  pl.BlockSpec((None, 1, mod3.shape[2]), lambda i: (i // per, 0, 0)),
                  pl.BlockSpec((1, d), lambda i: (0, 0)),
                  pl.BlockSpec(memory_space=pl.ANY)],
        out_specs=pl.BlockSpec((tm, d), lambda i: (i, 0)),
        out_shape=jax.ShapeDtypeStruct((t, d), f32),
        scratch_shapes=[pltpu.VMEM((2, tm * (d // LANES), LANES), f32),
                        pltpu.SemaphoreType.DMA((2,))],
        compiler_params=_cparams(("arbitrary",)),
        name="combine",
    )(pos3, pos3, h1, mod3, gf, ys)


def _row_tile(seq, target):
    tm = min(target, seq)
    assert seq % tm == 0 and tm % 8 == 0
    return tm


def _layer(h2, mod3, p, bsz, seq, final_gain, final_norm):
    t, d = h2.shape
    g, n_p, n_h = p["ssm_b_re"].shape
    d_ssm = g * n_h
    d_pool = p["pool_scale"].shape[-1]
    n_grp = p["router_coarse_w"].shape[-1]
    n_exp = p["router_fine_w"].shape[-1]
    assert n_grp + n_exp <= ROUTE_LANES // 2 and seq % SSM_SLABS == 0
    tm = _row_tile(seq, 512)

    swap = lambda v: jnp.swapaxes(v, 1, 2)
    pw, bbd, cbd = _ssm_prep_call(
        p["ssm_lam_re"], p["ssm_lam_im"], p["ssm_log_dt"], swap(p["ssm_b_re"]),
        swap(p["ssm_b_im"]), swap(p["ssm_c_re"]), swap(p["ssm_c_im"]), SSM_SLABS)
    bbd, cbd = bbd.astype(bf16), cbd.astype(bf16)
    pw2 = pw.reshape(SSM_SLABS, 2 * g * n_p)

    w_in, b_in = p["w_in"], p["b_in"]
    n_sp = d_ssm + d_pool
    a3, b2, u2d = _mixers_call(h2, mod3, p["norm1_g"].reshape(1, d), w_in[:, :n_sp].astype(bf16),
                          b_in[:n_sp].reshape(1, n_sp), bbd, cbd, pw2,
                          p["ssm_d"].reshape(1, d_ssm),
                          p["ssm_w_glu"].astype(bf16), p["ssm_b_glu"].reshape(1, d_ssm),
                          p["pool_w"].astype(bf16), p["pool_scale"].reshape(1, d_pool),
                          bsz, seq, tm)

    wr = jnp.concatenate([p["router_coarse_w"], p["router_fine_w"]], axis=1)
    wr_hi = wr.astype(bf16)
    wr_lo = (wr - wr_hi.astype(f32)).astype(bf16)
    padc = ROUTE_LANES // 2 - wr.shape[1]
    wr_cat = jnp.concatenate([jnp.pad(wr_hi, ((0, 0), (0, padc))),
                              jnp.pad(wr_lo, ((0, 0), (0, padc)))], axis=1)
    br = jnp.pad(jnp.concatenate([p["router_coarse_b"], p["router_fine_b"]]),
                 (0, ROUTE_LANES - wr.shape[1])).reshape(1, ROUTE_LANES)

    h1, up, metat, cnt = _post_call(
        h2, a3, b2, mod3, u2d, p["norm2_g"].reshape(1, d),
        w_in[:, n_sp:].astype(bf16), b_in[n_sp:].reshape(1, 2 * d),
        p["w_proj_ssm"].astype(bf16), p["w_proj_pool"].astype(bf16), p["w_out"].astype(bf16),
        wr_cat, br, seq, _row_tile(seq, 2 * tm), tm, n_grp, n_exp)

    tr = EXPERT_TILE_ROWS
    epg = n_exp // n_grp
    pairs = [(a, b) for a in range(epg) for b in range(a + 1, epg)]
    n_cls = n_grp * len(pairs)
    assert n_cls <= ROUTE_LANES
    cls_ea = jnp.asarray([gi * epg + a for gi in range(n_grp) for a, _ in pairs], jnp.int32)
    cls_eb = jnp.asarray([gi * epg + b for gi in range(n_grp) for _, b in pairs], jnp.int32)
    counts = cnt[0, 0:n_cls].astype(jnp.int32)
    padded = ((counts + tr - 1) // tr) * tr
    ends = jnp.cumsum(padded)
    offs = ends - padded
    ids = jnp.arange(n_cls, dtype=jnp.int32)
    cls = metat[0].astype(jnp.int32)
    off = jnp.sum(jnp.where(cls[None, :] == ids[:, None], offs[:, None], 0), axis=0)
    pos = off + metat[1].astype(jnp.int32)
    tms = _row_tile(seq, 2 * tm)
    pos3 = pos.reshape(t // tms, 1, tms)
    nt_max = t // tr + n_cls
    n_tiles = (ends[-1] // tr).astype(jnp.int32).reshape(1)
    tile_start = jnp.arange(nt_max, dtype=jnp.int32) * tr
    tile_cls = jnp.minimum(
        jnp.sum((ends[None, :] <= tile_start[:, None]).astype(jnp.int32), axis=1), n_cls - 1)

    xs = _scatter_call(up, pos3, nt_max * tr, tms, ROW_PANELS)
    ys = _experts_call(cls_ea[tile_cls], cls_eb[tile_cls], n_tiles, xs,
                       p["moe_w_gate"].astype(bf16), p["moe_w_up"].astype(bf16),
                       p["moe_w_down"].astype(bf16), tr)
    return _combine_call(pos3, h1, mod3, final_gain.reshape(1, d), ys, seq, tms, final_norm)


def kernel(x, c, w_mod, b_mod, norm1_g, w_in, b_in, ssm_lam_re, ssm_lam_im, ssm_log_dt, ssm_b_re, ssm_b_im, ssm_c_re, ssm_c_im, ssm_d, ssm_w_glu, ssm_b_glu, pool_w, pool_scale, w_proj_ssm, w_proj_pool, w_out, norm2_g, router_coarse_w, router_coarse_b, router_fine_w, router_fine_b, moe_w_gate, moe_w_up, moe_w_down, norm_f_g):
    bsz, seq, d = x.shape
    depth = w_mod.shape[0]
    per_layer = dict(
        norm1_g=norm1_g, w_in=w_in, b_in=b_in, ssm_lam_re=ssm_lam_re, ssm_lam_im=ssm_lam_im,
        ssm_log_dt=ssm_log_dt, ssm_b_re=ssm_b_re, ssm_b_im=ssm_b_im, ssm_c_re=ssm_c_re,
        ssm_c_im=ssm_c_im, ssm_d=ssm_d, ssm_w_glu=ssm_w_glu, ssm_b_glu=ssm_b_glu, pool_w=pool_w,
        pool_scale=pool_scale, w_proj_ssm=w_proj_ssm, w_proj_pool=w_proj_pool, w_out=w_out,
        norm2_g=norm2_g, router_coarse_w=router_coarse_w, router_coarse_b=router_coarse_b,
        router_fine_w=router_fine_w, router_fine_b=router_fine_b, moe_w_gate=moe_w_gate,
        moe_w_up=moe_w_up, moe_w_down=moe_w_down)
    h2 = x.reshape(bsz * seq, d)
    for l in range(depth):
        p = {k: v[l] for k, v in per_layer.items()}
        mod3 = _mod_call(c, w_mod[l], b_mod[l]).reshape(bsz, 1, N_MOD * d)
        h2 = _layer(h2, mod3, p, bsz, seq, norm_f_g, final_norm=(l == depth - 1))
    return h2.reshape(bsz, seq, d)
```

```python
import functools
import math

import jax
import jax.numpy as jnp
from jax import lax
from jax.experimental import pallas as pl
from jax.experimental.pallas import tpu as pltpu

EPS = 1e-6
POOL_WINDOWS = (2, 4, 8, 16)
TOP_K_FINE = 2
N_MOD = 6
LANES = 128
SSM_SLABS = 8
ROUTE_LANES = 128
ROW_DMA_UNROLL = 16
ROW_PANELS = 8
EXPERT_TILE_ROWS = 256
VMEM_LIMIT = 56 * 1024 * 1024
NEG = -1e30

f32 = jnp.float32
bf16 = jnp.bfloat16


def _cparams(sem):
    return pltpu.CompilerParams(dimension_semantics=sem, vmem_limit_bytes=VMEM_LIMIT)


def _dot(a, b):
    return jnp.dot(a, b, preferred_element_type=f32)


def _pack_bf16_pairs(v):
    half = v.shape[1] // 2
    hi = lax.bitcast_convert_type(v[:, 0:half].astype(bf16).astype(f32), jnp.uint32)
    lo = lax.bitcast_convert_type(v[:, half:].astype(bf16).astype(f32), jnp.uint32)
    return hi | (lo >> 16)


def _unpack_bf16_pairs(w):
    hi = lax.bitcast_convert_type(w & jnp.uint32(0xFFFF0000), f32)
    lo = lax.bitcast_convert_type(w << 16, f32)
    return jnp.concatenate([hi, lo], axis=1)


def _sigmoid(v):
    return 0.5 * jnp.tanh(0.5 * v) + 0.5


def _dot_split_rows(a, b):
    h = a.shape[0] // 2
    return jnp.concatenate([_dot(a[:h], b), _dot(a[h:], b)], axis=0)


def _store_panels(ref, val, rows, row0=0):
    p = val.shape[1] // LANES
    for j in range(p):
        ref[pl.ds(row0 * p + j, rows, stride=p), :] = val[:, j * LANES:(j + 1) * LANES]


def _load_panels(ref, rows, p):
    return jnp.concatenate([ref[pl.ds(j, rows, stride=p), :] for j in range(p)], axis=1)


def _split_bf16(v):
    hi = v.astype(bf16)
    return hi, (v - hi.astype(f32)).astype(bf16)


def _mod_kernel(c_ref, w_ref, b_ref, o_ref):
    c = c_ref[...]
    a_hi, a_lo = _split_bf16(c * _sigmoid(c))
    w_hi, w_lo = _split_bf16(w_ref[...])
    o_ref[...] = _dot(a_hi, w_hi) + (_dot(a_lo, w_hi) + _dot(a_hi, w_lo)) + b_ref[...]


def _mod_call(c, w_mod, b_mod):
    bsz, d = c.shape
    n = w_mod.shape[1]
    tn = d
    return pl.pallas_call(
        _mod_kernel,
        grid=(n // tn,),
        in_specs=[pl.BlockSpec((bsz, d), lambda j: (0, 0)),
                  pl.BlockSpec((d, tn), lambda j: (0, j)),
                  pl.BlockSpec((1, tn), lambda j: (0, j))],
        out_specs=pl.BlockSpec((bsz, tn), lambda j: (0, j)),
        out_shape=jax.ShapeDtypeStruct((bsz, n), f32),
        compiler_params=_cparams(("parallel",)),
        name="mod",
    )(c, w_mod, b_mod.reshape(1, n))


def _ssm_prep_kernel(lr_ref, li_ref, ldt_ref, btr_ref, bti_ref, ctr_ref, cti_ref,
                     pw_ref, bbd_ref, cbd_ref, *, n_pow, n_grp):
    n_h, n_p = btr_ref.shape[1], btr_ref.shape[2]
    gp = n_grp * n_p
    lr = lr_ref[...]
    li = li_ref[...]
    dt = jnp.exp(ldt_ref[...])
    for k in range(1, n_pow + 1):
        mag = jnp.exp(lr * dt * float(k))
        ang = li * dt * float(k)
        pw_ref[k - 1, 0] = mag * jnp.cos(ang)
        pw_ref[k - 1, 1] = mag * jnp.sin(ang)
    lb_re = pw_ref[0, 0]
    lb_im = pw_ref[0, 1]
    den = lr * lr + li * li
    nr = lb_re - 1.0
    f_re = (nr * lr + lb_im * li) / den
    f_im = (lb_im * lr - nr * li) / den
    bbd_ref[...] = jnp.zeros_like(bbd_ref)
    cbd_ref[...] = jnp.zeros_like(cbd_ref)
    for g in range(n_grp):
        fr, fi = f_re[g:g + 1, :], f_im[g:g + 1, :]
        hs, ps = pl.ds(g * n_h, n_h), pl.ds(g * n_p, n_p)
        bbd_ref[hs, ps] = fr * btr_ref[g] - fi * bti_ref[g]
        bbd_ref[hs, pl.ds(gp + g * n_p, n_p)] = fr * bti_ref[g] + fi * btr_ref[g]
        cbd_ref[ps, hs] = ctr_ref[g]
        cbd_ref[pl.ds(gp + g * n_p, n_p), hs] = -cti_ref[g]


def _ssm_prep_call(lam_re, lam_im, log_dt, bt_re, bt_im, ct_re, ct_im, n_pow):
    g, p = lam_re.shape
    h = bt_re.shape[1]
    return pl.pallas_call(
        functools.partial(_ssm_prep_kernel, n_pow=n_pow, n_grp=g),
        out_shape=(jax.ShapeDtypeStruct((n_pow, 2, g, p), f32),
                   jax.ShapeDtypeStruct((g * h, 2 * g * p), f32),
                   jax.ShapeDtypeStruct((2 * g * p, g * h), f32)),
        name="ssm_prep",
    )(lam_re, lam_im, log_dt.reshape(g, 1), bt_re, bt_im, ct_re, ct_im)


def _modulated_norm(x, gain, shift, scale):
    ms = jnp.mean(x * x, axis=-1, keepdims=True)
    y = x * lax.rsqrt(ms + EPS) * gain
    return y * (1.0 + scale) + shift


def _cmul_add(ar, ai, xr, xi, br, bi):
    return ar * xr - ai * xi + br, ar * xi + ai * xr + bi


def _mixers_kernel(x_ref, mod_ref, g1_ref, wsp_ref, bsp_ref,
                   bbd_ref, cbd_ref, pw_ref, dsk_ref, wglu_ref, bglu_ref, pw_pool_ref, psc_ref,
                   a_ref, b_ref, u_ref, us_ref, up_ref, e_ref, s_ref, y_ref,
                   *, seq, d, d_ssm, d_pool, n_state, slabs, ts):
    nc = seq // slabs
    nl = d_ssm // LANES
    ns = n_state

    for r0 in range(0, seq, ts):
        rows = pl.ds(r0, ts)
        u = _modulated_norm(x_ref[rows, :], g1_ref[...], mod_ref[:, 0:d], mod_ref[:, d:2 * d])
        u = u.astype(bf16)
        u_ref[rows, :] = u
        r = _dot(u, wsp_ref[...]) + bsp_ref[...]
        for j in range(nl):
            us_ref[j, rows, :] = r[:, j * LANES:(j + 1) * LANES]
        up_ref[rows, :] = r[:, d_ssm:]

    dsk = dsk_ref[...]
    l1r = pw_ref[0:1, 0:ns]
    l1i = pw_ref[0:1, ns:2 * ns]

    xr = xi = None
    for t in range(slabs):
        ut = jnp.concatenate([us_ref[j, pl.ds(t, nc, stride=slabs), :] for j in range(nl)], axis=1)
        bu = _dot(ut.astype(bf16), bbd_ref[...])
        br, bi = bu[:, 0:ns], bu[:, ns:2 * ns]
        if t == 0:
            xr, xi = br, bi
        else:
            xr, xi = _cmul_add(l1r, l1i, xr, xi, br, bi)
        xc = jnp.concatenate([xr, xi], axis=1).astype(bf16)
        y_ref[t] = _dot_split_rows(xc, cbd_ref[...]) + dsk * ut
    e_ref[:, 0:ns] = xr
    e_ref[:, ns:2 * ns] = xi

    lLr = pw_ref[slabs - 1:slabs, 0:ns]
    lLi = pw_ref[slabs - 1:slabs, ns:2 * ns]

    def chunk_step(c, carry):
        sr, si = carry
        s_ref[pl.ds(c, 1), 0:ns] = sr
        s_ref[pl.ds(c, 1), ns:2 * ns] = si
        er = e_ref[pl.ds(c, 1), 0:ns]
        ei = e_ref[pl.ds(c, 1), ns:2 * ns]
        return _cmul_add(lLr, lLi, sr, si, er, ei)

    zero = jnp.zeros((1, ns), f32)
    lax.fori_loop(0, nc, chunk_step, (zero, zero))

    sr = s_ref[:, 0:ns]
    si = s_ref[:, ns:2 * ns]
    for t in range(slabs):
        pr = pw_ref[t:t + 1, 0:ns]
        pi = pw_ref[t:t + 1, ns:2 * ns]
        zr = pr * sr - pi * si
        zi = pr * si + pi * sr
        zc = jnp.concatenate([zr, zi], axis=1).astype(bf16)
        y = y_ref[t] + _dot_split_rows(zc, cbd_ref[...])
        y = jax.nn.gelu(y)
        z = y * _sigmoid(_dot(y.astype(bf16), wglu_ref[...]) + bglu_ref[...])
        for j in range(nl):
            a_ref[j, pl.ds(t, nc, stride=slabs), :] = z[:, j * LANES:(j + 1) * LANES]

    gc = d_pool // len(POOL_WINDOWS)
    row = lax.broadcasted_iota(jnp.int32, (seq, gc), 0)
    for gi, w in enumerate(POOL_WINDOWS):
        lo = gi * gc
        v = up_ref[:, lo:lo + gc]
        acc = v
        span = 1
        while span < w:
            acc = acc + jnp.where(row >= span, pltpu.roll(acc, span, axis=0), 0.0)
            span *= 2
        cnt = jnp.minimum(row + 1, w).astype(f32)
        m = acc / cnt - v
        yg = _dot(m.astype(bf16), pw_pool_ref[gi])
        b_ref[:, lo:lo + gc] = (yg * psc_ref[:, lo:lo + gc]).astype(bf16)


def _mixers_call(x2, mod3, g1, w_sp, b_sp, bbd, cbd, pw, dsk, wglu, bglu, pool_w, pool_scale,
                 bsz, seq, ts):
    t, d = x2.shape
    d_ssm = dsk.shape[1]
    d_pool = pool_scale.shape[1]
    nl = d_ssm // LANES
    ns2 = bbd.shape[1]
    slabs = SSM_SLABS
    nc = seq // slabs
    kern = functools.partial(_mixers_kernel, seq=seq, d=d, d_ssm=d_ssm, d_pool=d_pool,
                             n_state=ns2 // 2, slabs=slabs, ts=ts)
    const2 = lambda b: (0, 0)
    return pl.pallas_call(
        kern,
        grid=(bsz,),
        in_specs=[pl.BlockSpec((seq, d), lambda b: (b, 0)),
                  pl.BlockSpec((None, 1, mod3.shape[2]), lambda b: (b, 0, 0)),
                  pl.BlockSpec((1, d), const2),
                  pl.BlockSpec(w_sp.shape, const2),
                  pl.BlockSpec(b_sp.shape, const2),
                  pl.BlockSpec(bbd.shape, const2),
                  pl.BlockSpec(cbd.shape, const2),
                  pl.BlockSpec(pw.shape, const2),
                  pl.BlockSpec(dsk.shape, const2),
                  pl.BlockSpec(wglu.shape, const2),
                  pl.BlockSpec(bglu.shape, const2),
                  pl.BlockSpec(pool_w.shape, lambda b: (0, 0, 0)),
                  pl.BlockSpec(pool_scale.shape, const2)],
        out_specs=[pl.BlockSpec((nl, seq, LANES), lambda b: (0, b, 0)),
                   pl.BlockSpec((seq, d_pool), lambda b: (b, 0)),
                   pl.BlockSpec((seq, d), lambda b: (b, 0))],
        out_shape=(jax.ShapeDtypeStruct((nl, t, LANES), f32),
                   jax.ShapeDtypeStruct((t, d_pool), bf16),
                   jax.ShapeDtypeStruct((t, d), bf16)),
        scratch_shapes=[pltpu.VMEM((nl, seq, LANES), f32),
                        pltpu.VMEM((seq, d_pool), f32),
                        pltpu.VMEM((nc, ns2), f32),
                        pltpu.VMEM((nc, ns2), f32),
                        pltpu.VMEM((slabs, nc, d_ssm), f32)],
        compiler_params=_cparams(("parallel",)),
        name="mixers",
    )(x2, mod3, g1, w_sp, b_sp, bbd, cbd, pw, dsk, wglu, bglu, pool_w, pool_scale)


def _post_kernel(x_ref, a_ref, b_ref, mod_ref, u_ref, g2_ref, wg_ref, bg_ref, wps_ref, wpp_ref,
                 wout_ref, wr_ref, br_ref,
                 h_ref, up_ref, metat_ref, cnt_ref, carry_ref,
                 *, d, d_ssm, n_grp, n_exp, tm, ts):
    i = pl.program_id(0)

    @pl.when(i == 0)
    def _():
        carry_ref[...] = jnp.zeros_like(carry_ref)

    for r0 in range(0, tm, ts):
        _post_rows(x_ref, a_ref, b_ref, mod_ref, u_ref, g2_ref, wg_ref, bg_ref, wps_ref, wpp_ref,
                   wout_ref, wr_ref, br_ref, h_ref, up_ref, metat_ref, carry_ref,
                   d=d, d_ssm=d_ssm, n_grp=n_grp, n_exp=n_exp, r0=r0, tm=ts)
    cnt_ref[...] = carry_ref[...]


def _post_rows(x_ref, a_ref, b_ref, mod_ref, u_ref, g2_ref, wg_ref, bg_ref, wps_ref, wpp_ref,
               wout_ref, wr_ref, br_ref, h_ref, up_ref, metat_ref, carry_ref,
               *, d, d_ssm, n_grp, n_exp, r0, tm):
    rows = pl.ds(r0, tm)
    x = x_ref[rows, :]
    gate1 = mod_ref[:, 2 * d:3 * d]
    shift2, scale2 = mod_ref[:, 3 * d:4 * d], mod_ref[:, 4 * d:5 * d]
    gates = _dot(u_ref[rows, :], wg_ref[...]) + bg_ref[...]
    a = jnp.concatenate([a_ref[j, rows, :] for j in range(d_ssm // LANES)], axis=1).astype(bf16)
    b = b_ref[rows, :]
    merged = (_sigmoid(gates[:, 0:d]) * _dot(a, wps_ref[...])
              + _sigmoid(gates[:, d:2 * d]) * _dot(b, wpp_ref[...]))
    h = x + gate1 * _dot(merged.astype(bf16), wout_ref[...])
    h_ref[rows, :] = h

    u2 = _modulated_norm(h, g2_ref[...], shift2, scale2)

    half = d // 2
    words = _pack_bf16_pairs(u2)

    u_hi = u2.astype(bf16)
    u_lo = (u2 - u_hi.astype(f32)).astype(bf16)
    r1 = _dot_split_rows(u_hi, wr_ref[...])
    r2 = _dot_split_rows(u_lo, wr_ref[...])
    lg = r1 + pltpu.roll(r1, ROUTE_LANES // 2, axis=1) + r2 + br_ref[...]

    lane = lax.broadcasted_iota(jnp.int32, (tm, ROUTE_LANES), 1).astype(f32)
    big = float(ROUTE_LANES)
    epg = float(n_exp // n_grp)
    is_c = lane < n_grp
    cl = jnp.where(is_c, lg, NEG)
    cmax = jnp.max(cl, axis=-1, keepdims=True)
    grp = jnp.min(jnp.where(cl == cmax, lane, big), axis=-1, keepdims=True)
    p_grp = 1.0 / jnp.sum(jnp.where(is_c, jnp.exp(cl - cmax), 0.0), axis=-1, keepdims=True)

    f_lo = n_grp + grp * epg
    fl = jnp.where((lane >= f_lo) & (lane < f_lo + epg), lg, NEG)
    f1 = jnp.max(fl, axis=-1, keepdims=True)
    i1 = jnp.min(jnp.where(fl == f1, lane, big), axis=-1, keepdims=True)
    fl2 = jnp.where(lane == i1, NEG, fl)
    f2 = jnp.max(fl2, axis=-1, keepdims=True)
    i2 = jnp.min(jnp.where(fl2 == f2, lane, big), axis=-1, keepdims=True)
    t2 = jnp.exp(f2 - f1)
    w0 = p_grp / (1.0 + t2)
    w1 = p_grp * t2 / (1.0 + t2)

    j0 = i1 - f_lo
    j1 = i2 - f_lo
    first = j0 < j1
    ja = jnp.minimum(j0, j1)
    jb = jnp.maximum(j0, j1)
    wa = jnp.where(first, w0, w1)
    wb = jnp.where(first, w1, w0)
    n_pair = epg * (epg - 1.0) * 0.5
    cls = grp * n_pair + ja * (2.0 * epg - ja - 1.0) * 0.5 + (jb - ja - 1.0)

    sel = lane == cls
    oh = jnp.where(sel, 1.0, 0.0)
    r_i = lax.broadcasted_iota(jnp.int32, (tm, tm), 0)
    c_i = lax.broadcasted_iota(jnp.int32, (tm, tm), 1)
    tri = jnp.where(c_i < r_i, 1.0, 0.0).astype(bf16)
    before = carry_ref[...] + _dot_split_rows(tri, oh.astype(bf16))
    rank = jnp.sum(jnp.where(sel, before, 0.0), axis=-1, keepdims=True)
    carry_ref[...] = carry_ref[...] + jnp.sum(oh, axis=0, keepdims=True)

    meta = jnp.zeros((tm, ROUTE_LANES), f32)
    for k, val in enumerate((cls, rank, wa, wb)):
        meta = jnp.where(lane == float(k), val, meta)

    wts = jnp.where(lane == 0.0, wa, jnp.where(lane == 1.0, wb, 0.0))
    pad = jnp.zeros((tm, (ROW_PANELS - 1) * LANES - half), jnp.uint32)
    row = jnp.concatenate([words, lax.bitcast_convert_type(wts, jnp.uint32), pad], axis=1)
    _store_panels(up_ref, row, tm, r0)
    metat_ref[:, rows] = meta.T[0:8, :]


def _post_call(x2, a3, b2, mod3, u2d, g2, wg, bg, wps, wpp, wout, wr, br, seq, tm, ts,
               n_grp, n_exp):
    t, d = x2.shape
    d_ssm = wps.shape[0]
    nl = a3.shape[0]
    per = seq // tm
    kern = functools.partial(_post_kernel, d=d, d_ssm=d_ssm, n_grp=n_grp, n_exp=n_exp,
                             tm=tm, ts=ts)
    const2 = lambda i: (0, 0)
    wspec = lambda w: pl.BlockSpec(w.shape, const2, pipeline_mode=pl.Buffered(1))
    return pl.pallas_call(
        kern,
        grid=(t // tm,),
        in_specs=[pl.BlockSpec((tm, d), lambda i: (i, 0)),
                  pl.BlockSpec((nl, tm, LANES), lambda i: (0, i, 0)),
                  pl.BlockSpec((tm, b2.shape[1]), lambda i: (i, 0)),
                  pl.BlockSpec((None, 1, mod3.shape[2]), lambda i: (i // per, 0, 0)),
                  pl.BlockSpec((tm, d), lambda i: (i, 0)),
                  pl.BlockSpec((1, d), const2),
                  wspec(wg),
                  pl.BlockSpec(bg.shape, const2),
                  wspec(wps),
                  wspec(wpp),
                  wspec(wout),
                  wspec(wr),
                  pl.BlockSpec(br.shape, const2)],
        out_specs=[pl.BlockSpec((tm, d), lambda i: (i, 0)),
                   pl.BlockSpec((tm * ROW_PANELS, LANES), lambda i: (i, 0)),
                   pl.BlockSpec((8, tm), lambda i: (0, i)),
                   pl.BlockSpec((1, ROUTE_LANES), const2)],
        out_shape=(jax.ShapeDtypeStruct((t, d), f32),
                   jax.ShapeDtypeStruct((t * ROW_PANELS, LANES), jnp.uint32),
                   jax.ShapeDtypeStruct((8, t), f32),
                   jax.ShapeDtypeStruct((1, ROUTE_LANES), f32)),
        scratch_shapes=[pltpu.VMEM((1, ROUTE_LANES), f32)],
        compiler_params=_cparams(("arbitrary",)),
        name="post",
    )(x2, a3, b2, mod3, u2d, g2, wg, bg, wps, wpp, wout, wr, br)


def _row_copy(src_ref, dst_ref, src_row, dst_row, p, sem):
    src = src_ref.at[pl.ds(pl.multiple_of(src_row * p, p), p)]
    dst = dst_ref.at[pl.ds(pl.multiple_of(dst_row * p, p), p)]
    return pltpu.make_async_copy(src, dst, sem)


def _wait_rows(src_ref, dst_ref, n_rows, p, sem):
    pltpu.make_async_copy(src_ref.at[pl.ds(0, n_rows * p)], dst_ref.at[pl.ds(0, n_rows * p)],
                          sem).wait()


def _scatter_kernel(pos_ref, up_ref, xs_ref, sem, *, tm, p):
    def issue(g, c):
        r0 = g * ROW_DMA_UNROLL
        idx = [pos_ref[0, 0, r0 + u] for u in range(ROW_DMA_UNROLL)]
        for u in range(ROW_DMA_UNROLL):
            _row_copy(up_ref, xs_ref, r0 + u, idx[u], p, sem).start(priority=u % 2)
        return c

    lax.fori_loop(0, tm // ROW_DMA_UNROLL, issue, 0)
    _wait_rows(up_ref, xs_ref, tm, p, sem)


def _scatter_call(up, pos3, n_rows, tm, p):
    n_steps = up.shape[0] // (tm * p)
    return pl.pallas_call(
        functools.partial(_scatter_kernel, tm=tm, p=p),
        grid=(n_steps,),
        in_specs=[pl.BlockSpec((1, 1, tm), lambda i: (i, 0, 0), memory_space=pltpu.SMEM),
                  pl.BlockSpec((tm * p, LANES), lambda i: (i, 0))],
        out_specs=pl.BlockSpec(memory_space=pl.ANY),
        out_shape=jax.ShapeDtypeStruct((n_rows * p, LANES), up.dtype),
        scratch_shapes=[pltpu.SemaphoreType.DMA(())],
        compiler_params=_cparams(("arbitrary",)),
        name="scatter_rows",
    )(pos3, up)


def _experts_kernel(ea_ref, eb_ref, nt_ref, xs_ref, wg_a_ref, wg_b_ref, wu_a_ref, wu_b_ref,
                    wd_a_ref, wd_b_ref, ys_ref, *, d, tr):
    i = pl.program_id(0)

    @pl.when(i < nt_ref[0])
    def _():
        px = d // 2 // LANES
        w = jnp.concatenate([xs_ref[pl.ds(j, tr, stride=ROW_PANELS), :] for j in range(px)],
                            axis=1)
        wts = lax.bitcast_convert_type(xs_ref[pl.ds(px, tr, stride=ROW_PANELS), :], f32)
        x = _unpack_bf16_pairs(w).astype(bf16)

        def ffn(wg_ref, wu_ref, wd_ref, wt):
            h1 = _dot(x, wg_ref[...])
            act = h1 * _sigmoid(h1) * _dot(x, wu_ref[...])
            return wt * _dot(act.astype(bf16), wd_ref[...])

        y = (ffn(wg_a_ref, wu_a_ref, wd_a_ref, wts[:, 0:1])
             + ffn(wg_b_ref, wu_b_ref, wd_b_ref, wts[:, 1:2]))
        _store_panels(ys_ref, _pack_bf16_pairs(y), tr)


def _experts_call(tile_ea, tile_eb, n_tiles, xs, wg, wu, wd, tr):
    n_exp, d, d_exp = wg.shape
    assert d // LANES == ROW_PANELS
    n_rows = xs.shape[0] // ROW_PANELS
    nt_max = n_rows // tr
    py = d // 2 // LANES

    def row_map(i, ea, eb, nt):
        return (jnp.minimum(i, nt[0] - 1), 0)

    def wa_map(i, ea, eb, nt):
        return (ea[jnp.minimum(i, nt[0] - 1)], 0, 0)

    def wb_map(i, ea, eb, nt):
        return (eb[jnp.minimum(i, nt[0] - 1)], 0, 0)

    wspec = pl.BlockSpec
    grid_spec = pltpu.PrefetchScalarGridSpec(
        num_scalar_prefetch=3,
        grid=(nt_max,),
        in_specs=[pl.BlockSpec((tr * ROW_PANELS, LANES), row_map),
                  wspec((None, d, d_exp), wa_map),
                  wspec((None, d, d_exp), wb_map),
                  wspec((None, d, d_exp), wa_map),
                  wspec((None, d, d_exp), wb_map),
                  wspec((None, d_exp, d), wa_map),
                  wspec((None, d_exp, d), wb_map)],
        out_specs=pl.BlockSpec((tr * py, LANES), row_map),
    )
    return pl.pallas_call(
        functools.partial(_experts_kernel, d=d, tr=tr),
        grid_spec=grid_spec,
        out_shape=jax.ShapeDtypeStruct((n_rows * py, LANES), jnp.uint32),
        compiler_params=_cparams(("arbitrary",)),
        name="experts",
    )(tile_ea, tile_eb, n_tiles, xs, wg, wg, wu, wu, wd, wd)


def _combine_kernel(pos_ref, posn_ref, h_ref, mod_ref, gf_ref, ys_ref, o_ref,
                    g_ref, sem, *, d, tm, n_steps, final_norm):
    i = pl.program_id(0)
    slot = i % 2
    py = d // 2 // LANES

    def issue(p_ref, s):
        def body(g, c):
            r0 = g * ROW_DMA_UNROLL
            idx = [p_ref[0, 0, r0 + u] for u in range(ROW_DMA_UNROLL)]
            for u in range(ROW_DMA_UNROLL):
                _row_copy(ys_ref, g_ref.at[s], idx[u], r0 + u, py,
                          sem.at[s]).start(priority=u % 2)
            return c

        lax.fori_loop(0, tm // ROW_DMA_UNROLL, body, 0)

    pl.when(i == 0)(lambda: issue(pos_ref, 0))
    pl.when(i + 1 < n_steps)(lambda: issue(posn_ref, 1 - slot))
    _wait_rows(ys_ref, g_ref.at[slot], tm, py, sem.at[slot])

    gate2 = mod_ref[:, 5 * d:6 * d]
    h = h_ref[...] + gate2 * _unpack_bf16_pairs(_load_panels(g_ref.at[slot], tm, py))
    if final_norm:
        ms = jnp.mean(h * h, axis=-1, keepdims=True)
        h = h * lax.rsqrt(ms + EPS) * gf_ref[...]
    o_ref[...] = h


def _combine_call(pos3, h1, mod3, gf, ys, seq, tm, final_norm):
    t, d = h1.shape
    per = seq // tm
    n_steps = t // tm
    pos_spec = lambda f: pl.BlockSpec((1, 1, tm), f, memory_space=pltpu.SMEM)
    return pl.pallas_call(
        functools.partial(_combine_kernel, d=d, tm=tm, n_steps=n_steps, final_norm=final_norm),
        grid=(n_steps,),
        in_specs=[pos_spec(lambda i: (i, 0, 0)),
                  pos_spec(lambda i: (jnp.minimum(i + 1, n_steps - 1), 0, 0)),
                  pl.BlockSpec((tm, d), lambda i: (i, 0)),
                  pl.BlockSpec((None, 1, mod3.shape[2]), lambda i: (i // per, 0, 0)),
                  pl.BlockSpec((1, d), lambda i: (0, 0)),
                  pl.BlockSpec(memory_space=pl.ANY)],
        out_specs=pl.BlockSpec((tm, d), lambda i: (i, 0)),
        out_shape=jax.ShapeDtypeStruct((t, d), f32),
        scratch_shapes=[pltpu.VMEM((2, tm * (d // 2 // LANES), LANES), jnp.uint32),
                        pltpu.SemaphoreType.DMA((2,))],
        compiler_params=_cparams(("arbitrary",)),
        name="combine",
    )(pos3, pos3, h1, mod3, gf, ys)


def _row_tile(seq, target):
    tm = min(target, seq)
    assert seq % tm == 0 and tm % 8 == 0
    return tm


def _layer(h2, mod3, p, bsz, seq, final_gain, final_norm):
    t, d = h2.shape
    g, n_p, n_h = p["ssm_b_re"].shape
    d_ssm = g * n_h
    d_pool = p["pool_scale"].shape[-1]
    n_grp = p["router_coarse_w"].shape[-1]
    n_exp = p["router_fine_w"].shape[-1]
    assert n_grp + n_exp <= ROUTE_LANES // 2 and seq % SSM_SLABS == 0
    tm = _row_tile(seq, 512)

    swap = lambda v: jnp.swapaxes(v, 1, 2)
    pw, bbd, cbd = _ssm_prep_call(
        p["ssm_lam_re"], p["ssm_lam_im"], p["ssm_log_dt"], swap(p["ssm_b_re"]),
        swap(p["ssm_b_im"]), swap(p["ssm_c_re"]), swap(p["ssm_c_im"]), SSM_SLABS)
    bbd, cbd = bbd.astype(bf16), cbd.astype(bf16)
    pw2 = pw.reshape(SSM_SLABS, 2 * g * n_p)

    w_in, b_in = p["w_in"], p["b_in"]
    n_sp = d_ssm + d_pool
    a3, b2, u2d = _mixers_call(h2, mod3, p["norm1_g"].reshape(1, d), w_in[:, :n_sp].astype(bf16),
                          b_in[:n_sp].reshape(1, n_sp), bbd, cbd, pw2,
                          p["ssm_d"].reshape(1, d_ssm),
                          p["ssm_w_glu"].astype(bf16), p["ssm_b_glu"].reshape(1, d_ssm),
                          p["pool_w"].astype(bf16), p["pool_scale"].reshape(1, d_pool),
                          bsz, seq, tm)

    wr = jnp.concatenate([p["router_coarse_w"], p["router_fine_w"]], axis=1)
    wr_hi = wr.astype(bf16)
    wr_lo = (wr - wr_hi.astype(f32)).astype(bf16)
    padc = ROUTE_LANES // 2 - wr.shape[1]
    wr_cat = jnp.concatenate([jnp.pad(wr_hi, ((0, 0), (0, padc))),
                              jnp.pad(wr_lo, ((0, 0), (0, padc)))], axis=1)
    br = jnp.pad(jnp.concatenate([p["router_coarse_b"], p["router_fine_b"]]),
                 (0, ROUTE_LANES - wr.shape[1])).reshape(1, ROUTE_LANES)

    h1, up, metat, cnt = _post_call(
        h2, a3, b2, mod3, u2d, p["norm2_g"].reshape(1, d),
        w_in[:, n_sp:].astype(bf16), b_in[n_sp:].reshape(1, 2 * d),
        p["w_proj_ssm"].astype(bf16), p["w_proj_pool"].astype(bf16), p["w_out"].astype(bf16),
        wr_cat, br, seq, _row_tile(seq, 2 * tm), tm, n_grp, n_exp)

    tr = EXPERT_TILE_ROWS
    epg = n_exp // n_grp
    pairs = [(a, b) for a in range(epg) for b in range(a + 1, epg)]
    n_cls = n_grp * len(pairs)
    assert n_cls <= ROUTE_LANES
    cls_ea = jnp.asarray([gi * epg + a for gi in range(n_grp) for a, _ in pairs], jnp.int32)
    cls_eb = jnp.asarray([gi * epg + b for gi in range(n_grp) for _, b in pairs], jnp.int32)
    counts = cnt[0, 0:n_cls].astype(jnp.int32)
    padded = ((counts + tr - 1) // tr) * tr
    ends = jnp.cumsum(padded)
    offs = ends - padded
    ids = jnp.arange(n_cls, dtype=jnp.int32)
    cls = metat[0].astype(jnp.int32)
    off = jnp.sum(jnp.where(cls[None, :] == ids[:, None], offs[:, None], 0), axis=0)
    pos = off + metat[1].astype(jnp.int32)
    tms = _row_tile(seq, 2 * tm)
    pos3 = pos.reshape(t // tms, 1, tms)
    nt_max = t // tr + n_cls
    n_tiles = (ends[-1] // tr).astype(jnp.int32).reshape(1)
    tile_start = jnp.arange(nt_max, dtype=jnp.int32) * tr
    tile_cls = jnp.minimum(
        jnp.sum((ends[None, :] <= tile_start[:, None]).astype(jnp.int32), axis=1), n_cls - 1)

    xs = _scatter_call(up, pos3, nt_max * tr, tms, ROW_PANELS)
    ys = _experts_call(cls_ea[tile_cls], cls_eb[tile_cls], n_tiles, xs,
                       p["moe_w_gate"].astype(bf16), p["moe_w_up"].astype(bf16),
                       p["moe_w_down"].astype(bf16), tr)
    return _combine_call(pos.reshape(t // tm, 1, tm), h1, mod3, final_gain.reshape(1, d), ys,
                         seq, tm, final_norm)


def kernel(x, c, w_mod, b_mod, norm1_g, w_in, b_in, ssm_lam_re, ssm_lam_im, ssm_log_dt, ssm_b_re, ssm_b_im, ssm_c_re, ssm_c_im, ssm_d, ssm_w_glu, ssm_b_glu, pool_w, pool_scale, w_proj_ssm, w_proj_pool, w_out, norm2_g, router_coarse_w, router_coarse_b, router_fine_w, router_fine_b, moe_w_gate, moe_w_up, moe_w_down, norm_f_g):
    bsz, seq, d = x.shape
    depth = w_mod.shape[0]
    per_layer = dict(
        norm1_g=norm1_g, w_in=w_in, b_in=b_in, ssm_lam_re=ssm_lam_re, ssm_lam_im=ssm_lam_im,
        ssm_log_dt=ssm_log_dt, ssm_b_re=ssm_b_re, ssm_b_im=ssm_b_im, ssm_c_re=ssm_c_re,
        ssm_c_im=ssm_c_im, ssm_d=ssm_d, ssm_w_glu=ssm_w_glu, ssm_b_glu=ssm_b_glu, pool_w=pool_w,
        pool_scale=pool_scale, w_proj_ssm=w_proj_ssm, w_proj_pool=w_proj_pool, w_out=w_out,
        norm2_g=norm2_g, router_coarse_w=router_coarse_w, router_coarse_b=router_coarse_b,
        router_fine_w=router_fine_w, router_fine_b=router_fine_b, moe_w_gate=moe_w_gate,
        moe_w_up=moe_w_up, moe_w_down=moe_w_down)
    h2 = x.reshape(bsz * seq, d)
    for l in range(depth):
        p = {k: v[l] for k, v in per_layer.items()}
        mod3 = _mod_call(c, w_mod[l], b_mod[l]).reshape(bsz, 1, N_MOD * d)
        h2 = _layer(h2, mod3, p, bsz, seq, norm_f_g, final_norm=(l == depth - 1))
    return h2.reshape(bsz, seq, d)
```

```python
import functools
import math

import jax
import jax.numpy as jnp
from jax import lax
from jax.experimental import pallas as pl
from jax.experimental.pallas import tpu as pltpu

EPS = 1e-6
POOL_WINDOWS = (2, 4, 8, 16)
TOP_K_FINE = 2
N_MOD = 6
LANES = 128
SSM_SLABS = 8
ROUTE_LANES = 128
ROW_DMA_UNROLL = 16
ROW_PANELS = 5
EXPERT_TILE_ROWS = 256
VMEM_LIMIT = 56 * 1024 * 1024
NEG = -1e30

f32 = jnp.float32
bf16 = jnp.bfloat16


def _cparams(sem):
    return pltpu.CompilerParams(dimension_semantics=sem, vmem_limit_bytes=VMEM_LIMIT)


def _dot(a, b):
    return jnp.dot(a, b, preferred_element_type=f32)


def _pack_bf16_pairs(v):
    half = v.shape[1] // 2
    hi = lax.bitcast_convert_type(v[:, 0:half].astype(bf16).astype(f32), jnp.uint32)
    lo = lax.bitcast_convert_type(v[:, half:].astype(bf16).astype(f32), jnp.uint32)
    return hi | (lo >> 16)


def _unpack_bf16_pairs(w):
    hi = lax.bitcast_convert_type(w & jnp.uint32(0xFFFF0000), f32)
    lo = lax.bitcast_convert_type(w << 16, f32)
    return jnp.concatenate([hi, lo], axis=1)


def _sigmoid(v):
    return 0.5 * jnp.tanh(0.5 * v) + 0.5


def _dot_split_rows(a, b):
    h = a.shape[0] // 2
    return jnp.concatenate([_dot(a[:h], b), _dot(a[h:], b)], axis=0)


def _store_panels(ref, val, rows, row0=0):
    p = val.shape[1] // LANES
    for j in range(p):
        ref[pl.ds(row0 * p + j, rows, stride=p), :] = val[:, j * LANES:(j + 1) * LANES]


def _load_panels(ref, rows, p):
    return jnp.concatenate([ref[pl.ds(j, rows, stride=p), :] for j in range(p)], axis=1)


def _split_bf16(v):
    hi = v.astype(bf16)
    return hi, (v - hi.astype(f32)).astype(bf16)


def _mod_kernel(c_ref, w_ref, b_ref, o_ref):
    c = c_ref[...]
    a_hi, a_lo = _split_bf16(c * _sigmoid(c))
    w_hi, w_lo = _split_bf16(w_ref[...])
    o_ref[...] = _dot(a_hi, w_hi) + (_dot(a_lo, w_hi) + _dot(a_hi, w_lo)) + b_ref[...]


def _mod_call(c, w_mod, b_mod):
    bsz, d = c.shape
    n = w_mod.shape[1]
    tn = d
    return pl.pallas_call(
        _mod_kernel,
        grid=(n // tn,),
        in_specs=[pl.BlockSpec((bsz, d), lambda j: (0, 0)),
                  pl.BlockSpec((d, tn), lambda j: (0, j)),
                  pl.BlockSpec((1, tn), lambda j: (0, j))],
        out_specs=pl.BlockSpec((bsz, tn), lambda j: (0, j)),
        out_shape=jax.ShapeDtypeStruct((bsz, n), f32),
        compiler_params=_cparams(("parallel",)),
        name="mod",
    )(c, w_mod, b_mod.reshape(1, n))


def _ssm_prep_kernel(lr_ref, li_ref, ldt_ref, btr_ref, bti_ref, ctr_ref, cti_ref,
                     pw_ref, bbd_ref, cbd_ref, *, n_pow, n_grp):
    n_h, n_p = btr_ref.shape[1], btr_ref.shape[2]
    gp = n_grp * n_p
    lr = lr_ref[...]
    li = li_ref[...]
    dt = jnp.exp(ldt_ref[...])
    for k in range(1, n_pow + 1):
        mag = jnp.exp(lr * dt * float(k))
        ang = li * dt * float(k)
        pw_ref[k - 1, 0] = mag * jnp.cos(ang)
        pw_ref[k - 1, 1] = mag * jnp.sin(ang)
    lb_re = pw_ref[0, 0]
    lb_im = pw_ref[0, 1]
    den = lr * lr + li * li
    nr = lb_re - 1.0
    f_re = (nr * lr + lb_im * li) / den
    f_im = (lb_im * lr - nr * li) / den
    bbd_ref[...] = jnp.zeros_like(bbd_ref)
    cbd_ref[...] = jnp.zeros_like(cbd_ref)
    for g in range(n_grp):
        fr, fi = f_re[g:g + 1, :], f_im[g:g + 1, :]
        hs, ps = pl.ds(g * n_h, n_h), pl.ds(g * n_p, n_p)
        bbd_ref[hs, ps] = fr * btr_ref[g] - fi * bti_ref[g]
        bbd_ref[hs, pl.ds(gp + g * n_p, n_p)] = fr * bti_ref[g] + fi * btr_ref[g]
        cbd_ref[ps, hs] = ctr_ref[g]
        cbd_ref[pl.ds(gp + g * n_p, n_p), hs] = -cti_ref[g]


def _ssm_prep_call(lam_re, lam_im, log_dt, bt_re, bt_im, ct_re, ct_im, n_pow):
    g, p = lam_re.shape
    h = bt_re.shape[1]
    return pl.pallas_call(
        functools.partial(_ssm_prep_kernel, n_pow=n_pow, n_grp=g),
        out_shape=(jax.ShapeDtypeStruct((n_pow, 2, g, p), f32),
                   jax.ShapeDtypeStruct((g * h, 2 * g * p), f32),
                   jax.ShapeDtypeStruct((2 * g * p, g * h), f32)),
        name="ssm_prep",
    )(lam_re, lam_im, log_dt.reshape(g, 1), bt_re, bt_im, ct_re, ct_im)


def _modulated_norm(x, gain, shift, scale):
    ms = jnp.mean(x * x, axis=-1, keepdims=True)
    return x * lax.rsqrt(ms + EPS) * (gain * (1.0 + scale)) + shift


def _cmul_add(ar, ai, xr, xi, br, bi):
    return ar * xr - ai * xi + br, ar * xi + ai * xr + bi


def _mixers_kernel(x_ref, mod_ref, g1_ref, wsp_ref, bsp_ref,
                   bbd_ref, cbd_ref, pw_ref, dsk_ref, wglu_ref, bglu_ref, pw_pool_ref, psc_ref,
                   a_ref, b_ref, u_ref, us_ref, up_ref, e_ref, s_ref, y_ref,
                   *, seq, d, d_ssm, d_pool, n_state, slabs, ts):
    nc = seq // slabs
    nl = d_ssm // LANES
    ns = n_state

    for r0 in range(0, seq, ts):
        rows = pl.ds(r0, ts)
        u = _modulated_norm(x_ref[rows, :], g1_ref[...], mod_ref[:, 0:d], mod_ref[:, d:2 * d])
        u = u.astype(bf16)
        u_ref[rows, :] = u
        r = _dot(u, wsp_ref[...]) + bsp_ref[...]
        for j in range(nl):
            us_ref[j, rows, :] = r[:, j * LANES:(j + 1) * LANES]
        up_ref[rows, :] = r[:, d_ssm:]

    dsk = dsk_ref[...]
    l1r = pw_ref[0:1, 0:ns]
    l1i = pw_ref[0:1, ns:2 * ns]

    xr = xi = None
    for t in range(slabs):
        ut = jnp.concatenate([us_ref[j, pl.ds(t, nc, stride=slabs), :] for j in range(nl)], axis=1)
        bu = _dot(ut.astype(bf16), bbd_ref[...])
        br, bi = bu[:, 0:ns], bu[:, ns:2 * ns]
        if t == 0:
            xr, xi = br, bi
        else:
            xr, xi = _cmul_add(l1r, l1i, xr, xi, br, bi)
        xc = jnp.concatenate([xr, xi], axis=1).astype(bf16)
        y_ref[t] = _dot_split_rows(xc, cbd_ref[...]) + dsk * ut
    e_ref[:, 0:ns] = xr
    e_ref[:, ns:2 * ns] = xi

    lLr = pw_ref[slabs - 1:slabs, 0:ns]
    lLi = pw_ref[slabs - 1:slabs, ns:2 * ns]

    def chunk_step(c, carry):
        sr, si = carry
        s_ref[pl.ds(c, 1), 0:ns] = sr
        s_ref[pl.ds(c, 1), ns:2 * ns] = si
        er = e_ref[pl.ds(c, 1), 0:ns]
        ei = e_ref[pl.ds(c, 1), ns:2 * ns]
        return _cmul_add(lLr, lLi, sr, si, er, ei)

    zero = jnp.zeros((1, ns), f32)
    lax.fori_loop(0, nc, chunk_step, (zero, zero))

    sr = s_ref[:, 0:ns]
    si = s_ref[:, ns:2 * ns]
    for t in range(slabs):
        pr = pw_ref[t:t + 1, 0:ns]
        pi = pw_ref[t:t + 1, ns:2 * ns]
        zr = pr * sr - pi * si
        zi = pr * si + pi * sr
        zc = jnp.concatenate([zr, zi], axis=1).astype(bf16)
        y = y_ref[t] + _dot_split_rows(zc, cbd_ref[...])
        y = jax.nn.gelu(y)
        z = y * _sigmoid(_dot(y.astype(bf16), wglu_ref[...]) + bglu_ref[...])
        for j in range(nl):
            a_ref[j, pl.ds(t, nc, stride=slabs), :] = z[:, j * LANES:(j + 1) * LANES]

    gc = d_pool // len(POOL_WINDOWS)
    row = lax.broadcasted_iota(jnp.int32, (seq, gc), 0)
    for gi, w in enumerate(POOL_WINDOWS):
        lo = gi * gc
        v = up_ref[:, lo:lo + gc]
        acc = v
        span = 1
        while span < w:
            acc = acc + jnp.where(row >= span, pltpu.roll(acc, span, axis=0), 0.0)
            span *= 2
        cnt = jnp.minimum(row + 1, w).astype(f32)
        m = acc / cnt - v
        yg = _dot(m.astype(bf16), pw_pool_ref[gi])
        b_ref[:, lo:lo + gc] = (yg * psc_ref[:, lo:lo + gc]).astype(bf16)


def _mixers_call(x2, mod3, g1, w_sp, b_sp, bbd, cbd, pw, dsk, wglu, bglu, pool_w, pool_scale,
                 bsz, seq, ts):
    t, d = x2.shape
    d_ssm = dsk.shape[1]
    d_pool = pool_scale.shape[1]
    nl = d_ssm // LANES
    ns2 = bbd.shape[1]
    slabs = SSM_SLABS
    nc = seq // slabs
    kern = functools.partial(_mixers_kernel, seq=seq, d=d, d_ssm=d_ssm, d_pool=d_pool,
                             n_state=ns2 // 2, slabs=slabs, ts=ts)
    const2 = lambda b: (0, 0)
    return pl.pallas_call(
        kern,
        grid=(bsz,),
        in_specs=[pl.BlockSpec((seq, d), lambda b: (b, 0)),
                  pl.BlockSpec((None, 1, mod3.shape[2]), lambda b: (b, 0, 0)),
                  pl.BlockSpec((1, d), const2),
                  pl.BlockSpec(w_sp.shape, const2),
                  pl.BlockSpec(b_sp.shape, const2),
                  pl.BlockSpec(bbd.shape, const2),
                  pl.BlockSpec(cbd.shape, const2),
                  pl.BlockSpec(pw.shape, const2),
                  pl.BlockSpec(dsk.shape, const2),
                  pl.BlockSpec(wglu.shape, const2),
                  pl.BlockSpec(bglu.shape, const2),
                  pl.BlockSpec(pool_w.shape, lambda b: (0, 0, 0)),
                  pl.BlockSpec(pool_scale.shape, const2)],
        out_specs=[pl.BlockSpec((nl, seq, LANES), lambda b: (0, b, 0)),
                   pl.BlockSpec((seq, d_pool), lambda b: (b, 0)),
                   pl.BlockSpec((seq, d), lambda b: (b, 0))],
        out_shape=(jax.ShapeDtypeStruct((nl, t, LANES), f32),
                   jax.ShapeDtypeStruct((t, d_pool), bf16),
                   jax.ShapeDtypeStruct((t, d), bf16)),
        scratch_shapes=[pltpu.VMEM((nl, seq, LANES), f32),
                        pltpu.VMEM((seq, d_pool), f32),
                        pltpu.VMEM((nc, ns2), f32),
                        pltpu.VMEM((nc, ns2), f32),
                        pltpu.VMEM((slabs, nc, d_ssm), f32)],
        compiler_params=_cparams(("parallel",)),
        name="mixers",
    )(x2, mod3, g1, w_sp, b_sp, bbd, cbd, pw, dsk, wglu, bglu, pool_w, pool_scale)


def _post_kernel(x_ref, a_ref, b_ref, mod_ref, u_ref, g2_ref, wg_ref, bg_ref, wps_ref, wpp_ref,
                 wout_ref, wr_ref, br_ref, tri_ref,
                 h_ref, up_ref, metat_ref, cnt_ref, carry_ref,
                 *, d, d_ssm, n_grp, n_exp, tm, ts):
    i = pl.program_id(0)

    @pl.when(i == 0)
    def _():
        carry_ref[...] = jnp.zeros_like(carry_ref)

    for r0 in range(0, tm, ts):
        _post_rows(x_ref, a_ref, b_ref, mod_ref, u_ref, g2_ref, wg_ref, bg_ref, wps_ref, wpp_ref,
                   wout_ref, wr_ref, br_ref, tri_ref, h_ref, up_ref, metat_ref, carry_ref,
                   d=d, d_ssm=d_ssm, n_grp=n_grp, n_exp=n_exp, r0=r0, tm=ts)
    cnt_ref[...] = carry_ref[...]


def _post_rows(x_ref, a_ref, b_ref, mod_ref, u_ref, g2_ref, wg_ref, bg_ref, wps_ref, wpp_ref,
               wout_ref, wr_ref, br_ref, tri_ref, h_ref, up_ref, metat_ref, carry_ref,
               *, d, d_ssm, n_grp, n_exp, r0, tm):
    rows = pl.ds(r0, tm)
    x = x_ref[rows, :]
    gate1 = mod_ref[:, 2 * d:3 * d]
    shift2, scale2 = mod_ref[:, 3 * d:4 * d], mod_ref[:, 4 * d:5 * d]
    gates = _dot(u_ref[rows, :], wg_ref[...]) + bg_ref[...]
    a = jnp.concatenate([a_ref[j, rows, :] for j in range(d_ssm // LANES)], axis=1).astype(bf16)
    b = b_ref[rows, :]
    merged = (_sigmoid(gates[:, 0:d]) * _dot(a, wps_ref[...])
              + _sigmoid(gates[:, d:2 * d]) * _dot(b, wpp_ref[...]))
    h = x + gate1 * _dot(merged.astype(bf16), wout_ref[...])
    h_ref[rows, :] = h

    u2 = _modulated_norm(h, g2_ref[...], shift2, scale2)

    half = d // 2
    words = _pack_bf16_pairs(u2)

    u_hi = u2.astype(bf16)
    u_lo = (u2 - u_hi.astype(f32)).astype(bf16)
    r1 = _dot_split_rows(u_hi, wr_ref[...])
    r2 = _dot_split_rows(u_lo, wr_ref[...])
    lg = r1 + pltpu.roll(r1, ROUTE_LANES // 2, axis=1) + r2 + br_ref[...]

    lane = lax.broadcasted_iota(jnp.int32, (tm, ROUTE_LANES), 1).astype(f32)
    big = float(ROUTE_LANES)
    epg = float(n_exp // n_grp)
    is_c = lane < n_grp
    cl = jnp.where(is_c, lg, NEG)
    cmax = jnp.max(cl, axis=-1, keepdims=True)
    grp = jnp.min(jnp.where(cl == cmax, lane, big), axis=-1, keepdims=True)
    p_grp = 1.0 / jnp.sum(jnp.where(is_c, jnp.exp(cl - cmax), 0.0), axis=-1, keepdims=True)

    f_lo = n_grp + grp * epg
    fl = jnp.where((lane >= f_lo) & (lane < f_lo + epg), lg, NEG)
    f1 = jnp.max(fl, axis=-1, keepdims=True)
    i1 = jnp.min(jnp.where(fl == f1, lane, big), axis=-1, keepdims=True)
    fl2 = jnp.where(lane == i1, NEG, fl)
    f2 = jnp.max(fl2, axis=-1, keepdims=True)
    i2 = jnp.min(jnp.where(fl2 == f2, lane, big), axis=-1, keepdims=True)
    t2 = jnp.exp(f2 - f1)
    w0 = p_grp / (1.0 + t2)
    w1 = p_grp * t2 / (1.0 + t2)

    j0 = i1 - f_lo
    j1 = i2 - f_lo
    first = j0 < j1
    ja = jnp.minimum(j0, j1)
    jb = jnp.maximum(j0, j1)
    wa = jnp.where(first, w0, w1)
    wb = jnp.where(first, w1, w0)
    n_pair = epg * (epg - 1.0) * 0.5
    cls = grp * n_pair + ja * (2.0 * epg - ja - 1.0) * 0.5 + (jb - ja - 1.0)

    sel = lane == cls
    oh = jnp.where(sel, 1.0, 0.0)
    before = carry_ref[...] + _dot_split_rows(tri_ref[...], oh.astype(bf16))
    rank = jnp.sum(jnp.where(sel, before, 0.0), axis=-1, keepdims=True)
    carry_ref[...] = carry_ref[...] + jnp.sum(oh, axis=0, keepdims=True)

    meta = jnp.zeros((tm, ROUTE_LANES), f32)
    for k, val in enumerate((cls, rank, wa, wb)):
        meta = jnp.where(lane == float(k), val, meta)

    wts = jnp.where(lane == 0.0, wa, jnp.where(lane == 1.0, wb, 0.0))
    assert half + LANES == ROW_PANELS * LANES
    row = jnp.concatenate([words, lax.bitcast_convert_type(wts, jnp.uint32)], axis=1)
    _store_panels(up_ref, row, tm, r0)
    metat_ref[:, rows] = meta.T[0:8, :]


def _post_call(x2, a3, b2, mod3, u2d, g2, wg, bg, wps, wpp, wout, wr, br, seq, tm, ts,
               n_grp, n_exp):
    t, d = x2.shape
    d_ssm = wps.shape[0]
    nl = a3.shape[0]
    per = seq // tm
    tri = jnp.tril(jnp.ones((ts, ts), bf16), -1)
    kern = functools.partial(_post_kernel, d=d, d_ssm=d_ssm, n_grp=n_grp, n_exp=n_exp,
                             tm=tm, ts=ts)
    const2 = lambda i: (0, 0)
    wspec = lambda w: pl.BlockSpec(w.shape, const2, pipeline_mode=pl.Buffered(1))
    return pl.pallas_call(
        kern,
        grid=(t // tm,),
        in_specs=[pl.BlockSpec((tm, d), lambda i: (i, 0)),
                  pl.BlockSpec((nl, tm, LANES), lambda i: (0, i, 0)),
                  pl.BlockSpec((tm, b2.shape[1]), lambda i: (i, 0)),
                  pl.BlockSpec((None, 1, mod3.shape[2]), lambda i: (i // per, 0, 0)),
                  pl.BlockSpec((tm, d), lambda i: (i, 0)),
                  pl.BlockSpec((1, d), const2),
                  wspec(wg),
                  pl.BlockSpec(bg.shape, const2),
                  wspec(wps),
                  wspec(wpp),
                  wspec(wout),
                  wspec(wr),
                  pl.BlockSpec(br.shape, const2),
                  wspec(tri)],
        out_specs=[pl.BlockSpec((tm, d), lambda i: (i, 0)),
                   pl.BlockSpec((tm * ROW_PANELS, LANES), lambda i: (i, 0)),
                   pl.BlockSpec((8, tm), lambda i: (0, i)),
                   pl.BlockSpec((1, ROUTE_LANES), const2)],
        out_shape=(jax.ShapeDtypeStruct((t, d), f32),
                   jax.ShapeDtypeStruct((t * ROW_PANELS, LANES), jnp.uint32),
                   jax.ShapeDtypeStruct((8, t), f32),
                   jax.ShapeDtypeStruct((1, ROUTE_LANES), f32)),
        scratch_shapes=[pltpu.VMEM((1, ROUTE_LANES), f32)],
        compiler_params=_cparams(("arbitrary",)),
        name="post",
    )(x2, a3, b2, mod3, u2d, g2, wg, bg, wps, wpp, wout, wr, br, tri)


def _row_copy(src_ref, dst_ref, src_row, dst_row, p, sem):
    src = src_ref.at[pl.ds(pl.multiple_of(src_row * p, p), p)]
    dst = dst_ref.at[pl.ds(pl.multiple_of(dst_row * p, p), p)]
    return pltpu.make_async_copy(src, dst, sem)


def _wait_rows(src_ref, dst_ref, n_rows, p, sem):
    pltpu.make_async_copy(src_ref.at[pl.ds(0, n_rows * p)], dst_ref.at[pl.ds(0, n_rows * p)],
                          sem).wait()


def _scatter_kernel(pos_ref, up_ref, xs_ref, sem, *, tm, p):
    def issue(g, c):
        r0 = g * ROW_DMA_UNROLL
        idx = [pos_ref[0, 0, r0 + u] for u in range(ROW_DMA_UNROLL)]
        for u in range(ROW_DMA_UNROLL):
            _row_copy(up_ref, xs_ref, r0 + u, idx[u], p, sem).start(priority=u % 2)
        return c

    lax.fori_loop(0, tm // ROW_DMA_UNROLL, issue, 0)
    _wait_rows(up_ref, xs_ref, tm, p, sem)


def _scatter_call(up, pos3, n_rows, tm, p):
    n_steps = up.shape[0] // (tm * p)
    return pl.pallas_call(
        functools.partial(_scatter_kernel, tm=tm, p=p),
        grid=(n_steps,),
        in_specs=[pl.BlockSpec((1, 1, tm), lambda i: (i, 0, 0), memory_space=pltpu.SMEM),
                  pl.BlockSpec((tm * p, LANES), lambda i: (i, 0))],
        out_specs=pl.BlockSpec(memory_space=pl.ANY),
        out_shape=jax.ShapeDtypeStruct((n_rows * p, LANES), up.dtype),
        scratch_shapes=[pltpu.SemaphoreType.DMA(())],
        compiler_params=_cparams(("arbitrary",)),
        name="scatter_rows",
    )(pos3, up)


def _experts_kernel(ea_ref, eb_ref, nt_ref, xs_ref, wg_a_ref, wg_b_ref, wu_a_ref, wu_b_ref,
                    wd_a_ref, wd_b_ref, ys_ref, *, d, tr):
    i = pl.program_id(0)

    @pl.when(i < nt_ref[0])
    def _():
        px = d // 2 // LANES
        w = jnp.concatenate([xs_ref[pl.ds(j, tr, stride=ROW_PANELS), :] for j in range(px)],
                            axis=1)
        wts = lax.bitcast_convert_type(xs_ref[pl.ds(px, tr, stride=ROW_PANELS), :], f32)
        x = _unpack_bf16_pairs(w).astype(bf16)

        def ffn(wg_ref, wu_ref, wd_ref, wt):
            h1 = _dot(x, wg_ref[...])
            act = h1 * _sigmoid(h1) * _dot(x, wu_ref[...])
            return wt * _dot(act.astype(bf16), wd_ref[...])

        y = (ffn(wg_a_ref, wu_a_ref, wd_a_ref, wts[:, 0:1])
             + ffn(wg_b_ref, wu_b_ref, wd_b_ref, wts[:, 1:2]))
        _store_panels(ys_ref, _pack_bf16_pairs(y), tr)


def _experts_call(tile_ea, tile_eb, n_tiles, xs, wg, wu, wd, tr):
    n_exp, d, d_exp = wg.shape
    n_rows = xs.shape[0] // ROW_PANELS
    nt_max = n_rows // tr
    py = d // 2 // LANES

    def row_map(i, ea, eb, nt):
        return (jnp.minimum(i, nt[0] - 1), 0)

    def wa_map(i, ea, eb, nt):
        return (ea[jnp.minimum(i, nt[0] - 1)], 0, 0)

    def wb_map(i, ea, eb, nt):
        return (eb[jnp.minimum(i, nt[0] - 1)], 0, 0)

    wspec = pl.BlockSpec
    grid_spec = pltpu.PrefetchScalarGridSpec(
        num_scalar_prefetch=3,
        grid=(nt_max,),
        in_specs=[pl.BlockSpec((tr * ROW_PANELS, LANES), row_map),
                  wspec((None, d, d_exp), wa_map),
                  wspec((None, d, d_exp), wb_map),
                  wspec((None, d, d_exp), wa_map),
                  wspec((None, d, d_exp), wb_map),
                  wspec((None, d_exp, d), wa_map),
                  wspec((None, d_exp, d), wb_map)],
        out_specs=pl.BlockSpec((tr * py, LANES), row_map),
    )
    return pl.pallas_call(
        functools.partial(_experts_kernel, d=d, tr=tr),
        grid_spec=grid_spec,
        out_shape=jax.ShapeDtypeStruct((n_rows * py, LANES), jnp.uint32),
        compiler_params=_cparams(("arbitrary",)),
        name="experts",
    )(tile_ea, tile_eb, n_tiles, xs, wg, wg, wu, wu, wd, wd)


def _combine_kernel(pos_ref, posn_ref, h_ref, mod_ref, gf_ref, ys_ref, o_ref,
                    g_ref, sem, *, d, tm, n_steps, final_norm):
    i = pl.program_id(0)
    slot = i % 2
    py = d // 2 // LANES

    def issue(p_ref, s):
        def body(g, c):
            r0 = g * ROW_DMA_UNROLL
            idx = [p_ref[0, 0, r0 + u] for u in range(ROW_DMA_UNROLL)]
            for u in range(ROW_DMA_UNROLL):
                _row_copy(ys_ref, g_ref.at[s], idx[u], r0 + u, py,
                          sem.at[s]).start(priority=u % 2)
            return c

        lax.fori_loop(0, tm // ROW_DMA_UNROLL, body, 0)

    pl.when(i == 0)(lambda: issue(pos_ref, 0))
    pl.when(i + 1 < n_steps)(lambda: issue(posn_ref, 1 - slot))
    _wait_rows(ys_ref, g_ref.at[slot], tm, py, sem.at[slot])

    gate2 = mod_ref[:, 5 * d:6 * d]
    h = h_ref[...] + gate2 * _unpack_bf16_pairs(_load_panels(g_ref.at[slot], tm, py))
    if final_norm:
        ms = jnp.mean(h * h, axis=-1, keepdims=True)
        h = h * lax.rsqrt(ms + EPS) * gf_ref[...]
    o_ref[...] = h


def _combine_call(pos3, h1, mod3, gf, ys, seq, tm, final_norm):
    t, d = h1.shape
    per = seq // tm
    n_steps = t // tm
    pos_spec = lambda f: pl.BlockSpec((1, 1, tm), f, memory_space=pltpu.SMEM)
    return pl.pallas_call(
        functools.partial(_combine_kernel, d=d, tm=tm, n_steps=n_steps, final_norm=final_norm),
        grid=(n_steps,),
        in_specs=[pos_spec(lambda i: (i, 0, 0)),
                  pos_spec(lambda i: (jnp.minimum(i + 1, n_steps - 1), 0, 0)),
                  pl.BlockSpec((tm, d), lambda i: (i, 0)),
                  pl.BlockSpec((None, 1, mod3.shape[2]), lambda i: (i // per, 0, 0)),
                  pl.BlockSpec((1, d), lambda i: (0, 0)),
                  pl.BlockSpec(memory_space=pl.ANY)],
        out_specs=pl.BlockSpec((tm, d), lambda i: (i, 0)),
        out_shape=jax.ShapeDtypeStruct((t, d), f32),
        scratch_shapes=[pltpu.VMEM((2, tm * (d // 2 // LANES), LANES), jnp.uint32),
                        pltpu.SemaphoreType.DMA((2,))],
        compiler_params=_cparams(("arbitrary",)),
        name="combine",
    )(pos3, pos3, h1, mod3, gf, ys)


def _row_tile(seq, target):
    tm = min(target, seq)
    assert seq % tm == 0 and tm % 8 == 0
    return tm


def _layer(h2, mod3, p, bsz, seq, final_gain, final_norm):
    t, d = h2.shape
    g, n_p, n_h = p["ssm_b_re"].shape
    d_ssm = g * n_h
    d_pool = p["pool_scale"].shape[-1]
    n_grp = p["router_coarse_w"].shape[-1]
    n_exp = p["router_fine_w"].shape[-1]
    assert n_grp + n_exp <= ROUTE_LANES // 2 and seq % SSM_SLABS == 0
    tm = _row_tile(seq, 512)

    swap = lambda v: jnp.swapaxes(v, 1, 2)
    pw, bbd, cbd = _ssm_prep_call(
        p["ssm_lam_re"], p["ssm_lam_im"], p["ssm_log_dt"], swap(p["ssm_b_re"]),
        swap(p["ssm_b_im"]), swap(p["ssm_c_re"]), swap(p["ssm_c_im"]), SSM_SLABS)
    bbd, cbd = bbd.astype(bf16), cbd.astype(bf16)
    pw2 = pw.reshape(SSM_SLABS, 2 * g * n_p)

    w_in, b_in = p["w_in"], p["b_in"]
    n_sp = d_ssm + d_pool
    a3, b2, u2d = _mixers_call(h2, mod3, p["norm1_g"].reshape(1, d), w_in[:, :n_sp].astype(bf16),
                          b_in[:n_sp].reshape(1, n_sp), bbd, cbd, pw2,
                          p["ssm_d"].reshape(1, d_ssm),
                          p["ssm_w_glu"].astype(bf16), p["ssm_b_glu"].reshape(1, d_ssm),
                          p["pool_w"].astype(bf16), p["pool_scale"].reshape(1, d_pool),
                          bsz, seq, tm)

    wr = jnp.concatenate([p["router_coarse_w"], p["router_fine_w"]], axis=1)
    wr_hi = wr.astype(bf16)
    wr_lo = (wr - wr_hi.astype(f32)).astype(bf16)
    padc = ROUTE_LANES // 2 - wr.shape[1]
    wr_cat = jnp.concatenate([jnp.pad(wr_hi, ((0, 0), (0, padc))),
                              jnp.pad(wr_lo, ((0, 0), (0, padc)))], axis=1)
    br = jnp.pad(jnp.concatenate([p["router_coarse_b"], p["router_fine_b"]]),
                 (0, ROUTE_LANES - wr.shape[1])).reshape(1, ROUTE_LANES)

    h1, up, metat, cnt = _post_call(
        h2, a3, b2, mod3, u2d, p["norm2_g"].reshape(1, d),
        w_in[:, n_sp:].astype(bf16), b_in[n_sp:].reshape(1, 2 * d),
        p["w_proj_ssm"].astype(bf16), p["w_proj_pool"].astype(bf16), p["w_out"].astype(bf16),
        wr_cat, br, seq, _row_tile(seq, 2 * tm), tm, n_grp, n_exp)

    tr = EXPERT_TILE_ROWS
    epg = n_exp // n_grp
    pairs = [(a, b) for a in range(epg) for b in range(a + 1, epg)]
    n_cls = n_grp * len(pairs)
    assert n_cls <= ROUTE_LANES
    cls_ea = jnp.asarray([gi * epg + a for gi in range(n_grp) for a, _ in pairs], jnp.int32)
    cls_eb = jnp.asarray([gi * epg + b for gi in range(n_grp) for _, b in pairs], jnp.int32)
    counts = cnt[0, 0:n_cls].astype(jnp.int32)
    padded = ((counts + tr - 1) // tr) * tr
    ends = jnp.cumsum(padded)
    offs = ends - padded
    ids = jnp.arange(n_cls, dtype=jnp.int32)
    cls = metat[0].astype(jnp.int32)
    off = jnp.sum(jnp.where(cls[None, :] == ids[:, None], offs[:, None], 0), axis=0)
    pos = off + metat[1].astype(jnp.int32)
    tms = _row_tile(seq, 2 * tm)
    pos3 = pos.reshape(t // tms, 1, tms)
    nt_max = t // tr + n_cls
    n_tiles = (ends[-1] // tr).astype(jnp.int32).reshape(1)
    tile_start = jnp.arange(nt_max, dtype=jnp.int32) * tr
    tile_cls = jnp.minimum(
        jnp.sum((ends[None, :] <= tile_start[:, None]).astype(jnp.int32), axis=1), n_cls - 1)

    xs = _scatter_call(up, pos3, nt_max * tr, tms, ROW_PANELS)
    ys = _experts_call(cls_ea[tile_cls], cls_eb[tile_cls], n_tiles, xs,
                       p["moe_w_gate"].astype(bf16), p["moe_w_up"].astype(bf16),
                       p["moe_w_down"].astype(bf16), tr)
    return _combine_call(pos.reshape(t // tm, 1, tm), h1, mod3, final_gain.reshape(1, d), ys,
                         seq, tm, final_norm)


def kernel(x, c, w_mod, b_mod, norm1_g, w_in, b_in, ssm_lam_re, ssm_lam_im, ssm_log_dt, ssm_b_re, ssm_b_im, ssm_c_re, ssm_c_im, ssm_d, ssm_w_glu, ssm_b_glu, pool_w, pool_scale, w_proj_ssm, w_proj_pool, w_out, norm2_g, router_coarse_w, router_coarse_b, router_fine_w, router_fine_b, moe_w_gate, moe_w_up, moe_w_down, norm_f_g):
    bsz, seq, d = x.shape
    depth = w_mod.shape[0]
    per_layer = dict(
        norm1_g=norm1_g, w_in=w_in, b_in=b_in, ssm_lam_re=ssm_lam_re, ssm_lam_im=ssm_lam_im,
        ssm_log_dt=ssm_log_dt, ssm_b_re=ssm_b_re, ssm_b_im=ssm_b_im, ssm_c_re=ssm_c_re,
        ssm_c_im=ssm_c_im, ssm_d=ssm_d, ssm_w_glu=ssm_w_glu, ssm_b_glu=ssm_b_glu, pool_w=pool_w,
        pool_scale=pool_scale, w_proj_ssm=w_proj_ssm, w_proj_pool=w_proj_pool, w_out=w_out,
        norm2_g=norm2_g, router_coarse_w=router_coarse_w, router_coarse_b=router_coarse_b,
        router_fine_w=router_fine_w, router_fine_b=router_fine_b, moe_w_gate=moe_w_gate,
        moe_w_up=moe_w_up, moe_w_down=moe_w_down)
    h2 = x.reshape(bsz * seq, d)
    for l in range(depth):
        p = {k: v[l] for k, v in per_layer.items()}
        mod3 = _mod_call(c, w_mod[l], b_mod[l]).reshape(bsz, 1, N_MOD * d)
        h2 = _layer(h2, mod3, p, bsz, seq, norm_f_g, final_norm=(l == depth - 1))
    return h2.reshape(bsz, seq, d)
```

```python
import functools
import math

import jax
import jax.numpy as jnp
from jax import lax
from jax.experimental import pallas as pl
from jax.experimental.pallas import tpu as pltpu

EPS = 1e-6
POOL_WINDOWS = (2, 4, 8, 16)
TOP_K_FINE = 2
N_MOD = 6
LANES = 128
SSM_SLABS = 8
ROUTE_LANES = 128
ROW_DMA_UNROLL = 16
ROW_PANELS = 5
EXPERT_TILE_ROWS = 256
VMEM_LIMIT = 56 * 1024 * 1024
NEG = -1e30

f32 = jnp.float32
bf16 = jnp.bfloat16


def _cparams(sem):
    return pltpu.CompilerParams(dimension_semantics=sem, vmem_limit_bytes=VMEM_LIMIT)


def _dot(a, b):
    return jnp.dot(a, b, preferred_element_type=f32)


def _pack_bf16_pairs(v):
    half = v.shape[1] // 2
    hi = lax.bitcast_convert_type(v[:, 0:half].astype(bf16).astype(f32), jnp.uint32)
    lo = lax.bitcast_convert_type(v[:, half:].astype(bf16).astype(f32), jnp.uint32)
    return hi | (lo >> 16)


def _unpack_bf16_pairs(w):
    hi = lax.bitcast_convert_type(w & jnp.uint32(0xFFFF0000), f32)
    lo = lax.bitcast_convert_type(w << 16, f32)
    return jnp.concatenate([hi, lo], axis=1)


def _sigmoid(v):
    return 0.5 * jnp.tanh(0.5 * v) + 0.5


def _dot_split_rows(a, b):
    h = a.shape[0] // 2
    return jnp.concatenate([_dot(a[:h], b), _dot(a[h:], b)], axis=0)


def _store_panels(ref, val, rows, row0=0):
    p = val.shape[1] // LANES
    for j in range(p):
        ref[pl.ds(row0 * p + j, rows, stride=p), :] = val[:, j * LANES:(j + 1) * LANES]


def _load_panels(ref, rows, p):
    return jnp.concatenate([ref[pl.ds(j, rows, stride=p), :] for j in range(p)], axis=1)


def _split_bf16(v):
    hi = v.astype(bf16)
    return hi, (v - hi.astype(f32)).astype(bf16)


def _mod_kernel(c_ref, w_ref, b_ref, o_ref):
    c = c_ref[...]
    a_hi, a_lo = _split_bf16(c * _sigmoid(c))
    w_hi, w_lo = _split_bf16(w_ref[...])
    o_ref[...] = _dot(a_hi, w_hi) + (_dot(a_lo, w_hi) + _dot(a_hi, w_lo)) + b_ref[...]


def _mod_call(c, w_mod, b_mod):
    bsz, d = c.shape
    n = w_mod.shape[1]
    tn = d
    return pl.pallas_call(
        _mod_kernel,
        grid=(n // tn,),
        in_specs=[pl.BlockSpec((bsz, d), lambda j: (0, 0)),
                  pl.BlockSpec((d, tn), lambda j: (0, j)),
                  pl.BlockSpec((1, tn), lambda j: (0, j))],
        out_specs=pl.BlockSpec((bsz, tn), lambda j: (0, j)),
        out_shape=jax.ShapeDtypeStruct((bsz, n), f32),
        compiler_params=_cparams(("parallel",)),
        name="mod",
    )(c, w_mod, b_mod.reshape(1, n))


def _ssm_prep_kernel(lr_ref, li_ref, ldt_ref, btr_ref, bti_ref, ctr_ref, cti_ref,
                     pw_ref, bbd_ref, cbd_ref, *, n_pow, n_grp):
    n_h, n_p = btr_ref.shape[1], btr_ref.shape[2]
    gp = n_grp * n_p
    lr = lr_ref[...]
    li = li_ref[...]
    dt = jnp.exp(ldt_ref[...])
    for k in range(1, n_pow + 1):
        mag = jnp.exp(lr * dt * float(k))
        ang = li * dt * float(k)
        pw_ref[k - 1, 0] = mag * jnp.cos(ang)
        pw_ref[k - 1, 1] = mag * jnp.sin(ang)
    lb_re = pw_ref[0, 0]
    lb_im = pw_ref[0, 1]
    den = lr * lr + li * li
    nr = lb_re - 1.0
    f_re = (nr * lr + lb_im * li) / den
    f_im = (lb_im * lr - nr * li) / den
    bbd_ref[...] = jnp.zeros_like(bbd_ref)
    cbd_ref[...] = jnp.zeros_like(cbd_ref)
    for g in range(n_grp):
        fr, fi = f_re[g:g + 1, :], f_im[g:g + 1, :]
        hs, ps = pl.ds(g * n_h, n_h), pl.ds(g * n_p, n_p)
        bbd_ref[hs, ps] = fr * btr_ref[g] - fi * bti_ref[g]
        bbd_ref[hs, pl.ds(gp + g * n_p, n_p)] = fr * bti_ref[g] + fi * btr_ref[g]
        cbd_ref[ps, hs] = ctr_ref[g]
        cbd_ref[pl.ds(gp + g * n_p, n_p), hs] = -cti_ref[g]


def _ssm_prep_call(lam_re, lam_im, log_dt, bt_re, bt_im, ct_re, ct_im, n_pow):
    g, p = lam_re.shape
    h = bt_re.shape[1]
    return pl.pallas_call(
        functools.partial(_ssm_prep_kernel, n_pow=n_pow, n_grp=g),
        out_shape=(jax.ShapeDtypeStruct((n_pow, 2, g, p), f32),
                   jax.ShapeDtypeStruct((g * h, 2 * g * p), f32),
                   jax.ShapeDtypeStruct((2 * g * p, g * h), f32)),
        name="ssm_prep",
    )(lam_re, lam_im, log_dt.reshape(g, 1), bt_re, bt_im, ct_re, ct_im)


def _modulated_norm(x, gain, shift, scale):
    ms = jnp.mean(x * x, axis=-1, keepdims=True)
    return x * lax.rsqrt(ms + EPS) * (gain * (1.0 + scale)) + shift


def _cmul_add(ar, ai, xr, xi, br, bi):
    return ar * xr - ai * xi + br, ar * xi + ai * xr + bi


def _mixers_kernel(x_ref, mod_ref, g1_ref, wsp_ref, bsp_ref,
                   bbd_ref, cbd_ref, pw_ref, dsk_ref, wglu_ref, bglu_ref, pw_pool_ref, psc_ref,
                   a_ref, b_ref, u_ref, us_ref, up_ref, e_ref, s_ref, y_ref,
                   *, seq, d, d_ssm, d_pool, n_state, slabs, ts):
    nc = seq // slabs
    nl = d_ssm // LANES
    ns = n_state

    for r0 in range(0, seq, ts):
        rows = pl.ds(r0, ts)
        u = _modulated_norm(x_ref[rows, :], g1_ref[...], mod_ref[:, 0:d], mod_ref[:, d:2 * d])
        u = u.astype(bf16)
        u_ref[rows, :] = u
        r = _dot(u, wsp_ref[...]) + bsp_ref[...]
        for j in range(nl):
            us_ref[j, rows, :] = r[:, j * LANES:(j + 1) * LANES]
        up_ref[rows, :] = r[:, d_ssm:]

    dsk = dsk_ref[...]
    l1r = pw_ref[0:1, 0:ns]
    l1i = pw_ref[0:1, ns:2 * ns]

    xr = xi = None
    for t in range(slabs):
        ut = jnp.concatenate([us_ref[j, pl.ds(t, nc, stride=slabs), :] for j in range(nl)], axis=1)
        bu = _dot(ut.astype(bf16), bbd_ref[...])
        br, bi = bu[:, 0:ns], bu[:, ns:2 * ns]
        if t == 0:
            xr, xi = br, bi
        else:
            xr, xi = _cmul_add(l1r, l1i, xr, xi, br, bi)
        xc = jnp.concatenate([xr, xi], axis=1).astype(bf16)
        y_ref[t] = _dot_split_rows(xc, cbd_ref[...]) + dsk * ut
    e_ref[:, 0:ns] = xr
    e_ref[:, ns:2 * ns] = xi

    lLr = pw_ref[slabs - 1:slabs, 0:ns]
    lLi = pw_ref[slabs - 1:slabs, ns:2 * ns]

    def chunk_step(c, carry):
        sr, si = carry
        s_ref[pl.ds(c, 1), 0:ns] = sr
        s_ref[pl.ds(c, 1), ns:2 * ns] = si
        er = e_ref[pl.ds(c, 1), 0:ns]
        ei = e_ref[pl.ds(c, 1), ns:2 * ns]
        return _cmul_add(lLr, lLi, sr, si, er, ei)

    zero = jnp.zeros((1, ns), f32)
    lax.fori_loop(0, nc, chunk_step, (zero, zero))

    sr = s_ref[:, 0:ns]
    si = s_ref[:, ns:2 * ns]
    for t in range(slabs):
        pr = pw_ref[t:t + 1, 0:ns]
        pi = pw_ref[t:t + 1, ns:2 * ns]
        zr = pr * sr - pi * si
        zi = pr * si + pi * sr
        zc = jnp.concatenate([zr, zi], axis=1).astype(bf16)
        y = y_ref[t] + _dot_split_rows(zc, cbd_ref[...])
        y = jax.nn.gelu(y)
        z = y * _sigmoid(_dot(y.astype(bf16), wglu_ref[...]) + bglu_ref[...])
        for j in range(nl):
            a_ref[j, pl.ds(t, nc, stride=slabs), :] = z[:, j * LANES:(j + 1) * LANES]

    gc = d_pool // len(POOL_WINDOWS)
    row = lax.broadcasted_iota(jnp.int32, (seq, gc), 0)
    for gi, w in enumerate(POOL_WINDOWS):
        lo = gi * gc
        v = up_ref[:, lo:lo + gc]
        acc = v
        span = 1
        while span < w:
            acc = acc + jnp.where(row >= span, pltpu.roll(acc, span, axis=0), 0.0)
            span *= 2
        cnt = jnp.minimum(row + 1, w).astype(f32)
        m = acc / cnt - v
        yg = _dot(m.astype(bf16), pw_pool_ref[gi])
        b_ref[:, lo:lo + gc] = (yg * psc_ref[:, lo:lo + gc]).astype(bf16)


def _mixers_call(x2, mod3, g1, w_sp, b_sp, bbd, cbd, pw, dsk, wglu, bglu, pool_w, pool_scale,
                 bsz, seq, ts):
    t, d = x2.shape
    d_ssm = dsk.shape[1]
    d_pool = pool_scale.shape[1]
    nl = d_ssm // LANES
    ns2 = bbd.shape[1]
    slabs = SSM_SLABS
    nc = seq // slabs
    kern = functools.partial(_mixers_kernel, seq=seq, d=d, d_ssm=d_ssm, d_pool=d_pool,
                             n_state=ns2 // 2, slabs=slabs, ts=ts)
    const2 = lambda b: (0, 0)
    return pl.pallas_call(
        kern,
        grid=(bsz,),
        in_specs=[pl.BlockSpec((seq, d), lambda b: (b, 0)),
                  pl.BlockSpec((None, 1, mod3.shape[2]), lambda b: (b, 0, 0)),
                  pl.BlockSpec((1, d), const2),
                  pl.BlockSpec(w_sp.shape, const2),
                  pl.BlockSpec(b_sp.shape, const2),
                  pl.BlockSpec(bbd.shape, const2),
                  pl.BlockSpec(cbd.shape, const2),
                  pl.BlockSpec(pw.shape, const2),
                  pl.BlockSpec(dsk.shape, const2),
                  pl.BlockSpec(wglu.shape, const2),
                  pl.BlockSpec(bglu.shape, const2),
                  pl.BlockSpec(pool_w.shape, lambda b: (0, 0, 0)),
                  pl.BlockSpec(pool_scale.shape, const2)],
        out_specs=[pl.BlockSpec((nl, seq, LANES), lambda b: (0, b, 0)),
                   pl.BlockSpec((seq, d_pool), lambda b: (b, 0)),
                   pl.BlockSpec((seq, d), lambda b: (b, 0))],
        out_shape=(jax.ShapeDtypeStruct((nl, t, LANES), f32),
                   jax.ShapeDtypeStruct((t, d_pool), bf16),
                   jax.ShapeDtypeStruct((t, d), bf16)),
        scratch_shapes=[pltpu.VMEM((nl, seq, LANES), f32),
                        pltpu.VMEM((seq, d_pool), f32),
                        pltpu.VMEM((nc, ns2), f32),
                        pltpu.VMEM((nc, ns2), f32),
                        pltpu.VMEM((slabs, nc, d_ssm), f32)],
        compiler_params=_cparams(("parallel",)),
        name="mixers",
    )(x2, mod3, g1, w_sp, b_sp, bbd, cbd, pw, dsk, wglu, bglu, pool_w, pool_scale)


def _post_kernel(x_ref, a_ref, b_ref, mod_ref, u_ref, g2_ref, wg_ref, bg_ref, wps_ref, wpp_ref,
                 wout_ref, wr_ref, br_ref, tri_ref, mg_ref, mu_ref, md_ref,
                 h_ref, up_ref, metat_ref, cnt_ref, mgo_ref, muo_ref, mdo_ref, carry_ref,
                 *, d, d_ssm, n_grp, n_exp, tm, ts):
    i = pl.program_id(0)

    @pl.when(i == 0)
    def _():
        carry_ref[...] = jnp.zeros_like(carry_ref)

    mgo_ref[...] = mg_ref[...].astype(bf16)
    muo_ref[...] = mu_ref[...].astype(bf16)
    mdo_ref[...] = md_ref[...].astype(bf16)

    for r0 in range(0, tm, ts):
        _post_rows(x_ref, a_ref, b_ref, mod_ref, u_ref, g2_ref, wg_ref, bg_ref, wps_ref, wpp_ref,
                   wout_ref, wr_ref, br_ref, tri_ref, h_ref, up_ref, metat_ref, carry_ref,
                   d=d, d_ssm=d_ssm, n_grp=n_grp, n_exp=n_exp, r0=r0, tm=ts)
    cnt_ref[...] = carry_ref[...]


def _post_rows(x_ref, a_ref, b_ref, mod_ref, u_ref, g2_ref, wg_ref, bg_ref, wps_ref, wpp_ref,
               wout_ref, wr_ref, br_ref, tri_ref, h_ref, up_ref, metat_ref, carry_ref,
               *, d, d_ssm, n_grp, n_exp, r0, tm):
    rows = pl.ds(r0, tm)
    x = x_ref[rows, :]
    gate1 = mod_ref[:, 2 * d:3 * d]
    shift2, scale2 = mod_ref[:, 3 * d:4 * d], mod_ref[:, 4 * d:5 * d]
    gates = _dot(u_ref[rows, :], wg_ref[...]) + bg_ref[...]
    a = jnp.concatenate([a_ref[j, rows, :] for j in range(d_ssm // LANES)], axis=1).astype(bf16)
    b = b_ref[rows, :]
    merged = (_sigmoid(gates[:, 0:d]) * _dot(a, wps_ref[...])
              + _sigmoid(gates[:, d:2 * d]) * _dot(b, wpp_ref[...]))
    h = x + gate1 * _dot(merged.astype(bf16), wout_ref[...])
    h_ref[rows, :] = h

    u2 = _modulated_norm(h, g2_ref[...], shift2, scale2)

    half = d // 2
    words = _pack_bf16_pairs(u2)

    u_hi = u2.astype(bf16)
    u_lo = (u2 - u_hi.astype(f32)).astype(bf16)
    r1 = _dot_split_rows(u_hi, wr_ref[...])
    r2 = _dot_split_rows(u_lo, wr_ref[...])
    lg = r1 + pltpu.roll(r1, ROUTE_LANES // 2, axis=1) + r2 + br_ref[...]

    lane = lax.broadcasted_iota(jnp.int32, (tm, ROUTE_LANES), 1).astype(f32)
    big = float(ROUTE_LANES)
    epg = float(n_exp // n_grp)
    is_c = lane < n_grp
    cl = jnp.where(is_c, lg, NEG)
    cmax = jnp.max(cl, axis=-1, keepdims=True)
    grp = jnp.min(jnp.where(cl == cmax, lane, big), axis=-1, keepdims=True)
    p_grp = 1.0 / jnp.sum(jnp.where(is_c, jnp.exp(cl - cmax), 0.0), axis=-1, keepdims=True)

    f_lo = n_grp + grp * epg
    fl = jnp.where((lane >= f_lo) & (lane < f_lo + epg), lg, NEG)
    f1 = jnp.max(fl, axis=-1, keepdims=True)
    i1 = jnp.min(jnp.where(fl == f1, lane, big), axis=-1, keepdims=True)
    fl2 = jnp.where(lane == i1, NEG, fl)
    f2 = jnp.max(fl2, axis=-1, keepdims=True)
    i2 = jnp.min(jnp.where(fl2 == f2, lane, big), axis=-1, keepdims=True)
    t2 = jnp.exp(f2 - f1)
    w0 = p_grp / (1.0 + t2)
    w1 = p_grp * t2 / (1.0 + t2)

    j0 = i1 - f_lo
    j1 = i2 - f_lo
    first = j0 < j1
    ja = jnp.minimum(j0, j1)
    jb = jnp.maximum(j0, j1)
    wa = jnp.where(first, w0, w1)
    wb = jnp.where(first, w1, w0)
    n_pair = epg * (epg - 1.0) * 0.5
    cls = grp * n_pair + ja * (2.0 * epg - ja - 1.0) * 0.5 + (jb - ja - 1.0)

    sel = lane == cls
    oh = jnp.where(sel, 1.0, 0.0)
    before = carry_ref[...] + _dot_split_rows(tri_ref[...], oh.astype(bf16))
    rank = jnp.sum(jnp.where(sel, before, 0.0), axis=-1, keepdims=True)
    carry_ref[...] = carry_ref[...] + jnp.sum(oh, axis=0, keepdims=True)

    meta = jnp.zeros((tm, ROUTE_LANES), f32)
    for k, val in enumerate((cls, rank, wa, wb)):
        meta = jnp.where(lane == float(k), val, meta)

    wts = jnp.where(lane == 0.0, wa, jnp.where(lane == 1.0, wb, 0.0))
    assert half + LANES == ROW_PANELS * LANES
    row = jnp.concatenate([words, lax.bitcast_convert_type(wts, jnp.uint32)], axis=1)
    _store_panels(up_ref, row, tm, r0)
    metat_ref[:, rows] = meta.T[0:8, :]


def _slice_per_step(w, steps):
    e, r, c = w.shape
    k = max(1, steps // e)
    assert (e * k) % steps == 0 and r % k == 0
    return w.reshape(e * k, r // k, c), (e * k) // steps


def _post_call(x2, a3, b2, mod3, u2d, g2, wg, bg, wps, wpp, wout, wr, br, moe_w, seq, tm, ts,
               n_grp, n_exp):
    t, d = x2.shape
    d_ssm = wps.shape[0]
    nl = a3.shape[0]
    per = seq // tm
    tri = jnp.tril(jnp.ones((ts, ts), bf16), -1)
    moe_v = [_slice_per_step(w, t // tm) for w in moe_w]
    moe_specs = [pl.BlockSpec((n,) + v.shape[1:], lambda i: (i, 0, 0)) for v, n in moe_v]
    kern = functools.partial(_post_kernel, d=d, d_ssm=d_ssm, n_grp=n_grp, n_exp=n_exp,
                             tm=tm, ts=ts)
    const2 = lambda i: (0, 0)
    wspec = lambda w: pl.BlockSpec(w.shape, const2, pipeline_mode=pl.Buffered(1))
    return pl.pallas_call(
        kern,
        grid=(t // tm,),
        in_specs=[pl.BlockSpec((tm, d), lambda i: (i, 0)),
                  pl.BlockSpec((nl, tm, LANES), lambda i: (0, i, 0)),
                  pl.BlockSpec((tm, b2.shape[1]), lambda i: (i, 0)),
                  pl.BlockSpec((None, 1, mod3.shape[2]), lambda i: (i // per, 0, 0)),
                  pl.BlockSpec((tm, d), lambda i: (i, 0)),
                  pl.BlockSpec((1, d), const2),
                  wspec(wg),
                  pl.BlockSpec(bg.shape, const2),
                  wspec(wps),
                  wspec(wpp),
                  wspec(wout),
                  wspec(wr),
                  pl.BlockSpec(br.shape, const2),
                  wspec(tri)] + moe_specs,
        out_specs=[pl.BlockSpec((tm, d), lambda i: (i, 0)),
                   pl.BlockSpec((tm * ROW_PANELS, LANES), lambda i: (i, 0)),
                   pl.BlockSpec((8, tm), lambda i: (0, i)),
                   pl.BlockSpec((1, ROUTE_LANES), const2)] + moe_specs,
        out_shape=(jax.ShapeDtypeStruct((t, d), f32),
                   jax.ShapeDtypeStruct((t * ROW_PANELS, LANES), jnp.uint32),
                   jax.ShapeDtypeStruct((8, t), f32),
                   jax.ShapeDtypeStruct((1, ROUTE_LANES), f32))
        + tuple(jax.ShapeDtypeStruct(v.shape, bf16) for v, _ in moe_v),
        scratch_shapes=[pltpu.VMEM((1, ROUTE_LANES), f32)],
        compiler_params=_cparams(("arbitrary",)),
        name="post",
    )(x2, a3, b2, mod3, u2d, g2, wg, bg, wps, wpp, wout, wr, br, tri, *[v for v, _ in moe_v])


def _row_copy(src_ref, dst_ref, src_row, dst_row, p, sem):
    src = src_ref.at[pl.ds(pl.multiple_of(src_row * p, p), p)]
    dst = dst_ref.at[pl.ds(pl.multiple_of(dst_row * p, p), p)]
    return pltpu.make_async_copy(src, dst, sem)


def _wait_rows(src_ref, dst_ref, n_rows, p, sem):
    pltpu.make_async_copy(src_ref.at[pl.ds(0, n_rows * p)], dst_ref.at[pl.ds(0, n_rows * p)],
                          sem).wait()


def _scatter_kernel(pos_ref, up_ref, xs_ref, sem, *, tm, p):
    def issue(g, c):
        r0 = g * ROW_DMA_UNROLL
        idx = [pos_ref[0, 0, r0 + u] for u in range(ROW_DMA_UNROLL)]
        for u in range(ROW_DMA_UNROLL):
            _row_copy(up_ref, xs_ref, r0 + u, idx[u], p, sem).start(priority=u % 2)
        return c

    lax.fori_loop(0, tm // ROW_DMA_UNROLL, issue, 0)
    _wait_rows(up_ref, xs_ref, tm, p, sem)


def _scatter_call(up, pos3, n_rows, tm, p):
    n_steps = up.shape[0] // (tm * p)
    return pl.pallas_call(
        functools.partial(_scatter_kernel, tm=tm, p=p),
        grid=(n_steps,),
        in_specs=[pl.BlockSpec((1, 1, tm), lambda i: (i, 0, 0), memory_space=pltpu.SMEM),
                  pl.BlockSpec((tm * p, LANES), lambda i: (i, 0))],
        out_specs=pl.BlockSpec(memory_space=pl.ANY),
        out_shape=jax.ShapeDtypeStruct((n_rows * p, LANES), up.dtype),
        scratch_shapes=[pltpu.SemaphoreType.DMA(())],
        compiler_params=_cparams(("arbitrary",)),
        name="scatter_rows",
    )(pos3, up)


def _experts_kernel(ea_ref, eb_ref, nt_ref, xs_ref, wg_a_ref, wg_b_ref, wu_a_ref, wu_b_ref,
                    wd_a_ref, wd_b_ref, ys_ref, *, d, tr):
    i = pl.program_id(0)

    @pl.when(i < nt_ref[0])
    def _():
        px = d // 2 // LANES
        w = jnp.concatenate([xs_ref[pl.ds(j, tr, stride=ROW_PANELS), :] for j in range(px)],
                            axis=1)
        wts = lax.bitcast_convert_type(xs_ref[pl.ds(px, tr, stride=ROW_PANELS), :], f32)
        x = _unpack_bf16_pairs(w).astype(bf16)

        def ffn(wg_ref, wu_ref, wd_ref, wt):
            h1 = _dot(x, wg_ref[...])
            act = h1 * _sigmoid(h1) * _dot(x, wu_ref[...])
            return wt * _dot(act.astype(bf16), wd_ref[...])

        y = (ffn(wg_a_ref, wu_a_ref, wd_a_ref, wts[:, 0:1])
             + ffn(wg_b_ref, wu_b_ref, wd_b_ref, wts[:, 1:2]))
        _store_panels(ys_ref, _pack_bf16_pairs(y), tr)


def _experts_call(tile_ea, tile_eb, n_tiles, xs, wg, wu, wd, tr):
    n_exp, d, d_exp = wg.shape
    n_rows = xs.shape[0] // ROW_PANELS
    nt_max = n_rows // tr
    py = d // 2 // LANES

    def row_map(i, ea, eb, nt):
        return (jnp.minimum(i, nt[0] - 1), 0)

    def wa_map(i, ea, eb, nt):
        return (ea[jnp.minimum(i, nt[0] - 1)], 0, 0)

    def wb_map(i, ea, eb, nt):
        return (eb[jnp.minimum(i, nt[0] - 1)], 0, 0)

    wspec = pl.BlockSpec
    grid_spec = pltpu.PrefetchScalarGridSpec(
        num_scalar_prefetch=3,
        grid=(nt_max,),
        in_specs=[pl.BlockSpec((tr * ROW_PANELS, LANES), row_map),
                  wspec((None, d, d_exp), wa_map),
                  wspec((None, d, d_exp), wb_map),
                  wspec((None, d, d_exp), wa_map),
                  wspec((None, d, d_exp), wb_map),
                  wspec((None, d_exp, d), wa_map),
                  wspec((None, d_exp, d), wb_map)],
        out_specs=pl.BlockSpec((tr * py, LANES), row_map),
    )
    return pl.pallas_call(
        functools.partial(_experts_kernel, d=d, tr=tr),
        grid_spec=grid_spec,
        out_shape=jax.ShapeDtypeStruct((n_rows * py, LANES), jnp.uint32),
        compiler_params=_cparams(("arbitrary",)),
        name="experts",
    )(tile_ea, tile_eb, n_tiles, xs, wg, wg, wu, wu, wd, wd)


def _combine_kernel(pos_ref, posn_ref, h_ref, mod_ref, gf_ref, ys_ref, o_ref,
                    g_ref, sem, *, d, tm, n_steps, final_norm):
    i = pl.program_id(0)
    slot = i % 2
    py = d // 2 // LANES

    def issue(p_ref, s):
        def body(g, c):
            r0 = g * ROW_DMA_UNROLL
            idx = [p_ref[0, 0, r0 + u] for u in range(ROW_DMA_UNROLL)]
            for u in range(ROW_DMA_UNROLL):
                _row_copy(ys_ref, g_ref.at[s], idx[u], r0 + u, py,
                          sem.at[s]).start(priority=u % 2)
            return c

        lax.fori_loop(0, tm // ROW_DMA_UNROLL, body, 0)

    pl.when(i == 0)(lambda: issue(pos_ref, 0))
    pl.when(i + 1 < n_steps)(lambda: issue(posn_ref, 1 - slot))
    _wait_rows(ys_ref, g_ref.at[slot], tm, py, sem.at[slot])

    gate2 = mod_ref[:, 5 * d:6 * d]
    h = h_ref[...] + gate2 * _unpack_bf16_pairs(_load_panels(g_ref.at[slot], tm, py))
    if final_norm:
        ms = jnp.mean(h * h, axis=-1, keepdims=True)
        h = h * lax.rsqrt(ms + EPS) * gf_ref[...]
    o_ref[...] = h


def _combine_call(pos3, h1, mod3, gf, ys, seq, tm, final_norm):
    t, d = h1.shape
    per = seq // tm
    n_steps = t // tm
    pos_spec = lambda f: pl.BlockSpec((1, 1, tm), f, memory_space=pltpu.SMEM)
    return pl.pallas_call(
        functools.partial(_combine_kernel, d=d, tm=tm, n_steps=n_steps, final_norm=final_norm),
        grid=(n_steps,),
        in_specs=[pos_spec(lambda i: (i, 0, 0)),
                  pos_spec(lambda i: (jnp.minimum(i + 1, n_steps - 1), 0, 0)),
                  pl.BlockSpec((tm, d), lambda i: (i, 0)),
                  pl.BlockSpec((None, 1, mod3.shape[2]), lambda i: (i // per, 0, 0)),
                  pl.BlockSpec((1, d), lambda i: (0, 0)),
                  pl.BlockSpec(memory_space=pl.ANY)],
        out_specs=pl.BlockSpec((tm, d), lambda i: (i, 0)),
        out_shape=jax.ShapeDtypeStruct((t, d), f32),
        scratch_shapes=[pltpu.VMEM((2, tm * (d // 2 // LANES), LANES), jnp.uint32),
                        pltpu.SemaphoreType.DMA((2,))],
        compiler_params=_cparams(("arbitrary",)),
        name="combine",
    )(pos3, pos3, h1, mod3, gf, ys)


def _row_tile(seq, target):
    tm = min(target, seq)
    assert seq % tm == 0 and tm % 8 == 0
    return tm


def _layer(h2, mod3, p, bsz, seq, final_gain, final_norm):
    t, d = h2.shape
    g, n_p, n_h = p["ssm_b_re"].shape
    d_ssm = g * n_h
    d_pool = p["pool_scale"].shape[-1]
    n_grp = p["router_coarse_w"].shape[-1]
    n_exp = p["router_fine_w"].shape[-1]
    assert n_grp + n_exp <= ROUTE_LANES // 2 and seq % SSM_SLABS == 0
    tm = _row_tile(seq, 512)

    swap = lambda v: jnp.swapaxes(v, 1, 2)
    pw, bbd, cbd = _ssm_prep_call(
        p["ssm_lam_re"], p["ssm_lam_im"], p["ssm_log_dt"], swap(p["ssm_b_re"]),
        swap(p["ssm_b_im"]), swap(p["ssm_c_re"]), swap(p["ssm_c_im"]), SSM_SLABS)
    bbd, cbd = bbd.astype(bf16), cbd.astype(bf16)
    pw2 = pw.reshape(SSM_SLABS, 2 * g * n_p)

    w_in, b_in = p["w_in"], p["b_in"]
    n_sp = d_ssm + d_pool
    a3, b2, u2d = _mixers_call(h2, mod3, p["norm1_g"].reshape(1, d), w_in[:, :n_sp].astype(bf16),
                          b_in[:n_sp].reshape(1, n_sp), bbd, cbd, pw2,
                          p["ssm_d"].reshape(1, d_ssm),
                          p["ssm_w_glu"].astype(bf16), p["ssm_b_glu"].reshape(1, d_ssm),
                          p["pool_w"].astype(bf16), p["pool_scale"].reshape(1, d_pool),
                          bsz, seq, tm)

    wr = jnp.concatenate([p["router_coarse_w"], p["router_fine_w"]], axis=1)
    wr_hi = wr.astype(bf16)
    wr_lo = (wr - wr_hi.astype(f32)).astype(bf16)
    padc = ROUTE_LANES // 2 - wr.shape[1]
    wr_cat = jnp.concatenate([jnp.pad(wr_hi, ((0, 0), (0, padc))),
                              jnp.pad(wr_lo, ((0, 0), (0, padc)))], axis=1)
    br = jnp.pad(jnp.concatenate([p["router_coarse_b"], p["router_fine_b"]]),
                 (0, ROUTE_LANES - wr.shape[1])).reshape(1, ROUTE_LANES)

    moe_w = (p["moe_w_gate"], p["moe_w_up"], p["moe_w_down"])
    h1, up, metat, cnt, *moe_bf = _post_call(
        h2, a3, b2, mod3, u2d, p["norm2_g"].reshape(1, d),
        w_in[:, n_sp:].astype(bf16), b_in[n_sp:].reshape(1, 2 * d),
        p["w_proj_ssm"].astype(bf16), p["w_proj_pool"].astype(bf16), p["w_out"].astype(bf16),
        wr_cat, br, moe_w, seq, _row_tile(seq, 2 * tm), tm, n_grp, n_exp)
    moe_bf = [v.reshape(w.shape) for v, w in zip(moe_bf, moe_w)]

    tr = EXPERT_TILE_ROWS
    epg = n_exp // n_grp
    pairs = [(a, b) for a in range(epg) for b in range(a + 1, epg)]
    n_cls = n_grp * len(pairs)
    assert n_cls <= ROUTE_LANES
    cls_ea = jnp.asarray([gi * epg + a for gi in range(n_grp) for a, _ in pairs], jnp.int32)
    cls_eb = jnp.asarray([gi * epg + b for gi in range(n_grp) for _, b in pairs], jnp.int32)
    counts = cnt[0, 0:n_cls].astype(jnp.int32)
    padded = ((counts + tr - 1) // tr) * tr
    ends = jnp.cumsum(padded)
    offs = ends - padded
    ids = jnp.arange(n_cls, dtype=jnp.int32)
    cls = metat[0].astype(jnp.int32)
    off = jnp.sum(jnp.where(cls[None, :] == ids[:, None], offs[:, None], 0), axis=0)
    pos = off + metat[1].astype(jnp.int32)
    tms = _row_tile(seq, 2 * tm)
    pos3 = pos.reshape(t // tms, 1, tms)
    nt_max = t // tr + n_cls
    n_tiles = (ends[-1] // tr).astype(jnp.int32).reshape(1)
    tile_start = jnp.arange(nt_max, dtype=jnp.int32) * tr
    tile_cls = jnp.minimum(
        jnp.sum((ends[None, :] <= tile_start[:, None]).astype(jnp.int32), axis=1), n_cls - 1)

    xs = _scatter_call(up, pos3, nt_max * tr, tms, ROW_PANELS)
    ys = _experts_call(cls_ea[tile_cls], cls_eb[tile_cls], n_tiles, xs, *moe_bf, tr)
    return _combine_call(pos.reshape(t // tm, 1, tm), h1, mod3, final_gain.reshape(1, d), ys,
                         seq, tm, final_norm)


def kernel(x, c, w_mod, b_mod, norm1_g, w_in, b_in, ssm_lam_re, ssm_lam_im, ssm_log_dt, ssm_b_re, ssm_b_im, ssm_c_re, ssm_c_im, ssm_d, ssm_w_glu, ssm_b_glu, pool_w, pool_scale, w_proj_ssm, w_proj_pool, w_out, norm2_g, router_coarse_w, router_coarse_b, router_fine_w, router_fine_b, moe_w_gate, moe_w_up, moe_w_down, norm_f_g):
    bsz, seq, d = x.shape
    depth = w_mod.shape[0]
    per_layer = dict(
        norm1_g=norm1_g, w_in=w_in, b_in=b_in, ssm_lam_re=ssm_lam_re, ssm_lam_im=ssm_lam_im,
        ssm_log_dt=ssm_log_dt, ssm_b_re=ssm_b_re, ssm_b_im=ssm_b_im, ssm_c_re=ssm_c_re,
        ssm_c_im=ssm_c_im, ssm_d=ssm_d, ssm_w_glu=ssm_w_glu, ssm_b_glu=ssm_b_glu, pool_w=pool_w,
        pool_scale=pool_scale, w_proj_ssm=w_proj_ssm, w_proj_pool=w_proj_pool, w_out=w_out,
        norm2_g=norm2_g, router_coarse_w=router_coarse_w, router_coarse_b=router_coarse_b,
        router_fine_w=router_fine_w, router_fine_b=router_fine_b, moe_w_gate=moe_w_gate,
        moe_w_up=moe_w_up, moe_w_down=moe_w_down)
    h2 = x.reshape(bsz * seq, d)
    for l in range(depth):
        p = {k: v[l] for k, v in per_layer.items()}
        mod3 = _mod_call(c, w_mod[l], b_mod[l]).reshape(bsz, 1, N_MOD * d)
        h2 = _layer(h2, mod3, p, bsz, seq, norm_f_g, final_norm=(l == depth - 1))
    return h2.reshape(bsz, seq, d)
```

```python
import functools
import math

import jax
import jax.numpy as jnp
from jax import lax
from jax.experimental import pallas as pl
from jax.experimental.pallas import tpu as pltpu

EPS = 1e-6
POOL_WINDOWS = (2, 4, 8, 16)
TOP_K_FINE = 2
N_MOD = 6
LANES = 128
SSM_SLABS = 8
ROUTE_LANES = 128
ROW_DMA_UNROLL = 16
ROW_PANELS = 5
EXPERT_TILE_ROWS = 256
VMEM_LIMIT = 56 * 1024 * 1024
NEG = -1e30

f32 = jnp.float32
bf16 = jnp.bfloat16


def _cparams(sem):
    return pltpu.CompilerParams(dimension_semantics=sem, vmem_limit_bytes=VMEM_LIMIT)


def _dot(a, b):
    return jnp.dot(a, b, preferred_element_type=f32)


def _pack_bf16_pairs(v):
    half = v.shape[1] // 2
    hi = lax.bitcast_convert_type(v[:, 0:half].astype(bf16).astype(f32), jnp.uint32)
    lo = lax.bitcast_convert_type(v[:, half:].astype(bf16).astype(f32), jnp.uint32)
    return hi | (lo >> 16)


def _unpack_bf16_pairs(w):
    hi = lax.bitcast_convert_type(w & jnp.uint32(0xFFFF0000), f32)
    lo = lax.bitcast_convert_type(w << 16, f32)
    return jnp.concatenate([hi, lo], axis=1)


def _sigmoid(v):
    return 0.5 * jnp.tanh(0.5 * v) + 0.5


def _dot_split_rows(a, b):
    h = a.shape[0] // 2
    return jnp.concatenate([_dot(a[:h], b), _dot(a[h:], b)], axis=0)


def _store_panels(ref, val, rows, row0=0):
    p = val.shape[1] // LANES
    for j in range(p):
        ref[pl.ds(row0 * p + j, rows, stride=p), :] = val[:, j * LANES:(j + 1) * LANES]


def _load_panels(ref, rows, p):
    return jnp.concatenate([ref[pl.ds(j, rows, stride=p), :] for j in range(p)], axis=1)


def _split_bf16(v):
    hi = v.astype(bf16)
    return hi, (v - hi.astype(f32)).astype(bf16)


def _mod_kernel(c_ref, w_ref, b_ref, o_ref):
    c = c_ref[...]
    a_hi, a_lo = _split_bf16(c * _sigmoid(c))
    w_hi, w_lo = _split_bf16(w_ref[...])
    o_ref[...] = _dot(a_hi, w_hi) + (_dot(a_lo, w_hi) + _dot(a_hi, w_lo)) + b_ref[...]


def _mod_call(c, w_mod, b_mod):
    bsz, d = c.shape
    n = w_mod.shape[1]
    tn = d
    return pl.pallas_call(
        _mod_kernel,
        grid=(n // tn,),
        in_specs=[pl.BlockSpec((bsz, d), lambda j: (0, 0)),
                  pl.BlockSpec((d, tn), lambda j: (0, j)),
                  pl.BlockSpec((1, tn), lambda j: (0, j))],
        out_specs=pl.BlockSpec((bsz, tn), lambda j: (0, j)),
        out_shape=jax.ShapeDtypeStruct((bsz, n), f32),
        compiler_params=_cparams(("parallel",)),
        name="mod",
    )(c, w_mod, b_mod.reshape(1, n))


def _ssm_prep_kernel(lr_ref, li_ref, ldt_ref, btr_ref, bti_ref, ctr_ref, cti_ref,
                     pw_ref, bbd_ref, cbd_ref, *, n_pow, n_grp):
    n_h, n_p = btr_ref.shape[1], btr_ref.shape[2]
    gp = n_grp * n_p
    lr = lr_ref[...]
    li = li_ref[...]
    dt = jnp.exp(ldt_ref[...])
    for k in range(1, n_pow + 1):
        mag = jnp.exp(lr * dt * float(k))
        ang = li * dt * float(k)
        pw_ref[k - 1, 0] = mag * jnp.cos(ang)
        pw_ref[k - 1, 1] = mag * jnp.sin(ang)
    lb_re = pw_ref[0, 0]
    lb_im = pw_ref[0, 1]
    den = lr * lr + li * li
    nr = lb_re - 1.0
    f_re = (nr * lr + lb_im * li) / den
    f_im = (lb_im * lr - nr * li) / den
    bbd_ref[...] = jnp.zeros_like(bbd_ref)
    cbd_ref[...] = jnp.zeros_like(cbd_ref)
    for g in range(n_grp):
        fr, fi = f_re[g:g + 1, :], f_im[g:g + 1, :]
        hs, ps = pl.ds(g * n_h, n_h), pl.ds(g * n_p, n_p)
        bbd_ref[hs, ps] = fr * btr_ref[g] - fi * bti_ref[g]
        bbd_ref[hs, pl.ds(gp + g * n_p, n_p)] = fr * bti_ref[g] + fi * btr_ref[g]
        cbd_ref[ps, hs] = ctr_ref[g]
        cbd_ref[pl.ds(gp + g * n_p, n_p), hs] = -cti_ref[g]


def _ssm_prep_call(lam_re, lam_im, log_dt, bt_re, bt_im, ct_re, ct_im, n_pow):
    g, p = lam_re.shape
    h = bt_re.shape[1]
    return pl.pallas_call(
        functools.partial(_ssm_prep_kernel, n_pow=n_pow, n_grp=g),
        out_shape=(jax.ShapeDtypeStruct((n_pow, 2, g, p), f32),
                   jax.ShapeDtypeStruct((g * h, 2 * g * p), f32),
                   jax.ShapeDtypeStruct((2 * g * p, g * h), f32)),
        name="ssm_prep",
    )(lam_re, lam_im, log_dt.reshape(g, 1), bt_re, bt_im, ct_re, ct_im)


def _modulated_norm(x, gain, shift, scale):
    ms = jnp.mean(x * x, axis=-1, keepdims=True)
    return x * lax.rsqrt(ms + EPS) * (gain * (1.0 + scale)) + shift


def _cmul_add(ar, ai, xr, xi, br, bi):
    return ar * xr - ai * xi + br, ar * xi + ai * xr + bi


def _mixers_kernel(x_ref, mod_ref, g1_ref, wsp_ref, bsp_ref,
                   bbd_ref, cbd_ref, pw_ref, dsk_ref, wglu_ref, bglu_ref, pw_pool_ref, psc_ref,
                   a_ref, b_ref, u_ref, us_ref, up_ref, e_ref, s_ref, y_ref,
                   *, seq, d, d_ssm, d_pool, n_state, slabs, ts):
    nc = seq // slabs
    nl = d_ssm // LANES
    ns = n_state

    for r0 in range(0, seq, ts):
        rows = pl.ds(r0, ts)
        u = _modulated_norm(x_ref[rows, :], g1_ref[...], mod_ref[:, 0:d], mod_ref[:, d:2 * d])
        u = u.astype(bf16)
        u_ref[rows, :] = u
        r = _dot(u, wsp_ref[...]) + bsp_ref[...]
        for j in range(nl):
            us_ref[j, rows, :] = r[:, j * LANES:(j + 1) * LANES]
        up_ref[rows, :] = r[:, d_ssm:]

    dsk = dsk_ref[...]
    l1r = pw_ref[0:1, 0:ns]
    l1i = pw_ref[0:1, ns:2 * ns]

    xr = xi = None
    for t in range(slabs):
        ut = jnp.concatenate([us_ref[j, pl.ds(t, nc, stride=slabs), :] for j in range(nl)], axis=1)
        bu = _dot(ut.astype(bf16), bbd_ref[...])
        br, bi = bu[:, 0:ns], bu[:, ns:2 * ns]
        if t == 0:
            xr, xi = br, bi
        else:
            xr, xi = _cmul_add(l1r, l1i, xr, xi, br, bi)
        xc = jnp.concatenate([xr, xi], axis=1).astype(bf16)
        y_ref[t] = _dot_split_rows(xc, cbd_ref[...]) + dsk * ut
    e_ref[:, 0:ns] = xr
    e_ref[:, ns:2 * ns] = xi

    lLr = pw_ref[slabs - 1:slabs, 0:ns]
    lLi = pw_ref[slabs - 1:slabs, ns:2 * ns]

    def chunk_step(c, carry):
        sr, si = carry
        s_ref[pl.ds(c, 1), 0:ns] = sr
        s_ref[pl.ds(c, 1), ns:2 * ns] = si
        er = e_ref[pl.ds(c, 1), 0:ns]
        ei = e_ref[pl.ds(c, 1), ns:2 * ns]
        return _cmul_add(lLr, lLi, sr, si, er, ei)

    zero = jnp.zeros((1, ns), f32)
    lax.fori_loop(0, nc, chunk_step, (zero, zero))

    sr = s_ref[:, 0:ns]
    si = s_ref[:, ns:2 * ns]
    for t in range(slabs):
        pr = pw_ref[t:t + 1, 0:ns]
        pi = pw_ref[t:t + 1, ns:2 * ns]
        zr = pr * sr - pi * si
        zi = pr * si + pi * sr
        zc = jnp.concatenate([zr, zi], axis=1).astype(bf16)
        y = y_ref[t] + _dot_split_rows(zc, cbd_ref[...])
        y = jax.nn.gelu(y)
        z = y * _sigmoid(_dot(y.astype(bf16), wglu_ref[...]) + bglu_ref[...])
        for j in range(nl):
            a_ref[j, pl.ds(t, nc, stride=slabs), :] = z[:, j * LANES:(j + 1) * LANES]

    gc = d_pool // len(POOL_WINDOWS)
    row = lax.broadcasted_iota(jnp.int32, (seq, gc), 0)
    for gi, w in enumerate(POOL_WINDOWS):
        lo = gi * gc
        v = up_ref[:, lo:lo + gc]
        acc = v
        span = 1
        while span < w:
            acc = acc + jnp.where(row >= span, pltpu.roll(acc, span, axis=0), 0.0)
            span *= 2
        cnt = jnp.minimum(row + 1, w).astype(f32)
        m = acc / cnt - v
        yg = _dot(m.astype(bf16), pw_pool_ref[gi])
        b_ref[:, lo:lo + gc] = (yg * psc_ref[:, lo:lo + gc]).astype(bf16)


def _mixers_call(x2, mod3, g1, w_sp, b_sp, bbd, cbd, pw, dsk, wglu, bglu, pool_w, pool_scale,
                 bsz, seq, ts):
    t, d = x2.shape
    d_ssm = dsk.shape[1]
    d_pool = pool_scale.shape[1]
    nl = d_ssm // LANES
    ns2 = bbd.shape[1]
    slabs = SSM_SLABS
    nc = seq // slabs
    kern = functools.partial(_mixers_kernel, seq=seq, d=d, d_ssm=d_ssm, d_pool=d_pool,
                             n_state=ns2 // 2, slabs=slabs, ts=ts)
    const2 = lambda b: (0, 0)
    return pl.pallas_call(
        kern,
        grid=(bsz,),
        in_specs=[pl.BlockSpec((seq, d), lambda b: (b, 0)),
                  pl.BlockSpec((None, 1, mod3.shape[2]), lambda b: (b, 0, 0)),
                  pl.BlockSpec((1, d), const2),
                  pl.BlockSpec(w_sp.shape, const2),
                  pl.BlockSpec(b_sp.shape, const2),
                  pl.BlockSpec(bbd.shape, const2),
                  pl.BlockSpec(cbd.shape, const2),
                  pl.BlockSpec(pw.shape, const2),
                  pl.BlockSpec(dsk.shape, const2),
                  pl.BlockSpec(wglu.shape, const2),
                  pl.BlockSpec(bglu.shape, const2),
                  pl.BlockSpec(pool_w.shape, lambda b: (0, 0, 0)),
                  pl.BlockSpec(pool_scale.shape, const2)],
        out_specs=[pl.BlockSpec((nl, seq, LANES), lambda b: (0, b, 0)),
                   pl.BlockSpec((seq, d_pool), lambda b: (b, 0)),
                   pl.BlockSpec((seq, d), lambda b: (b, 0))],
        out_shape=(jax.ShapeDtypeStruct((nl, t, LANES), f32),
                   jax.ShapeDtypeStruct((t, d_pool), bf16),
                   jax.ShapeDtypeStruct((t, d), bf16)),
        scratch_shapes=[pltpu.VMEM((nl, seq, LANES), f32),
                        pltpu.VMEM((seq, d_pool), f32),
                        pltpu.VMEM((nc, ns2), f32),
                        pltpu.VMEM((nc, ns2), f32),
                        pltpu.VMEM((slabs, nc, d_ssm), f32)],
        compiler_params=_cparams(("parallel",)),
        name="mixers",
    )(x2, mod3, g1, w_sp, b_sp, bbd, cbd, pw, dsk, wglu, bglu, pool_w, pool_scale)


def _post_kernel(x_ref, a_ref, b_ref, mod_ref, u_ref, g2_ref, wg_ref, bg_ref, wps_ref, wpp_ref,
                 wout_ref, wr_ref, br_ref, tri_ref, mg_ref, mu_ref, md_ref,
                 h_ref, up_ref, metat_ref, cnt_ref, mgo_ref, muo_ref, mdo_ref, carry_ref,
                 *, d, d_ssm, n_grp, n_exp, tm, ts):
    i = pl.program_id(0)

    @pl.when(i == 0)
    def _():
        carry_ref[...] = jnp.zeros_like(carry_ref)

    mgo_ref[...] = mg_ref[...].astype(bf16)
    muo_ref[...] = mu_ref[...].astype(bf16)
    mdo_ref[...] = md_ref[...].astype(bf16)

    for r0 in range(0, tm, ts):
        _post_rows(x_ref, a_ref, b_ref, mod_ref, u_ref, g2_ref, wg_ref, bg_ref, wps_ref, wpp_ref,
                   wout_ref, wr_ref, br_ref, tri_ref, h_ref, up_ref, metat_ref, carry_ref,
                   d=d, d_ssm=d_ssm, n_grp=n_grp, n_exp=n_exp, r0=r0, tm=ts)
    cnt_ref[...] = carry_ref[...]


def _post_rows(x_ref, a_ref, b_ref, mod_ref, u_ref, g2_ref, wg_ref, bg_ref, wps_ref, wpp_ref,
               wout_ref, wr_ref, br_ref, tri_ref, h_ref, up_ref, metat_ref, carry_ref,
               *, d, d_ssm, n_grp, n_exp, r0, tm):
    rows = pl.ds(r0, tm)
    x = x_ref[rows, :]
    gate1 = mod_ref[:, 2 * d:3 * d]
    shift2, scale2 = mod_ref[:, 3 * d:4 * d], mod_ref[:, 4 * d:5 * d]
    gates = _dot(u_ref[rows, :], wg_ref[...]) + bg_ref[...]
    a = jnp.concatenate([a_ref[j, rows, :] for j in range(d_ssm // LANES)], axis=1).astype(bf16)
    b = b_ref[rows, :]
    merged = (_sigmoid(gates[:, 0:d]) * _dot(a, wps_ref[...])
              + _sigmoid(gates[:, d:2 * d]) * _dot(b, wpp_ref[...]))
    h = x + gate1 * _dot(merged.astype(bf16), wout_ref[...])
    h_ref[rows, :] = h

    u2 = _modulated_norm(h, g2_ref[...], shift2, scale2)

    half = d // 2
    words = _pack_bf16_pairs(u2)

    u_hi = u2.astype(bf16)
    u_lo = (u2 - u_hi.astype(f32)).astype(bf16)
    r1 = _dot_split_rows(u_hi, wr_ref[...])
    r2 = _dot_split_rows(u_lo, wr_ref[...])
    lg = r1 + pltpu.roll(r1, ROUTE_LANES // 2, axis=1) + r2 + br_ref[...]

    lane = lax.broadcasted_iota(jnp.int32, (tm, ROUTE_LANES), 1).astype(f32)
    big = float(ROUTE_LANES)
    epg = float(n_exp // n_grp)
    is_c = lane < n_grp
    cl = jnp.where(is_c, lg, NEG)
    cmax = jnp.max(cl, axis=-1, keepdims=True)
    grp = jnp.min(jnp.where(cl == cmax, lane, big), axis=-1, keepdims=True)
    p_grp = 1.0 / jnp.sum(jnp.where(is_c, jnp.exp(cl - cmax), 0.0), axis=-1, keepdims=True)

    f_lo = n_grp + grp * epg
    fl = jnp.where((lane >= f_lo) & (lane < f_lo + epg), lg, NEG)
    f1 = jnp.max(fl, axis=-1, keepdims=True)
    i1 = jnp.min(jnp.where(fl == f1, lane, big), axis=-1, keepdims=True)
    fl2 = jnp.where(lane == i1, NEG, fl)
    f2 = jnp.max(fl2, axis=-1, keepdims=True)
    i2 = jnp.min(jnp.where(fl2 == f2, lane, big), axis=-1, keepdims=True)
    t2 = jnp.exp(f2 - f1)
    w0 = p_grp / (1.0 + t2)
    w1 = p_grp * t2 / (1.0 + t2)

    j0 = i1 - f_lo
    j1 = i2 - f_lo
    first = j0 < j1
    ja = jnp.minimum(j0, j1)
    jb = jnp.maximum(j0, j1)
    wa = jnp.where(first, w0, w1)
    wb = jnp.where(first, w1, w0)
    n_pair = epg * (epg - 1.0) * 0.5
    cls = grp * n_pair + ja * (2.0 * epg - ja - 1.0) * 0.5 + (jb - ja - 1.0)

    sel = lane == cls
    oh = jnp.where(sel, 1.0, 0.0)
    before = carry_ref[...] + _dot_split_rows(tri_ref[...], oh.astype(bf16))
    rank = jnp.sum(jnp.where(sel, before, 0.0), axis=-1, keepdims=True)
    carry_ref[...] = carry_ref[...] + jnp.sum(oh, axis=0, keepdims=True)

    meta = jnp.zeros((tm, ROUTE_LANES), f32)
    for k, val in enumerate((cls, rank, wa, wb)):
        meta = jnp.where(lane == float(k), val, meta)

    wts = jnp.where(lane == 0.0, wa, jnp.where(lane == 1.0, wb, 0.0))
    assert half + LANES == ROW_PANELS * LANES
    row = jnp.concatenate([words, lax.bitcast_convert_type(wts, jnp.uint32)], axis=1)
    _store_panels(up_ref, row, tm, r0)
    metat_ref[:, rows] = meta.T[0:8, :]


def _slice_per_step(w, steps):
    e, r, c = w.shape
    k = max(1, steps // e)
    assert (e * k) % steps == 0 and r % k == 0
    return w.reshape(e * k, r // k, c), (e * k) // steps


def _post_call(x2, a3, b2, mod3, u2d, g2, wg, bg, wps, wpp, wout, wr, br, moe_w, seq, tm, ts,
               n_grp, n_exp):
    t, d = x2.shape
    d_ssm = wps.shape[0]
    nl = a3.shape[0]
    per = seq // tm
    tri = jnp.tril(jnp.ones((ts, ts), bf16), -1)
    moe_v = [_slice_per_step(w, t // tm) for w in moe_w]
    moe_specs = [pl.BlockSpec((n,) + v.shape[1:], lambda i: (i, 0, 0)) for v, n in moe_v]
    kern = functools.partial(_post_kernel, d=d, d_ssm=d_ssm, n_grp=n_grp, n_exp=n_exp,
                             tm=tm, ts=ts)
    const2 = lambda i: (0, 0)
    wspec = lambda w: pl.BlockSpec(w.shape, const2, pipeline_mode=pl.Buffered(1))
    return pl.pallas_call(
        kern,
        grid=(t // tm,),
        in_specs=[pl.BlockSpec((tm, d), lambda i: (i, 0)),
                  pl.BlockSpec((nl, tm, LANES), lambda i: (0, i, 0)),
                  pl.BlockSpec((tm, b2.shape[1]), lambda i: (i, 0)),
                  pl.BlockSpec((None, 1, mod3.shape[2]), lambda i: (i // per, 0, 0)),
                  pl.BlockSpec((tm, d), lambda i: (i, 0)),
                  pl.BlockSpec((1, d), const2),
                  wspec(wg),
                  pl.BlockSpec(bg.shape, const2),
                  wspec(wps),
                  wspec(wpp),
                  wspec(wout),
                  wspec(wr),
                  pl.BlockSpec(br.shape, const2),
                  wspec(tri)] + moe_specs,
        out_specs=[pl.BlockSpec((tm, d), lambda i: (i, 0)),
                   pl.BlockSpec((tm * ROW_PANELS, LANES), lambda i: (i, 0)),
                   pl.BlockSpec((8, tm), lambda i: (0, i)),
                   pl.BlockSpec((1, ROUTE_LANES), const2)] + moe_specs,
        out_shape=(jax.ShapeDtypeStruct((t, d), f32),
                   jax.ShapeDtypeStruct((t * ROW_PANELS, LANES), jnp.uint32),
                   jax.ShapeDtypeStruct((8, t), f32),
                   jax.ShapeDtypeStruct((1, ROUTE_LANES), f32))
        + tuple(jax.ShapeDtypeStruct(v.shape, bf16) for v, _ in moe_v),
        scratch_shapes=[pltpu.VMEM((1, ROUTE_LANES), f32)],
        compiler_params=_cparams(("arbitrary",)),
        name="post",
    )(x2, a3, b2, mod3, u2d, g2, wg, bg, wps, wpp, wout, wr, br, tri, *[v for v, _ in moe_v])


def _row_copy(src_ref, dst_ref, src_row, dst_row, p, sem):
    src = src_ref.at[pl.ds(pl.multiple_of(src_row * p, p), p)]
    dst = dst_ref.at[pl.ds(pl.multiple_of(dst_row * p, p), p)]
    return pltpu.make_async_copy(src, dst, sem)


def _wait_rows(src_ref, dst_ref, n_rows, p, sem):
    pltpu.make_async_copy(src_ref.at[pl.ds(0, n_rows * p)], dst_ref.at[pl.ds(0, n_rows * p)],
                          sem).wait()


def _scatter_kernel(pos_ref, up_ref, xs_ref, sem, *, tm, p):
    def issue(g, c):
        r0 = g * ROW_DMA_UNROLL
        idx = [pos_ref[0, 0, r0 + u] for u in range(ROW_DMA_UNROLL)]
        for u in range(ROW_DMA_UNROLL):
            _row_copy(up_ref, xs_ref, r0 + u, idx[u], p, sem).start(priority=u % 2)
        return c

    lax.fori_loop(0, tm // ROW_DMA_UNROLL, issue, 0)
    _wait_rows(up_ref, xs_ref, tm, p, sem)


def _scatter_call(up, pos3, n_rows, tm, p):
    n_steps = up.shape[0] // (tm * p)
    return pl.pallas_call(
        functools.partial(_scatter_kernel, tm=tm, p=p),
        grid=(n_steps,),
        in_specs=[pl.BlockSpec((1, 1, tm), lambda i: (i, 0, 0), memory_space=pltpu.SMEM),
                  pl.BlockSpec((tm * p, LANES), lambda i: (i, 0))],
        out_specs=pl.BlockSpec(memory_space=pl.ANY),
        out_shape=jax.ShapeDtypeStruct((n_rows * p, LANES), up.dtype),
        scratch_shapes=[pltpu.SemaphoreType.DMA(())],
        compiler_params=_cparams(("arbitrary",)),
        name="scatter_rows",
    )(pos3, up)


def _experts_kernel(ea_ref, eb_ref, first_ref, nt_ref, xs_ref, wg_a_ref, wg_b_ref, wu_a_ref,
                    wu_b_ref, wd_a_ref, wd_b_ref, ys_ref, xbuf, ybuf, sem_in, sem_out,
                    *, d, tr):
    c = pl.program_id(0)
    n = nt_ref[c]
    t0 = first_ref[c]
    px = d // 2 // LANES
    py = px
    rin, rout = tr * ROW_PANELS, tr * py

    def in_copy(j, slot):
        src = xs_ref.at[pl.ds(pl.multiple_of((t0 + j) * rin, rin), rin)]
        return pltpu.make_async_copy(src, xbuf.at[slot], sem_in.at[slot])

    def out_copy(j, slot):
        dst = ys_ref.at[pl.ds(pl.multiple_of((t0 + j) * rout, rout), rout)]
        return pltpu.make_async_copy(ybuf.at[slot], dst, sem_out.at[slot])

    pl.when(n > 0)(lambda: in_copy(0, 0).start())

    def tile(j, carry):
        slot = j % 2
        in_copy(j, slot).wait()
        pl.when(j + 1 < n)(lambda: in_copy(j + 1, 1 - slot).start())
        pl.when(j >= 2)(lambda: out_copy(j - 2, slot).wait())

        xt = xbuf.at[slot]
        w = jnp.concatenate([xt[pl.ds(k, tr, stride=ROW_PANELS), :] for k in range(px)], axis=1)
        wts = lax.bitcast_convert_type(xt[pl.ds(px, tr, stride=ROW_PANELS), :], f32)
        x = _unpack_bf16_pairs(w).astype(bf16)

        def ffn(wg_ref, wu_ref, wd_ref, wt):
            h1 = _dot(x, wg_ref[...])
            act = h1 * _sigmoid(h1) * _dot(x, wu_ref[...])
            return wt * _dot(act.astype(bf16), wd_ref[...])

        y = (ffn(wg_a_ref, wu_a_ref, wd_a_ref, wts[:, 0:1])
             + ffn(wg_b_ref, wu_b_ref, wd_b_ref, wts[:, 1:2]))
        _store_panels(ybuf.at[slot], _pack_bf16_pairs(y), tr)
        out_copy(j, slot).start()
        return carry

    lax.fori_loop(0, n, tile, 0)
    pl.when(n >= 2)(lambda: out_copy(n - 2, n % 2).wait())
    pl.when(n >= 1)(lambda: out_copy(n - 1, (n - 1) % 2).wait())


def _experts_call(cls_ea, cls_eb, cls_first, cls_nt, xs, wg, wu, wd, tr):
    n_exp, d, d_exp = wg.shape
    n_cls = cls_ea.shape[0]
    n_rows = xs.shape[0] // ROW_PANELS
    py = d // 2 // LANES

    def wa_map(c, ea, eb, first, nt):
        return (ea[c], 0, 0)

    def wb_map(c, ea, eb, first, nt):
        return (eb[c], 0, 0)

    grid_spec = pltpu.PrefetchScalarGridSpec(
        num_scalar_prefetch=4,
        grid=(n_cls,),
        in_specs=[pl.BlockSpec(memory_space=pl.ANY),
                  pl.BlockSpec((None, d, d_exp), wa_map),
                  pl.BlockSpec((None, d, d_exp), wb_map),
                  pl.BlockSpec((None, d, d_exp), wa_map),
                  pl.BlockSpec((None, d, d_exp), wb_map),
                  pl.BlockSpec((None, d_exp, d), wa_map),
                  pl.BlockSpec((None, d_exp, d), wb_map)],
        out_specs=pl.BlockSpec(memory_space=pl.ANY),
        scratch_shapes=[pltpu.VMEM((2, tr * ROW_PANELS, LANES), jnp.uint32),
                        pltpu.VMEM((2, tr * py, LANES), jnp.uint32),
                        pltpu.SemaphoreType.DMA((2,)),
                        pltpu.SemaphoreType.DMA((2,))],
    )
    return pl.pallas_call(
        functools.partial(_experts_kernel, d=d, tr=tr),
        grid_spec=grid_spec,
        out_shape=jax.ShapeDtypeStruct((n_rows * py, LANES), jnp.uint32),
        compiler_params=_cparams(("arbitrary",)),
        name="experts",
    )(cls_ea, cls_eb, cls_first, cls_nt, xs, wg, wg, wu, wu, wd, wd)


def _combine_kernel(pos_ref, posn_ref, h_ref, mod_ref, gf_ref, ys_ref, o_ref,
                    g_ref, sem, *, d, tm, n_steps, final_norm):
    i = pl.program_id(0)
    slot = i % 2
    py = d // 2 // LANES

    def issue(p_ref, s):
        def body(g, c):
            r0 = g * ROW_DMA_UNROLL
            idx = [p_ref[0, 0, r0 + u] for u in range(ROW_DMA_UNROLL)]
            for u in range(ROW_DMA_UNROLL):
                _row_copy(ys_ref, g_ref.at[s], idx[u], r0 + u, py,
                          sem.at[s]).start(priority=u % 2)
            return c

        lax.fori_loop(0, tm // ROW_DMA_UNROLL, body, 0)

    pl.when(i == 0)(lambda: issue(pos_ref, 0))
    pl.when(i + 1 < n_steps)(lambda: issue(posn_ref, 1 - slot))
    _wait_rows(ys_ref, g_ref.at[slot], tm, py, sem.at[slot])

    gate2 = mod_ref[:, 5 * d:6 * d]
    h = h_ref[...] + gate2 * _unpack_bf16_pairs(_load_panels(g_ref.at[slot], tm, py))
    if final_norm:
        ms = jnp.mean(h * h, axis=-1, keepdims=True)
        h = h * lax.rsqrt(ms + EPS) * gf_ref[...]
    o_ref[...] = h


def _combine_call(pos3, h1, mod3, gf, ys, seq, tm, final_norm):
    t, d = h1.shape
    per = seq // tm
    n_steps = t // tm
    pos_spec = lambda f: pl.BlockSpec((1, 1, tm), f, memory_space=pltpu.SMEM)
    return pl.pallas_call(
        functools.partial(_combine_kernel, d=d, tm=tm, n_steps=n_steps, final_norm=final_norm),
        grid=(n_steps,),
        in_specs=[pos_spec(lambda i: (i, 0, 0)),
                  pos_spec(lambda i: (jnp.minimum(i + 1, n_steps - 1), 0, 0)),
                  pl.BlockSpec((tm, d), lambda i: (i, 0)),
                  pl.BlockSpec((None, 1, mod3.shape[2]), lambda i: (i // per, 0, 0)),
                  pl.BlockSpec((1, d), lambda i: (0, 0)),
                  pl.BlockSpec(memory_space=pl.ANY)],
        out_specs=pl.BlockSpec((tm, d), lambda i: (i, 0)),
        out_shape=jax.ShapeDtypeStruct((t, d), f32),
        scratch_shapes=[pltpu.VMEM((2, tm * (d // 2 // LANES), LANES), jnp.uint32),
                        pltpu.SemaphoreType.DMA((2,))],
        compiler_params=_cparams(("arbitrary",)),
        name="combine",
    )(pos3, pos3, h1, mod3, gf, ys)


def _row_tile(seq, target):
    tm = min(target, seq)
    assert seq % tm == 0 and tm % 8 == 0
    return tm


def _layer(h2, mod3, p, bsz, seq, final_gain, final_norm):
    t, d = h2.shape
    g, n_p, n_h = p["ssm_b_re"].shape
    d_ssm = g * n_h
    d_pool = p["pool_scale"].shape[-1]
    n_grp = p["router_coarse_w"].shape[-1]
    n_exp = p["router_fine_w"].shape[-1]
    assert n_grp + n_exp <= ROUTE_LANES // 2 and seq % SSM_SLABS == 0
    tm = _row_tile(seq, 512)

    swap = lambda v: jnp.swapaxes(v, 1, 2)
    pw, bbd, cbd = _ssm_prep_call(
        p["ssm_lam_re"], p["ssm_lam_im"], p["ssm_log_dt"], swap(p["ssm_b_re"]),
        swap(p["ssm_b_im"]), swap(p["ssm_c_re"]), swap(p["ssm_c_im"]), SSM_SLABS)
    bbd, cbd = bbd.astype(bf16), cbd.astype(bf16)
    pw2 = pw.reshape(SSM_SLABS, 2 * g * n_p)

    w_in, b_in = p["w_in"], p["b_in"]
    n_sp = d_ssm + d_pool
    a3, b2, u2d = _mixers_call(h2, mod3, p["norm1_g"].reshape(1, d), w_in[:, :n_sp].astype(bf16),
                          b_in[:n_sp].reshape(1, n_sp), bbd, cbd, pw2,
                          p["ssm_d"].reshape(1, d_ssm),
                          p["ssm_w_glu"].astype(bf16), p["ssm_b_glu"].reshape(1, d_ssm),
                          p["pool_w"].astype(bf16), p["pool_scale"].reshape(1, d_pool),
                          bsz, seq, tm)

    wr = jnp.concatenate([p["router_coarse_w"], p["router_fine_w"]], axis=1)
    wr_hi = wr.astype(bf16)
    wr_lo = (wr - wr_hi.astype(f32)).astype(bf16)
    padc = ROUTE_LANES // 2 - wr.shape[1]
    wr_cat = jnp.concatenate([jnp.pad(wr_hi, ((0, 0), (0, padc))),
                              jnp.pad(wr_lo, ((0, 0), (0, padc)))], axis=1)
    br = jnp.pad(jnp.concatenate([p["router_coarse_b"], p["router_fine_b"]]),
                 (0, ROUTE_LANES - wr.shape[1])).reshape(1, ROUTE_LANES)

    moe_w = (p["moe_w_gate"], p["moe_w_up"], p["moe_w_down"])
    h1, up, metat, cnt, *moe_bf = _post_call(
        h2, a3, b2, mod3, u2d, p["norm2_g"].reshape(1, d),
        w_in[:, n_sp:].astype(bf16), b_in[n_sp:].reshape(1, 2 * d),
        p["w_proj_ssm"].astype(bf16), p["w_proj_pool"].astype(bf16), p["w_out"].astype(bf16),
        wr_cat, br, moe_w, seq, _row_tile(seq, 2 * tm), tm, n_grp, n_exp)
    moe_bf = [v.reshape(w.shape) for v, w in zip(moe_bf, moe_w)]

    tr = EXPERT_TILE_ROWS
    epg = n_exp // n_grp
    pairs = [(a, b) for a in range(epg) for b in range(a + 1, epg)]
    n_cls = n_grp * len(pairs)
    assert n_cls <= ROUTE_LANES
    cls_ea = jnp.asarray([gi * epg + a for gi in range(n_grp) for a, _ in pairs], jnp.int32)
    cls_eb = jnp.asarray([gi * epg + b for gi in range(n_grp) for _, b in pairs], jnp.int32)
    counts = cnt[0, 0:n_cls].astype(jnp.int32)
    padded = ((counts + tr - 1) // tr) * tr
    ends = jnp.cumsum(padded)
    offs = ends - padded
    ids = jnp.arange(n_cls, dtype=jnp.int32)
    cls = metat[0].astype(jnp.int32)
    off = jnp.sum(jnp.where(cls[None, :] == ids[:, None], offs[:, None], 0), axis=0)
    pos = off + metat[1].astype(jnp.int32)
    tms = _row_tile(seq, 2 * tm)
    pos3 = pos.reshape(t // tms, 1, tms)
    nt_max = t // tr + n_cls

    xs = _scatter_call(up, pos3, nt_max * tr, tms, ROW_PANELS)
    ys = _experts_call(cls_ea, cls_eb, offs // tr, padded // tr, xs, *moe_bf, tr)
    return _combine_call(pos.reshape(t // tm, 1, tm), h1, mod3, final_gain.reshape(1, d), ys,
                         seq, tm, final_norm)


def kernel(x, c, w_mod, b_mod, norm1_g, w_in, b_in, ssm_lam_re, ssm_lam_im, ssm_log_dt, ssm_b_re, ssm_b_im, ssm_c_re, ssm_c_im, ssm_d, ssm_w_glu, ssm_b_glu, pool_w, pool_scale, w_proj_ssm, w_proj_pool, w_out, norm2_g, router_coarse_w, router_coarse_b, router_fine_w, router_fine_b, moe_w_gate, moe_w_up, moe_w_down, norm_f_g):
    bsz, seq, d = x.shape
    depth = w_mod.shape[0]
    per_layer = dict(
        norm1_g=norm1_g, w_in=w_in, b_in=b_in, ssm_lam_re=ssm_lam_re, ssm_lam_im=ssm_lam_im,
        ssm_log_dt=ssm_log_dt, ssm_b_re=ssm_b_re, ssm_b_im=ssm_b_im, ssm_c_re=ssm_c_re,
        ssm_c_im=ssm_c_im, ssm_d=ssm_d, ssm_w_glu=ssm_w_glu, ssm_b_glu=ssm_b_glu, pool_w=pool_w,
        pool_scale=pool_scale, w_proj_ssm=w_proj_ssm, w_proj_pool=w_proj_pool, w_out=w_out,
        norm2_g=norm2_g, router_coarse_w=router_coarse_w, router_coarse_b=router_coarse_b,
        router_fine_w=router_fine_w, router_fine_b=router_fine_b, moe_w_gate=moe_w_gate,
        moe_w_up=moe_w_up, moe_w_down=moe_w_down)
    h2 = x.reshape(bsz * seq, d)
    for l in range(depth):
        p = {k: v[l] for k, v in per_layer.items()}
        mod3 = _mod_call(c, w_mod[l], b_mod[l]).reshape(bsz, 1, N_MOD * d)
        h2 = _layer(h2, mod3, p, bsz, seq, norm_f_g, final_norm=(l == depth - 1))
    return h2.reshape(bsz, seq, d)
```

```python
import functools
import math

import jax
import jax.numpy as jnp
from jax import lax
from jax.experimental import pallas as pl
from jax.experimental.pallas import tpu as pltpu

EPS = 1e-6
POOL_WINDOWS = (2, 4, 8, 16)
TOP_K_FINE = 2
N_MOD = 6
LANES = 128
SSM_SLABS = 8
ROUTE_LANES = 128
ROW_DMA_UNROLL = 16
ROW_PANELS = 5
EXPERT_TILE_ROWS = 256
VMEM_LIMIT = 56 * 1024 * 1024
NEG = -1e30

f32 = jnp.float32
bf16 = jnp.bfloat16


def _cparams(sem):
    return pltpu.CompilerParams(dimension_semantics=sem, vmem_limit_bytes=VMEM_LIMIT)


def _dot(a, b):
    return jnp.dot(a, b, preferred_element_type=f32)


def _pack_bf16_pairs(v):
    half = v.shape[1] // 2
    hi = lax.bitcast_convert_type(v[:, 0:half].astype(bf16).astype(f32), jnp.uint32)
    lo = lax.bitcast_convert_type(v[:, half:].astype(bf16).astype(f32), jnp.uint32)
    return hi | (lo >> 16)


def _unpack_bf16_pairs(w):
    hi = lax.bitcast_convert_type(w & jnp.uint32(0xFFFF0000), f32)
    lo = lax.bitcast_convert_type(w << 16, f32)
    return jnp.concatenate([hi, lo], axis=1)


def _sigmoid(v):
    return 0.5 * jnp.tanh(0.5 * v) + 0.5


def _dot_split_rows(a, b):
    h = a.shape[0] // 2
    return jnp.concatenate([_dot(a[:h], b), _dot(a[h:], b)], axis=0)


def _store_panels(ref, val, rows, row0=0):
    p = val.shape[1] // LANES
    for j in range(p):
        ref[pl.ds(row0 * p + j, rows, stride=p), :] = val[:, j * LANES:(j + 1) * LANES]


def _load_panels(ref, rows, p):
    return jnp.concatenate([ref[pl.ds(j, rows, stride=p), :] for j in range(p)], axis=1)


def _split_bf16(v):
    hi = v.astype(bf16)
    return hi, (v - hi.astype(f32)).astype(bf16)


def _mod_kernel(c_ref, w_ref, b_ref, o_ref):
    c = c_ref[...]
    a_hi, a_lo = _split_bf16(c * _sigmoid(c))
    w_hi, w_lo = _split_bf16(w_ref[...])
    o_ref[...] = _dot(a_hi, w_hi) + (_dot(a_lo, w_hi) + _dot(a_hi, w_lo)) + b_ref[...]


def _mod_call(c, w_mod, b_mod):
    bsz, d = c.shape
    n = w_mod.shape[1]
    tn = d
    return pl.pallas_call(
        _mod_kernel,
        grid=(n // tn,),
        in_specs=[pl.BlockSpec((bsz, d), lambda j: (0, 0)),
                  pl.BlockSpec((d, tn), lambda j: (0, j)),
                  pl.BlockSpec((1, tn), lambda j: (0, j))],
        out_specs=pl.BlockSpec((bsz, tn), lambda j: (0, j)),
        out_shape=jax.ShapeDtypeStruct((bsz, n), f32),
        compiler_params=_cparams(("parallel",)),
        name="mod",
    )(c, w_mod, b_mod.reshape(1, n))


def _ssm_prep_kernel(lr_ref, li_ref, ldt_ref, btr_ref, bti_ref, ctr_ref, cti_ref,
                     pw_ref, bbd_ref, cbd_ref, *, n_pow, n_grp):
    n_h, n_p = btr_ref.shape[1], btr_ref.shape[2]
    gp = n_grp * n_p
    lr = lr_ref[...]
    li = li_ref[...]
    dt = jnp.exp(ldt_ref[...])
    for k in range(1, n_pow + 1):
        mag = jnp.exp(lr * dt * float(k))
        ang = li * dt * float(k)
        pw_ref[k - 1, 0] = mag * jnp.cos(ang)
        pw_ref[k - 1, 1] = mag * jnp.sin(ang)
    lb_re = pw_ref[0, 0]
    lb_im = pw_ref[0, 1]
    den = lr * lr + li * li
    nr = lb_re - 1.0
    f_re = (nr * lr + lb_im * li) / den
    f_im = (lb_im * lr - nr * li) / den
    bbd_ref[...] = jnp.zeros_like(bbd_ref)
    cbd_ref[...] = jnp.zeros_like(cbd_ref)
    for g in range(n_grp):
        fr, fi = f_re[g:g + 1, :], f_im[g:g + 1, :]
        hs, ps = pl.ds(g * n_h, n_h), pl.ds(g * n_p, n_p)
        bbd_ref[hs, ps] = fr * btr_ref[g] - fi * bti_ref[g]
        bbd_ref[hs, pl.ds(gp + g * n_p, n_p)] = fr * bti_ref[g] + fi * btr_ref[g]
        cbd_ref[ps, hs] = ctr_ref[g]
        cbd_ref[pl.ds(gp + g * n_p, n_p), hs] = -cti_ref[g]


def _ssm_prep_call(lam_re, lam_im, log_dt, bt_re, bt_im, ct_re, ct_im, n_pow):
    g, p = lam_re.shape
    h = bt_re.shape[1]
    return pl.pallas_call(
        functools.partial(_ssm_prep_kernel, n_pow=n_pow, n_grp=g),
        out_shape=(jax.ShapeDtypeStruct((n_pow, 2, g, p), f32),
                   jax.ShapeDtypeStruct((g * h, 2 * g * p), f32),
                   jax.ShapeDtypeStruct((2 * g * p, g * h), f32)),
        name="ssm_prep",
    )(lam_re, lam_im, log_dt.reshape(g, 1), bt_re, bt_im, ct_re, ct_im)


def _modulated_norm(x, gain, shift, scale):
    ms = jnp.mean(x * x, axis=-1, keepdims=True)
    return x * lax.rsqrt(ms + EPS) * (gain * (1.0 + scale)) + shift


def _cmul_add(ar, ai, xr, xi, br, bi):
    return ar * xr - ai * xi + br, ar * xi + ai * xr + bi


def _mixers_kernel(x_ref, mod_ref, g1_ref, wsp_ref, bsp_ref,
                   bbd_ref, cbd_ref, pw_ref, dsk_ref, wglu_ref, bglu_ref, pw_pool_ref, psc_ref,
                   a_ref, b_ref, u_ref, us_ref, up_ref, e_ref, s_ref, y_ref,
                   *, seq, d, d_ssm, d_pool, n_state, slabs, ts):
    nc = seq // slabs
    nl = d_ssm // LANES
    ns = n_state

    for r0 in range(0, seq, ts):
        rows = pl.ds(r0, ts)
        u = _modulated_norm(x_ref[rows, :], g1_ref[...], mod_ref[:, 0:d], mod_ref[:, d:2 * d])
        u = u.astype(bf16)
        u_ref[rows, :] = u
        r = _dot(u, wsp_ref[...]) + bsp_ref[...]
        for j in range(nl):
            us_ref[j, rows, :] = r[:, j * LANES:(j + 1) * LANES]
        up_ref[rows, :] = r[:, d_ssm:]

    dsk = dsk_ref[...]
    l1r = pw_ref[0:1, 0:ns]
    l1i = pw_ref[0:1, ns:2 * ns]

    xr = xi = None
    for t in range(slabs):
        ut = jnp.concatenate([us_ref[j, pl.ds(t, nc, stride=slabs), :] for j in range(nl)], axis=1)
        bu = _dot(ut.astype(bf16), bbd_ref[...])
        br, bi = bu[:, 0:ns], bu[:, ns:2 * ns]
        if t == 0:
            xr, xi = br, bi
        else:
            xr, xi = _cmul_add(l1r, l1i, xr, xi, br, bi)
        xc = jnp.concatenate([xr, xi], axis=1).astype(bf16)
        y_ref[t] = _dot_split_rows(xc, cbd_ref[...]) + dsk * ut
    e_ref[:, 0:ns] = xr
    e_ref[:, ns:2 * ns] = xi

    lLr = pw_ref[slabs - 1:slabs, 0:ns]
    lLi = pw_ref[slabs - 1:slabs, ns:2 * ns]

    def chunk_step(c, carry):
        sr, si = carry
        s_ref[pl.ds(c, 1), 0:ns] = sr
        s_ref[pl.ds(c, 1), ns:2 * ns] = si
        er = e_ref[pl.ds(c, 1), 0:ns]
        ei = e_ref[pl.ds(c, 1), ns:2 * ns]
        return _cmul_add(lLr, lLi, sr, si, er, ei)

    zero = jnp.zeros((1, ns), f32)
    lax.fori_loop(0, nc, chunk_step, (zero, zero))

    sr = s_ref[:, 0:ns]
    si = s_ref[:, ns:2 * ns]
    for t in range(slabs):
        pr = pw_ref[t:t + 1, 0:ns]
        pi = pw_ref[t:t + 1, ns:2 * ns]
        zr = pr * sr - pi * si
        zi = pr * si + pi * sr
        zc = jnp.concatenate([zr, zi], axis=1).astype(bf16)
        y = y_ref[t] + _dot_split_rows(zc, cbd_ref[...])
        y = jax.nn.gelu(y)
        z = y * _sigmoid(_dot(y.astype(bf16), wglu_ref[...]) + bglu_ref[...])
        for j in range(nl):
            a_ref[j, pl.ds(t, nc, stride=slabs), :] = z[:, j * LANES:(j + 1) * LANES]

    gc = d_pool // len(POOL_WINDOWS)
    row = lax.broadcasted_iota(jnp.int32, (seq, gc), 0)
    for gi, w in enumerate(POOL_WINDOWS):
        lo = gi * gc
        v = up_ref[:, lo:lo + gc]
        acc = v
        span = 1
        while span < w:
            acc = acc + jnp.where(row >= span, pltpu.roll(acc, span, axis=0), 0.0)
            span *= 2
        cnt = jnp.minimum(row + 1, w).astype(f32)
        m = acc / cnt - v
        yg = _dot(m.astype(bf16), pw_pool_ref[gi])
        b_ref[:, lo:lo + gc] = (yg * psc_ref[:, lo:lo + gc]).astype(bf16)


def _mixers_call(x2, mod3, g1, w_sp, b_sp, bbd, cbd, pw, dsk, wglu, bglu, pool_w, pool_scale,
                 bsz, seq, ts):
    t, d = x2.shape
    d_ssm = dsk.shape[1]
    d_pool = pool_scale.shape[1]
    nl = d_ssm // LANES
    ns2 = bbd.shape[1]
    slabs = SSM_SLABS
    nc = seq // slabs
    kern = functools.partial(_mixers_kernel, seq=seq, d=d, d_ssm=d_ssm, d_pool=d_pool,
                             n_state=ns2 // 2, slabs=slabs, ts=ts)
    const2 = lambda b: (0, 0)
    return pl.pallas_call(
        kern,
        grid=(bsz,),
        in_specs=[pl.BlockSpec((seq, d), lambda b: (b, 0)),
                  pl.BlockSpec((None, 1, mod3.shape[2]), lambda b: (b, 0, 0)),
                  pl.BlockSpec((1, d), const2),
                  pl.BlockSpec(w_sp.shape, const2),
                  pl.BlockSpec(b_sp.shape, const2),
                  pl.BlockSpec(bbd.shape, const2),
                  pl.BlockSpec(cbd.shape, const2),
                  pl.BlockSpec(pw.shape, const2),
                  pl.BlockSpec(dsk.shape, const2),
                  pl.BlockSpec(wglu.shape, const2),
                  pl.BlockSpec(bglu.shape, const2),
                  pl.BlockSpec(pool_w.shape, lambda b: (0, 0, 0)),
                  pl.BlockSpec(pool_scale.shape, const2)],
        out_specs=[pl.BlockSpec((nl, seq, LANES), lambda b: (0, b, 0)),
                   pl.BlockSpec((seq, d_pool), lambda b: (b, 0)),
                   pl.BlockSpec((seq, d), lambda b: (b, 0))],
        out_shape=(jax.ShapeDtypeStruct((nl, t, LANES), f32),
                   jax.ShapeDtypeStruct((t, d_pool), bf16),
                   jax.ShapeDtypeStruct((t, d), bf16)),
        scratch_shapes=[pltpu.VMEM((nl, seq, LANES), f32),
                        pltpu.VMEM((seq, d_pool), f32),
                        pltpu.VMEM((nc, ns2), f32),
                        pltpu.VMEM((nc, ns2), f32),
                        pltpu.VMEM((slabs, nc, d_ssm), f32)],
        compiler_params=_cparams(("parallel",)),
        name="mixers",
    )(x2, mod3, g1, w_sp, b_sp, bbd, cbd, pw, dsk, wglu, bglu, pool_w, pool_scale)


def _post_kernel(x_ref, a_ref, b_ref, mod_ref, u_ref, g2_ref, wg_ref, bg_ref, wps_ref, wpp_ref,
                 wout_ref, wr_ref, br_ref, tri_ref, mg_ref, mu_ref, md_ref,
                 h_ref, up_ref, metat_ref, cnt_ref, mgo_ref, muo_ref, mdo_ref, carry_ref,
                 *, d, d_ssm, n_grp, n_exp, tm, ts):
    i = pl.program_id(0)

    @pl.when(i == 0)
    def _():
        carry_ref[...] = jnp.zeros_like(carry_ref)

    mgo_ref[...] = mg_ref[...].astype(bf16)
    muo_ref[...] = mu_ref[...].astype(bf16)
    mdo_ref[...] = md_ref[...].astype(bf16)

    for r0 in range(0, tm, ts):
        _post_rows(x_ref, a_ref, b_ref, mod_ref, u_ref, g2_ref, wg_ref, bg_ref, wps_ref, wpp_ref,
                   wout_ref, wr_ref, br_ref, tri_ref, h_ref, up_ref, metat_ref, carry_ref,
                   d=d, d_ssm=d_ssm, n_grp=n_grp, n_exp=n_exp, r0=r0, tm=ts)
    cnt_ref[...] = carry_ref[...]


def _post_rows(x_ref, a_ref, b_ref, mod_ref, u_ref, g2_ref, wg_ref, bg_ref, wps_ref, wpp_ref,
               wout_ref, wr_ref, br_ref, tri_ref, h_ref, up_ref, metat_ref, carry_ref,
               *, d, d_ssm, n_grp, n_exp, r0, tm):
    rows = pl.ds(r0, tm)
    x = x_ref[rows, :]
    gate1 = mod_ref[:, 2 * d:3 * d]
    shift2, scale2 = mod_ref[:, 3 * d:4 * d], mod_ref[:, 4 * d:5 * d]
    gates = _dot(u_ref[rows, :], wg_ref[...]) + bg_ref[...]
    a = jnp.concatenate([a_ref[j, rows, :] for j in range(d_ssm // LANES)], axis=1).astype(bf16)
    b = b_ref[rows, :]
    merged = (_sigmoid(gates[:, 0:d]) * _dot(a, wps_ref[...])
              + _sigmoid(gates[:, d:2 * d]) * _dot(b, wpp_ref[...]))
    h = x + gate1 * _dot(merged.astype(bf16), wout_ref[...])
    h_ref[rows, :] = h

    u2 = _modulated_norm(h, g2_ref[...], shift2, scale2)

    half = d // 2
    words = _pack_bf16_pairs(u2)

    u_hi = u2.astype(bf16)
    u_lo = (u2 - u_hi.astype(f32)).astype(bf16)
    r1 = _dot_split_rows(u_hi, wr_ref[...])
    r2 = _dot_split_rows(u_lo, wr_ref[...])
    lg = r1 + pltpu.roll(r1, ROUTE_LANES // 2, axis=1) + r2 + br_ref[...]

    lane = lax.broadcasted_iota(jnp.int32, (tm, ROUTE_LANES), 1).astype(f32)
    big = float(ROUTE_LANES)
    epg = float(n_exp // n_grp)
    is_c = lane < n_grp
    cl = jnp.where(is_c, lg, NEG)
    cmax = jnp.max(cl, axis=-1, keepdims=True)
    grp = jnp.min(jnp.where(cl == cmax, lane, big), axis=-1, keepdims=True)
    p_grp = 1.0 / jnp.sum(jnp.where(is_c, jnp.exp(cl - cmax), 0.0), axis=-1, keepdims=True)

    f_lo = n_grp + grp * epg
    fl = jnp.where((lane >= f_lo) & (lane < f_lo + epg), lg, NEG)
    f1 = jnp.max(fl, axis=-1, keepdims=True)
    i1 = jnp.min(jnp.where(fl == f1, lane, big), axis=-1, keepdims=True)
    fl2 = jnp.where(lane == i1, NEG, fl)
    f2 = jnp.max(fl2, axis=-1, keepdims=True)
    i2 = jnp.min(jnp.where(fl2 == f2, lane, big), axis=-1, keepdims=True)
    t2 = jnp.exp(f2 - f1)
    w0 = p_grp / (1.0 + t2)
    w1 = p_grp * t2 / (1.0 + t2)

    j0 = i1 - f_lo
    j1 = i2 - f_lo
    first = j0 < j1
    ja = jnp.minimum(j0, j1)
    jb = jnp.maximum(j0, j1)
    wa = jnp.where(first, w0, w1)
    wb = jnp.where(first, w1, w0)
    n_pair = epg * (epg - 1.0) * 0.5
    cls = grp * n_pair + ja * (2.0 * epg - ja - 1.0) * 0.5 + (jb - ja - 1.0)

    sel = lane == cls
    oh = jnp.where(sel, 1.0, 0.0)
    before = carry_ref[...] + _dot_split_rows(tri_ref[...], oh.astype(bf16))
    rank = jnp.sum(jnp.where(sel, before, 0.0), axis=-1, keepdims=True)
    carry_ref[...] = carry_ref[...] + jnp.sum(oh, axis=0, keepdims=True)

    meta = jnp.zeros((tm, ROUTE_LANES), f32)
    for k, val in enumerate((cls, rank, wa, wb)):
        meta = jnp.where(lane == float(k), val, meta)

    wts = jnp.where(lane == 0.0, wa, jnp.where(lane == 1.0, wb, 0.0))
    assert half + LANES == ROW_PANELS * LANES
    row = jnp.concatenate([words, lax.bitcast_convert_type(wts, jnp.uint32)], axis=1)
    _store_panels(up_ref, row, tm, r0)
    metat_ref[:, rows] = meta.T[0:8, :]


def _slice_per_step(w, steps):
    e, r, c = w.shape
    k = max(1, steps // e)
    assert (e * k) % steps == 0 and r % k == 0
    return w.reshape(e * k, r // k, c), (e * k) // steps


def _post_call(x2, a3, b2, mod3, u2d, g2, wg, bg, wps, wpp, wout, wr, br, moe_w, seq, tm, ts,
               n_grp, n_exp):
    t, d = x2.shape
    d_ssm = wps.shape[0]
    nl = a3.shape[0]
    per = seq // tm
    tri = jnp.tril(jnp.ones((ts, ts), bf16), -1)
    moe_v = [_slice_per_step(w, t // tm) for w in moe_w]
    moe_specs = [pl.BlockSpec((n,) + v.shape[1:], lambda i: (i, 0, 0)) for v, n in moe_v]
    kern = functools.partial(_post_kernel, d=d, d_ssm=d_ssm, n_grp=n_grp, n_exp=n_exp,
                             tm=tm, ts=ts)
    const2 = lambda i: (0, 0)
    wspec = lambda w: pl.BlockSpec(w.shape, const2, pipeline_mode=pl.Buffered(1))
    return pl.pallas_call(
        kern,
        grid=(t // tm,),
        in_specs=[pl.BlockSpec((tm, d), lambda i: (i, 0)),
                  pl.BlockSpec((nl, tm, LANES), lambda i: (0, i, 0)),
                  pl.BlockSpec((tm, b2.shape[1]), lambda i: (i, 0)),
                  pl.BlockSpec((None, 1, mod3.shape[2]), lambda i: (i // per, 0, 0)),
                  pl.BlockSpec((tm, d), lambda i: (i, 0)),
                  pl.BlockSpec((1, d), const2),
                  wspec(wg),
                  pl.BlockSpec(bg.shape, const2),
                  wspec(wps),
                  wspec(wpp),
                  wspec(wout),
                  wspec(wr),
                  pl.BlockSpec(br.shape, const2),
                  wspec(tri)] + moe_specs,
        out_specs=[pl.BlockSpec((tm, d), lambda i: (i, 0)),
                   pl.BlockSpec((tm * ROW_PANELS, LANES), lambda i: (i, 0)),
                   pl.BlockSpec((8, tm), lambda i: (0, i)),
                   pl.BlockSpec((1, ROUTE_LANES), const2)] + moe_specs,
        out_shape=(jax.ShapeDtypeStruct((t, d), f32),
                   jax.ShapeDtypeStruct((t * ROW_PANELS, LANES), jnp.uint32),
                   jax.ShapeDtypeStruct((8, t), f32),
                   jax.ShapeDtypeStruct((1, ROUTE_LANES), f32))
        + tuple(jax.ShapeDtypeStruct(v.shape, bf16) for v, _ in moe_v),
        scratch_shapes=[pltpu.VMEM((1, ROUTE_LANES), f32)],
        compiler_params=_cparams(("arbitrary",)),
        name="post",
    )(x2, a3, b2, mod3, u2d, g2, wg, bg, wps, wpp, wout, wr, br, tri, *[v for v, _ in moe_v])


def _row_copy(src_ref, dst_ref, src_row, dst_row, p, sem):
    src = src_ref.at[pl.ds(pl.multiple_of(src_row * p, p), p)]
    dst = dst_ref.at[pl.ds(pl.multiple_of(dst_row * p, p), p)]
    return pltpu.make_async_copy(src, dst, sem)


def _wait_rows(src_ref, dst_ref, n_rows, p, sem):
    pltpu.make_async_copy(src_ref.at[pl.ds(0, n_rows * p)], dst_ref.at[pl.ds(0, n_rows * p)],
                          sem).wait()


def _scatter_kernel(pos_ref, up_ref, xs_ref, sem, *, tm, p):
    def issue(g, c):
        r0 = g * ROW_DMA_UNROLL
        idx = [pos_ref[0, 0, r0 + u] for u in range(ROW_DMA_UNROLL)]
        for u in range(ROW_DMA_UNROLL):
            _row_copy(up_ref, xs_ref, r0 + u, idx[u], p, sem).start(priority=u % 2)
        return c

    lax.fori_loop(0, tm // ROW_DMA_UNROLL, issue, 0)
    _wait_rows(up_ref, xs_ref, tm, p, sem)


def _scatter_call(up, pos3, n_rows, tm, p):
    n_steps = up.shape[0] // (tm * p)
    return pl.pallas_call(
        functools.partial(_scatter_kernel, tm=tm, p=p),
        grid=(n_steps,),
        in_specs=[pl.BlockSpec((1, 1, tm), lambda i: (i, 0, 0), memory_space=pltpu.SMEM),
                  pl.BlockSpec((tm * p, LANES), lambda i: (i, 0))],
        out_specs=pl.BlockSpec(memory_space=pl.ANY),
        out_shape=jax.ShapeDtypeStruct((n_rows * p, LANES), up.dtype),
        scratch_shapes=[pltpu.SemaphoreType.DMA(())],
        compiler_params=_cparams(("arbitrary",)),
        name="scatter_rows",
    )(pos3, up)


def _experts_kernel(ea_ref, eb_ref, first_ref, nt_ref, xs_ref, wg_a_ref, wg_b_ref, wu_a_ref,
                    wu_b_ref, wd_a_ref, wd_b_ref, ys_ref, xbuf, ybuf, sem_in, sem_out,
                    *, d, tr):
    c = pl.program_id(0)
    n_cls = pl.num_programs(0)
    t0 = first_ref[c]
    t1 = t0 + nt_ref[c]
    total = first_ref[n_cls - 1] + nt_ref[n_cls - 1]
    px = d // 2 // LANES
    py = px
    rin, rout = tr * ROW_PANELS, tr * py

    def in_copy(g):
        src = xs_ref.at[pl.ds(pl.multiple_of(g * rin, rin), rin)]
        return pltpu.make_async_copy(src, xbuf.at[g % 2], sem_in.at[g % 2])

    def out_copy(g):
        dst = ys_ref.at[pl.ds(pl.multiple_of(g * rout, rout), rout)]
        return pltpu.make_async_copy(ybuf.at[g % 2], dst, sem_out.at[g % 2])

    pl.when((c == 0) & (total > 0))(lambda: in_copy(0).start())

    def tile(g, carry):
        slot = g % 2
        in_copy(g).wait()
        pl.when(g + 1 < total)(lambda: in_copy(g + 1).start())
        pl.when(g >= 2)(lambda: out_copy(g - 2).wait())

        xt = xbuf.at[slot]
        w = jnp.concatenate([xt[pl.ds(k, tr, stride=ROW_PANELS), :] for k in range(px)], axis=1)
        wts = lax.bitcast_convert_type(xt[pl.ds(px, tr, stride=ROW_PANELS), :], f32)
        x = _unpack_bf16_pairs(w).astype(bf16)

        def ffn(wg_ref, wu_ref, wd_ref, wt):
            h1 = _dot(x, wg_ref[...])
            act = h1 * _sigmoid(h1) * _dot(x, wu_ref[...])
            return wt * _dot(act.astype(bf16), wd_ref[...])

        y = (ffn(wg_a_ref, wu_a_ref, wd_a_ref, wts[:, 0:1])
             + ffn(wg_b_ref, wu_b_ref, wd_b_ref, wts[:, 1:2]))
        _store_panels(ybuf.at[slot], _pack_bf16_pairs(y), tr)
        out_copy(g).start()
        return carry

    lax.fori_loop(t0, t1, tile, 0)

    @pl.when(c == n_cls - 1)
    def _():
        pl.when(total >= 2)(lambda: out_copy(total - 2).wait())
        pl.when(total >= 1)(lambda: out_copy(total - 1).wait())


def _experts_call(cls_ea, cls_eb, cls_first, cls_nt, xs, wg, wu, wd, tr):
    n_exp, d, d_exp = wg.shape
    n_cls = cls_ea.shape[0]
    n_rows = xs.shape[0] // ROW_PANELS
    py = d // 2 // LANES

    def wa_map(c, ea, eb, first, nt):
        return (ea[c], 0, 0)

    def wb_map(c, ea, eb, first, nt):
        return (eb[c], 0, 0)

    grid_spec = pltpu.PrefetchScalarGridSpec(
        num_scalar_prefetch=4,
        grid=(n_cls,),
        in_specs=[pl.BlockSpec(memory_space=pl.ANY),
                  pl.BlockSpec((None, d, d_exp), wa_map),
                  pl.BlockSpec((None, d, d_exp), wb_map),
                  pl.BlockSpec((None, d, d_exp), wa_map),
                  pl.BlockSpec((None, d, d_exp), wb_map),
                  pl.BlockSpec((None, d_exp, d), wa_map),
                  pl.BlockSpec((None, d_exp, d), wb_map)],
        out_specs=pl.BlockSpec(memory_space=pl.ANY),
        scratch_shapes=[pltpu.VMEM((2, tr * ROW_PANELS, LANES), jnp.uint32),
                        pltpu.VMEM((2, tr * py, LANES), jnp.uint32),
                        pltpu.SemaphoreType.DMA((2,)),
                        pltpu.SemaphoreType.DMA((2,))],
    )
    return pl.pallas_call(
        functools.partial(_experts_kernel, d=d, tr=tr),
        grid_spec=grid_spec,
        out_shape=jax.ShapeDtypeStruct((n_rows * py, LANES), jnp.uint32),
        compiler_params=_cparams(("arbitrary",)),
        name="experts",
    )(cls_ea, cls_eb, cls_first, cls_nt, xs, wg, wg, wu, wu, wd, wd)


def _combine_kernel(pos_ref, posn_ref, h_ref, mod_ref, gf_ref, ys_ref, o_ref,
                    g_ref, sem, *, d, tm, n_steps, final_norm):
    i = pl.program_id(0)
    slot = i % 2
    py = d // 2 // LANES

    def issue(p_ref, s):
        def body(g, c):
            r0 = g * ROW_DMA_UNROLL
            idx = [p_ref[0, 0, r0 + u] for u in range(ROW_DMA_UNROLL)]
            for u in range(ROW_DMA_UNROLL):
                _row_copy(ys_ref, g_ref.at[s], idx[u], r0 + u, py,
                          sem.at[s]).start(priority=u % 2)
            return c

        lax.fori_loop(0, tm // ROW_DMA_UNROLL, body, 0)

    pl.when(i == 0)(lambda: issue(pos_ref, 0))
    pl.when(i + 1 < n_steps)(lambda: issue(posn_ref, 1 - slot))
    _wait_rows(ys_ref, g_ref.at[slot], tm, py, sem.at[slot])

    gate2 = mod_ref[:, 5 * d:6 * d]
    h = h_ref[...] + gate2 * _unpack_bf16_pairs(_load_panels(g_ref.at[slot], tm, py))
    if final_norm:
        ms = jnp.mean(h * h, axis=-1, keepdims=True)
        h = h * lax.rsqrt(ms + EPS) * gf_ref[...]
    o_ref[...] = h


def _combine_call(pos3, h1, mod3, gf, ys, seq, tm, final_norm):
    t, d = h1.shape
    per = seq // tm
    n_steps = t // tm
    pos_spec = lambda f: pl.BlockSpec((1, 1, tm), f, memory_space=pltpu.SMEM)
    return pl.pallas_call(
        functools.partial(_combine_kernel, d=d, tm=tm, n_steps=n_steps, final_norm=final_norm),
        grid=(n_steps,),
        in_specs=[pos_spec(lambda i: (i, 0, 0)),
                  pos_spec(lambda i: (jnp.minimum(i + 1, n_steps - 1), 0, 0)),
                  pl.BlockSpec((tm, d), lambda i: (i, 0)),
                  pl.BlockSpec((None, 1, mod3.shape[2]), lambda i: (i // per, 0, 0)),
                  pl.BlockSpec((1, d), lambda i: (0, 0)),
                  pl.BlockSpec(memory_space=pl.ANY)],
        out_specs=pl.BlockSpec((tm, d), lambda i: (i, 0)),
        out_shape=jax.ShapeDtypeStruct((t, d), f32),
        scratch_shapes=[pltpu.VMEM((2, tm * (d // 2 // LANES), LANES), jnp.uint32),
                        pltpu.SemaphoreType.DMA((2,))],
        compiler_params=_cparams(("arbitrary",)),
        name="combine",
    )(pos3, pos3, h1, mod3, gf, ys)


def _row_tile(seq, target):
    tm = min(target, seq)
    assert seq % tm == 0 and tm % 8 == 0
    return tm


def _layer(h2, mod3, p, bsz, seq, final_gain, final_norm):
    t, d = h2.shape
    g, n_p, n_h = p["ssm_b_re"].shape
    d_ssm = g * n_h
    d_pool = p["pool_scale"].shape[-1]
    n_grp = p["router_coarse_w"].shape[-1]
    n_exp = p["router_fine_w"].shape[-1]
    assert n_grp + n_exp <= ROUTE_LANES // 2 and seq % SSM_SLABS == 0
    tm = _row_tile(seq, 512)

    swap = lambda v: jnp.swapaxes(v, 1, 2)
    pw, bbd, cbd = _ssm_prep_call(
        p["ssm_lam_re"], p["ssm_lam_im"], p["ssm_log_dt"], swap(p["ssm_b_re"]),
        swap(p["ssm_b_im"]), swap(p["ssm_c_re"]), swap(p["ssm_c_im"]), SSM_SLABS)
    bbd, cbd = bbd.astype(bf16), cbd.astype(bf16)
    pw2 = pw.reshape(SSM_SLABS, 2 * g * n_p)

    w_in, b_in = p["w_in"], p["b_in"]
    n_sp = d_ssm + d_pool
    a3, b2, u2d = _mixers_call(h2, mod3, p["norm1_g"].reshape(1, d), w_in[:, :n_sp].astype(bf16),
                          b_in[:n_sp].reshape(1, n_sp), bbd, cbd, pw2,
                          p["ssm_d"].reshape(1, d_ssm),
                          p["ssm_w_glu"].astype(bf16), p["ssm_b_glu"].reshape(1, d_ssm),
                          p["pool_w"].astype(bf16), p["pool_scale"].reshape(1, d_pool),
                          bsz, seq, tm)

    wr = jnp.concatenate([p["router_coarse_w"], p["router_fine_w"]], axis=1)
    wr_hi = wr.astype(bf16)
    wr_lo = (wr - wr_hi.astype(f32)).astype(bf16)
    padc = ROUTE_LANES // 2 - wr.shape[1]
    wr_cat = jnp.concatenate([jnp.pad(wr_hi, ((0, 0), (0, padc))),
                              jnp.pad(wr_lo, ((0, 0), (0, padc)))], axis=1)
    br = jnp.pad(jnp.concatenate([p["router_coarse_b"], p["router_fine_b"]]),
                 (0, ROUTE_LANES - wr.shape[1])).reshape(1, ROUTE_LANES)

    moe_w = (p["moe_w_gate"], p["moe_w_up"], p["moe_w_down"])
    h1, up, metat, cnt, *moe_bf = _post_call(
        h2, a3, b2, mod3, u2d, p["norm2_g"].reshape(1, d),
        w_in[:, n_sp:].astype(bf16), b_in[n_sp:].reshape(1, 2 * d),
        p["w_proj_ssm"].astype(bf16), p["w_proj_pool"].astype(bf16), p["w_out"].astype(bf16),
        wr_cat, br, moe_w, seq, _row_tile(seq, 2 * tm), tm, n_grp, n_exp)
    moe_bf = [v.reshape(w.shape) for v, w in zip(moe_bf, moe_w)]

    tr = EXPERT_TILE_ROWS
    epg = n_exp // n_grp
    pairs = [(a, b) for a in range(epg) for b in range(a + 1, epg)]
    n_cls = n_grp * len(pairs)
    assert n_cls <= ROUTE_LANES
    cls_ea = jnp.asarray([gi * epg + a for gi in range(n_grp) for a, _ in pairs], jnp.int32)
    cls_eb = jnp.asarray([gi * epg + b for gi in range(n_grp) for _, b in pairs], jnp.int32)
    counts = cnt[0, 0:n_cls].astype(jnp.int32)
    padded = ((counts + tr - 1) // tr) * tr
    ends = jnp.cumsum(padded)
    offs = ends - padded
    ids = jnp.arange(n_cls, dtype=jnp.int32)
    cls = metat[0].astype(jnp.int32)
    off = jnp.sum(jnp.where(cls[None, :] == ids[:, None], offs[:, None], 0), axis=0)
    pos = off + metat[1].astype(jnp.int32)
    tms = _row_tile(seq, 2 * tm)
    pos3 = pos.reshape(t // tms, 1, tms)
    nt_max = t // tr + n_cls

    xs = _scatter_call(up, pos3, nt_max * tr, tms, ROW_PANELS)
    ys = _experts_call(cls_ea, cls_eb, offs // tr, padded // tr, xs, *moe_bf, tr)
    return _combine_call(pos.reshape(t // tm, 1, tm), h1, mod3, final_gain.reshape(1, d), ys,
                         seq, tm, final_norm)


def kernel(x, c, w_mod, b_mod, norm1_g, w_in, b_in, ssm_lam_re, ssm_lam_im, ssm_log_dt, ssm_b_re, ssm_b_im, ssm_c_re, ssm_c_im, ssm_d, ssm_w_glu, ssm_b_glu, pool_w, pool_scale, w_proj_ssm, w_proj_pool, w_out, norm2_g, router_coarse_w, router_coarse_b, router_fine_w, router_fine_b, moe_w_gate, moe_w_up, moe_w_down, norm_f_g):
    bsz, seq, d = x.shape
    depth = w_mod.shape[0]
    per_layer = dict(
        norm1_g=norm1_g, w_in=w_in, b_in=b_in, ssm_lam_re=ssm_lam_re, ssm_lam_im=ssm_lam_im,
        ssm_log_dt=ssm_log_dt, ssm_b_re=ssm_b_re, ssm_b_im=ssm_b_im, ssm_c_re=ssm_c_re,
        ssm_c_im=ssm_c_im, ssm_d=ssm_d, ssm_w_glu=ssm_w_glu, ssm_b_glu=ssm_b_glu, pool_w=pool_w,
        pool_scale=pool_scale, w_proj_ssm=w_proj_ssm, w_proj_pool=w_proj_pool, w_out=w_out,
        norm2_g=norm2_g, router_coarse_w=router_coarse_w, router_coarse_b=router_coarse_b,
        router_fine_w=router_fine_w, router_fine_b=router_fine_b, moe_w_gate=moe_w_gate,
        moe_w_up=moe_w_up, moe_w_down=moe_w_down)
    h2 = x.reshape(bsz * seq, d)
    for l in range(depth):
        p = {k: v[l] for k, v in per_layer.items()}
        mod3 = _mod_call(c, w_mod[l], b_mod[l]).reshape(bsz, 1, N_MOD * d)
        h2 = _layer(h2, mod3, p, bsz, seq, norm_f_g, final_norm=(l == depth - 1))
    return h2.reshape(bsz, seq, d)
```

```python
import functools
import math

import jax
import jax.numpy as jnp
from jax import lax
from jax.experimental import pallas as pl
from jax.experimental.pallas import tpu as pltpu

EPS = 1e-6
POOL_WINDOWS = (2, 4, 8, 16)
TOP_K_FINE = 2
N_MOD = 6
LANES = 128
SSM_SLABS = 8
ROUTE_LANES = 128
ROW_DMA_UNROLL = 16
ROW_PANELS = 5
EXPERT_TILE_ROWS = 256
VMEM_LIMIT = 56 * 1024 * 1024
NEG = -1e30

f32 = jnp.float32
bf16 = jnp.bfloat16


def _cparams(sem):
    return pltpu.CompilerParams(dimension_semantics=sem, vmem_limit_bytes=VMEM_LIMIT)


def _dot(a, b):
    return jnp.dot(a, b, preferred_element_type=f32)


def _pack_bf16_pairs(v):
    half = v.shape[1] // 2
    hi = lax.bitcast_convert_type(v[:, 0:half].astype(bf16).astype(f32), jnp.uint32)
    lo = lax.bitcast_convert_type(v[:, half:].astype(bf16).astype(f32), jnp.uint32)
    return hi | (lo >> 16)


def _unpack_bf16_pairs(w):
    hi = lax.bitcast_convert_type(w & jnp.uint32(0xFFFF0000), f32)
    lo = lax.bitcast_convert_type(w << 16, f32)
    return jnp.concatenate([hi, lo], axis=1)


def _sigmoid(v):
    return 0.5 * jnp.tanh(0.5 * v) + 0.5


def _dot_split_rows(a, b):
    h = a.shape[0] // 2
    return jnp.concatenate([_dot(a[:h], b), _dot(a[h:], b)], axis=0)


def _store_panels(ref, val, rows, row0=0):
    p = val.shape[1] // LANES
    for j in range(p):
        ref[pl.ds(row0 * p + j, rows, stride=p), :] = val[:, j * LANES:(j + 1) * LANES]


def _load_panels(ref, rows, p):
    return jnp.concatenate([ref[pl.ds(j, rows, stride=p), :] for j in range(p)], axis=1)


def _split_bf16(v):
    hi = v.astype(bf16)
    return hi, (v - hi.astype(f32)).astype(bf16)


def _mod_kernel(c_ref, w_ref, b_ref, o_ref):
    c = c_ref[...]
    a_hi, a_lo = _split_bf16(c * _sigmoid(c))
    w_hi, w_lo = _split_bf16(w_ref[...])
    o_ref[...] = _dot(a_hi, w_hi) + (_dot(a_lo, w_hi) + _dot(a_hi, w_lo)) + b_ref[...]


def _mod_call(c, w_mod, b_mod):
    bsz, d = c.shape
    n = w_mod.shape[1]
    tn = d
    return pl.pallas_call(
        _mod_kernel,
        grid=(n // tn,),
        in_specs=[pl.BlockSpec((bsz, d), lambda j: (0, 0)),
                  pl.BlockSpec((d, tn), lambda j: (0, j)),
                  pl.BlockSpec((1, tn), lambda j: (0, j))],
        out_specs=pl.BlockSpec((bsz, tn), lambda j: (0, j)),
        out_shape=jax.ShapeDtypeStruct((bsz, n), f32),
        compiler_params=_cparams(("parallel",)),
        name="mod",
    )(c, w_mod, b_mod.reshape(1, n))


def _ssm_prep_kernel(lr_ref, li_ref, ldt_ref, btr_ref, bti_ref, ctr_ref, cti_ref,
                     pw_ref, bbd_ref, cbd_ref, *, n_pow, n_grp):
    n_h, n_p = btr_ref.shape[1], btr_ref.shape[2]
    gp = n_grp * n_p
    lr = lr_ref[...]
    li = li_ref[...]
    dt = jnp.exp(ldt_ref[...])
    for k in range(1, n_pow + 1):
        mag = jnp.exp(lr * dt * float(k))
        ang = li * dt * float(k)
        pw_ref[k - 1, 0] = mag * jnp.cos(ang)
        pw_ref[k - 1, 1] = mag * jnp.sin(ang)
    lb_re = pw_ref[0, 0]
    lb_im = pw_ref[0, 1]
    den = lr * lr + li * li
    nr = lb_re - 1.0
    f_re = (nr * lr + lb_im * li) / den
    f_im = (lb_im * lr - nr * li) / den
    bbd_ref[...] = jnp.zeros_like(bbd_ref)
    cbd_ref[...] = jnp.zeros_like(cbd_ref)
    for g in range(n_grp):
        fr, fi = f_re[g:g + 1, :], f_im[g:g + 1, :]
        hs, ps = pl.ds(g * n_h, n_h), pl.ds(g * n_p, n_p)
        bbd_ref[hs, ps] = fr * btr_ref[g] - fi * bti_ref[g]
        bbd_ref[hs, pl.ds(gp + g * n_p, n_p)] = fr * bti_ref[g] + fi * btr_ref[g]
        cbd_ref[ps, hs] = ctr_ref[g]
        cbd_ref[pl.ds(gp + g * n_p, n_p), hs] = -cti_ref[g]


def _ssm_prep_call(lam_re, lam_im, log_dt, bt_re, bt_im, ct_re, ct_im, n_pow):
    g, p = lam_re.shape
    h = bt_re.shape[1]
    return pl.pallas_call(
        functools.partial(_ssm_prep_kernel, n_pow=n_pow, n_grp=g),
        out_shape=(jax.ShapeDtypeStruct((n_pow, 2, g, p), f32),
                   jax.ShapeDtypeStruct((g * h, 2 * g * p), f32),
                   jax.ShapeDtypeStruct((2 * g * p, g * h), f32)),
        name="ssm_prep",
    )(lam_re, lam_im, log_dt.reshape(g, 1), bt_re, bt_im, ct_re, ct_im)


def _modulated_norm(x, gain, shift, scale):
    ms = jnp.mean(x * x, axis=-1, keepdims=True)
    return x * lax.rsqrt(ms + EPS) * (gain * (1.0 + scale)) + shift


def _cmul_add(ar, ai, xr, xi, br, bi):
    return ar * xr - ai * xi + br, ar * xi + ai * xr + bi


def _mixers_kernel(x_ref, mod_ref, g1_ref, wsp_ref, bsp_ref,
                   bbd_ref, cbd_ref, pw_ref, dsk_ref, wglu_ref, bglu_ref, pw_pool_ref, psc_ref,
                   a_ref, b_ref, u_ref, us_ref, up_ref, e_ref, s_ref, y_ref,
                   *, seq, d, d_ssm, d_pool, n_state, slabs, ts):
    nc = seq // slabs
    nl = d_ssm // LANES
    ns = n_state

    for r0 in range(0, seq, ts):
        rows = pl.ds(r0, ts)
        u = _modulated_norm(x_ref[rows, :], g1_ref[...], mod_ref[:, 0:d], mod_ref[:, d:2 * d])
        u = u.astype(bf16)
        u_ref[rows, :] = u
        r = _dot(u, wsp_ref[...]) + bsp_ref[...]
        for j in range(nl):
            us_ref[j, rows, :] = r[:, j * LANES:(j + 1) * LANES]
        up_ref[rows, :] = r[:, d_ssm:]

    dsk = dsk_ref[...]
    l1r = pw_ref[0:1, 0:ns]
    l1i = pw_ref[0:1, ns:2 * ns]

    xr = xi = None
    for t in range(slabs):
        ut = jnp.concatenate([us_ref[j, pl.ds(t, nc, stride=slabs), :] for j in range(nl)], axis=1)
        bu = _dot(ut.astype(bf16), bbd_ref[...])
        br, bi = bu[:, 0:ns], bu[:, ns:2 * ns]
        if t == 0:
            xr, xi = br, bi
        else:
            xr, xi = _cmul_add(l1r, l1i, xr, xi, br, bi)
        xc = jnp.concatenate([xr, xi], axis=1).astype(bf16)
        y_ref[t] = _dot_split_rows(xc, cbd_ref[...]) + dsk * ut
    e_ref[:, 0:ns] = xr
    e_ref[:, ns:2 * ns] = xi

    lLr = pw_ref[slabs - 1:slabs, 0:ns]
    lLi = pw_ref[slabs - 1:slabs, ns:2 * ns]

    def chunk_step(c, carry):
        sr, si = carry
        s_ref[pl.ds(c, 1), 0:ns] = sr
        s_ref[pl.ds(c, 1), ns:2 * ns] = si
        er = e_ref[pl.ds(c, 1), 0:ns]
        ei = e_ref[pl.ds(c, 1), ns:2 * ns]
        return _cmul_add(lLr, lLi, sr, si, er, ei)

    zero = jnp.zeros((1, ns), f32)
    lax.fori_loop(0, nc, chunk_step, (zero, zero), unroll=8)

    sr = s_ref[:, 0:ns]
    si = s_ref[:, ns:2 * ns]
    for t in range(slabs):
        pr = pw_ref[t:t + 1, 0:ns]
        pi = pw_ref[t:t + 1, ns:2 * ns]
        zr = pr * sr - pi * si
        zi = pr * si + pi * sr
        zc = jnp.concatenate([zr, zi], axis=1).astype(bf16)
        y = y_ref[t] + _dot_split_rows(zc, cbd_ref[...])
        y = jax.nn.gelu(y)
        z = y * _sigmoid(_dot(y.astype(bf16), wglu_ref[...]) + bglu_ref[...])
        for j in range(nl):
            a_ref[j, pl.ds(t, nc, stride=slabs), :] = z[:, j * LANES:(j + 1) * LANES]

    gc = d_pool // len(POOL_WINDOWS)
    row = lax.broadcasted_iota(jnp.int32, (seq, gc), 0)
    for gi, w in enumerate(POOL_WINDOWS):
        lo = gi * gc
        v = up_ref[:, lo:lo + gc]
        acc = v
        span = 1
        while span < w:
            acc = acc + jnp.where(row >= span, pltpu.roll(acc, span, axis=0), 0.0)
            span *= 2
        cnt = jnp.minimum(row + 1, w).astype(f32)
        m = acc / cnt - v
        yg = _dot(m.astype(bf16), pw_pool_ref[gi])
        b_ref[:, lo:lo + gc] = (yg * psc_ref[:, lo:lo + gc]).astype(bf16)


def _mixers_call(x2, mod3, g1, w_sp, b_sp, bbd, cbd, pw, dsk, wglu, bglu, pool_w, pool_scale,
                 bsz, seq, ts):
    t, d = x2.shape
    d_ssm = dsk.shape[1]
    d_pool = pool_scale.shape[1]
    nl = d_ssm // LANES
    ns2 = bbd.shape[1]
    slabs = SSM_SLABS
    nc = seq // slabs
    kern = functools.partial(_mixers_kernel, seq=seq, d=d, d_ssm=d_ssm, d_pool=d_pool,
                             n_state=ns2 // 2, slabs=slabs, ts=ts)
    const2 = lambda b: (0, 0)
    return pl.pallas_call(
        kern,
        grid=(bsz,),
        in_specs=[pl.BlockSpec((seq, d), lambda b: (b, 0)),
                  pl.BlockSpec((None, 1, mod3.shape[2]), lambda b: (b, 0, 0)),
                  pl.BlockSpec((1, d), const2),
                  pl.BlockSpec(w_sp.shape, const2),
                  pl.BlockSpec(b_sp.shape, const2),
                  pl.BlockSpec(bbd.shape, const2),
                  pl.BlockSpec(cbd.shape, const2),
                  pl.BlockSpec(pw.shape, const2),
                  pl.BlockSpec(dsk.shape, const2),
                  pl.BlockSpec(wglu.shape, const2),
                  pl.BlockSpec(bglu.shape, const2),
                  pl.BlockSpec(pool_w.shape, lambda b: (0, 0, 0)),
                  pl.BlockSpec(pool_scale.shape, const2)],
        out_specs=[pl.BlockSpec((nl, seq, LANES), lambda b: (0, b, 0)),
                   pl.BlockSpec((seq, d_pool), lambda b: (b, 0)),
                   pl.BlockSpec((seq, d), lambda b: (b, 0))],
        out_shape=(jax.ShapeDtypeStruct((nl, t, LANES), f32),
                   jax.ShapeDtypeStruct((t, d_pool), bf16),
                   jax.ShapeDtypeStruct((t, d), bf16)),
        scratch_shapes=[pltpu.VMEM((nl, seq, LANES), f32),
                        pltpu.VMEM((seq, d_pool), f32),
                        pltpu.VMEM((nc, ns2), f32),
                        pltpu.VMEM((nc, ns2), f32),
                        pltpu.VMEM((slabs, nc, d_ssm), f32)],
        compiler_params=_cparams(("parallel",)),
        name="mixers",
    )(x2, mod3, g1, w_sp, b_sp, bbd, cbd, pw, dsk, wglu, bglu, pool_w, pool_scale)


def _post_kernel(x_ref, a_ref, b_ref, mod_ref, u_ref, g2_ref, wg_ref, bg_ref, wps_ref, wpp_ref,
                 wout_ref, wr_ref, br_ref, tri_ref, mg_ref, mu_ref, md_ref,
                 h_ref, up_ref, metat_ref, cnt_ref, mgo_ref, muo_ref, mdo_ref, carry_ref,
                 *, d, d_ssm, n_grp, n_exp, tm, ts):
    i = pl.program_id(0)

    @pl.when(i == 0)
    def _():
        carry_ref[...] = jnp.zeros_like(carry_ref)

    mgo_ref[...] = mg_ref[...].astype(bf16)
    muo_ref[...] = mu_ref[...].astype(bf16)
    mdo_ref[...] = md_ref[...].astype(bf16)

    for r0 in range(0, tm, ts):
        _post_rows(x_ref, a_ref, b_ref, mod_ref, u_ref, g2_ref, wg_ref, bg_ref, wps_ref, wpp_ref,
                   wout_ref, wr_ref, br_ref, tri_ref, h_ref, up_ref, metat_ref, carry_ref,
                   d=d, d_ssm=d_ssm, n_grp=n_grp, n_exp=n_exp, r0=r0, tm=ts)
    cnt_ref[...] = carry_ref[...]


def _post_rows(x_ref, a_ref, b_ref, mod_ref, u_ref, g2_ref, wg_ref, bg_ref, wps_ref, wpp_ref,
               wout_ref, wr_ref, br_ref, tri_ref, h_ref, up_ref, metat_ref, carry_ref,
               *, d, d_ssm, n_grp, n_exp, r0, tm):
    rows = pl.ds(r0, tm)
    x = x_ref[rows, :]
    gate1 = mod_ref[:, 2 * d:3 * d]
    shift2, scale2 = mod_ref[:, 3 * d:4 * d], mod_ref[:, 4 * d:5 * d]
    gates = _dot(u_ref[rows, :], wg_ref[...]) + bg_ref[...]
    a = jnp.concatenate([a_ref[j, rows, :] for j in range(d_ssm // LANES)], axis=1).astype(bf16)
    b = b_ref[rows, :]
    merged = (_sigmoid(gates[:, 0:d]) * _dot(a, wps_ref[...])
              + _sigmoid(gates[:, d:2 * d]) * _dot(b, wpp_ref[...]))
    h = x + gate1 * _dot(merged.astype(bf16), wout_ref[...])
    h_ref[rows, :] = h

    u2 = _modulated_norm(h, g2_ref[...], shift2, scale2)

    half = d // 2
    words = _pack_bf16_pairs(u2)

    u_hi = u2.astype(bf16)
    u_lo = (u2 - u_hi.astype(f32)).astype(bf16)
    r1 = _dot_split_rows(u_hi, wr_ref[...])
    r2 = _dot_split_rows(u_lo, wr_ref[...])
    lg = r1 + pltpu.roll(r1, ROUTE_LANES // 2, axis=1) + r2 + br_ref[...]

    lane = lax.broadcasted_iota(jnp.int32, (tm, ROUTE_LANES), 1).astype(f32)
    big = float(ROUTE_LANES)
    epg = float(n_exp // n_grp)
    is_c = lane < n_grp
    cl = jnp.where(is_c, lg, NEG)
    cmax = jnp.max(cl, axis=-1, keepdims=True)
    grp = jnp.min(jnp.where(cl == cmax, lane, big), axis=-1, keepdims=True)
    p_grp = 1.0 / jnp.sum(jnp.where(is_c, jnp.exp(cl - cmax), 0.0), axis=-1, keepdims=True)

    f_lo = n_grp + grp * epg
    fl = jnp.where((lane >= f_lo) & (lane < f_lo + epg), lg, NEG)
    f1 = jnp.max(fl, axis=-1, keepdims=True)
    i1 = jnp.min(jnp.where(fl == f1, lane, big), axis=-1, keepdims=True)
    fl2 = jnp.where(lane == i1, NEG, fl)
    f2 = jnp.max(fl2, axis=-1, keepdims=True)
    i2 = jnp.min(jnp.where(fl2 == f2, lane, big), axis=-1, keepdims=True)
    t2 = jnp.exp(f2 - f1)
    w0 = p_grp / (1.0 + t2)
    w1 = p_grp * t2 / (1.0 + t2)

    j0 = i1 - f_lo
    j1 = i2 - f_lo
    first = j0 < j1
    ja = jnp.minimum(j0, j1)
    jb = jnp.maximum(j0, j1)
    wa = jnp.where(first, w0, w1)
    wb = jnp.where(first, w1, w0)
    n_pair = epg * (epg - 1.0) * 0.5
    cls = grp * n_pair + ja * (2.0 * epg - ja - 1.0) * 0.5 + (jb - ja - 1.0)

    sel = lane == cls
    oh = jnp.where(sel, 1.0, 0.0)
    before = carry_ref[...] + _dot_split_rows(tri_ref[...], oh.astype(bf16))
    rank = jnp.sum(jnp.where(sel, before, 0.0), axis=-1, keepdims=True)
    carry_ref[...] = carry_ref[...] + jnp.sum(oh, axis=0, keepdims=True)

    meta = jnp.zeros((tm, ROUTE_LANES), f32)
    for k, val in enumerate((cls, rank, wa, wb)):
        meta = jnp.where(lane == float(k), val, meta)

    wts = jnp.where(lane == 0.0, wa, jnp.where(lane == 1.0, wb, 0.0))
    assert half + LANES == ROW_PANELS * LANES
    row = jnp.concatenate([words, lax.bitcast_convert_type(wts, jnp.uint32)], axis=1)
    _store_panels(up_ref, row, tm, r0)
    metat_ref[:, rows] = meta.T[0:8, :]


def _slice_per_step(w, steps):
    e, r, c = w.shape
    k = max(1, steps // e)
    assert (e * k) % steps == 0 and r % k == 0
    return w.reshape(e * k, r // k, c), (e * k) // steps


def _post_call(x2, a3, b2, mod3, u2d, g2, wg, bg, wps, wpp, wout, wr, br, moe_w, seq, tm, ts,
               n_grp, n_exp):
    t, d = x2.shape
    d_ssm = wps.shape[0]
    nl = a3.shape[0]
    per = seq // tm
    tri = jnp.tril(jnp.ones((ts, ts), bf16), -1)
    moe_v = [_slice_per_step(w, t // tm) for w in moe_w]
    moe_specs = [pl.BlockSpec((n,) + v.shape[1:], lambda i: (i, 0, 0)) for v, n in moe_v]
    kern = functools.partial(_post_kernel, d=d, d_ssm=d_ssm, n_grp=n_grp, n_exp=n_exp,
                             tm=tm, ts=ts)
    const2 = lambda i: (0, 0)
    wspec = lambda w: pl.BlockSpec(w.shape, const2, pipeline_mode=pl.Buffered(1))
    return pl.pallas_call(
        kern,
        grid=(t // tm,),
        in_specs=[pl.BlockSpec((tm, d), lambda i: (i, 0)),
                  pl.BlockSpec((nl, tm, LANES), lambda i: (0, i, 0)),
                  pl.BlockSpec((tm, b2.shape[1]), lambda i: (i, 0)),
                  pl.BlockSpec((None, 1, mod3.shape[2]), lambda i: (i // per, 0, 0)),
                  pl.BlockSpec((tm, d), lambda i: (i, 0)),
                  pl.BlockSpec((1, d), const2),
                  wspec(wg),
                  pl.BlockSpec(bg.shape, const2),
                  wspec(wps),
                  wspec(wpp),
                  wspec(wout),
                  wspec(wr),
                  pl.BlockSpec(br.shape, const2),
                  wspec(tri)] + moe_specs,
        out_specs=[pl.BlockSpec((tm, d), lambda i: (i, 0)),
                   pl.BlockSpec((tm * ROW_PANELS, LANES), lambda i: (i, 0)),
                   pl.BlockSpec((8, tm), lambda i: (0, i)),
                   pl.BlockSpec((1, ROUTE_LANES), const2)] + moe_specs,
        out_shape=(jax.ShapeDtypeStruct((t, d), f32),
                   jax.ShapeDtypeStruct((t * ROW_PANELS, LANES), jnp.uint32),
                   jax.ShapeDtypeStruct((8, t), f32),
                   jax.ShapeDtypeStruct((1, ROUTE_LANES), f32))
        + tuple(jax.ShapeDtypeStruct(v.shape, bf16) for v, _ in moe_v),
        scratch_shapes=[pltpu.VMEM((1, ROUTE_LANES), f32)],
        compiler_params=_cparams(("arbitrary",)),
        name="post",
    )(x2, a3, b2, mod3, u2d, g2, wg, bg, wps, wpp, wout, wr, br, tri, *[v for v, _ in moe_v])


def _row_copy(src_ref, dst_ref, src_row, dst_row, p, sem):
    src = src_ref.at[pl.ds(pl.multiple_of(src_row * p, p), p)]
    dst = dst_ref.at[pl.ds(pl.multiple_of(dst_row * p, p), p)]
    return pltpu.make_async_copy(src, dst, sem)


def _wait_rows(src_ref, dst_ref, n_rows, p, sem):
    pltpu.make_async_copy(src_ref.at[pl.ds(0, n_rows * p)], dst_ref.at[pl.ds(0, n_rows * p)],
                          sem).wait()


def _scatter_kernel(pos_ref, up_ref, xs_ref, sem, *, tm, p):
    def issue(g, c):
        r0 = g * ROW_DMA_UNROLL
        idx = [pos_ref[0, 0, r0 + u] for u in range(ROW_DMA_UNROLL)]
        for u in range(ROW_DMA_UNROLL):
            _row_copy(up_ref, xs_ref, r0 + u, idx[u], p, sem).start(priority=u % 2)
        return c

    lax.fori_loop(0, tm // ROW_DMA_UNROLL, issue, 0)
    _wait_rows(up_ref, xs_ref, tm, p, sem)


def _scatter_call(up, pos3, n_rows, tm, p):
    n_steps = up.shape[0] // (tm * p)
    return pl.pallas_call(
        functools.partial(_scatter_kernel, tm=tm, p=p),
        grid=(n_steps,),
        in_specs=[pl.BlockSpec((1, 1, tm), lambda i: (i, 0, 0), memory_space=pltpu.SMEM),
                  pl.BlockSpec((tm * p, LANES), lambda i: (i, 0))],
        out_specs=pl.BlockSpec(memory_space=pl.ANY),
        out_shape=jax.ShapeDtypeStruct((n_rows * p, LANES), up.dtype),
        scratch_shapes=[pltpu.SemaphoreType.DMA(())],
        compiler_params=_cparams(("arbitrary",)),
        name="scatter_rows",
    )(pos3, up)


def _experts_kernel(ea_ref, eb_ref, first_ref, nt_ref, xs_ref, wg_a_ref, wg_b_ref, wu_a_ref,
                    wu_b_ref, wd_a_ref, wd_b_ref, ys_ref, xbuf, ybuf, sem_in, sem_out,
                    *, d, tr):
    c = pl.program_id(0)
    n_cls = pl.num_programs(0)
    t0 = first_ref[c]
    t1 = t0 + nt_ref[c]
    total = first_ref[n_cls - 1] + nt_ref[n_cls - 1]
    px = d // 2 // LANES
    py = px
    rin, rout = tr * ROW_PANELS, tr * py

    def in_copy(g):
        src = xs_ref.at[pl.ds(pl.multiple_of(g * rin, rin), rin)]
        return pltpu.make_async_copy(src, xbuf.at[g % 2], sem_in.at[g % 2])

    def out_copy(g):
        dst = ys_ref.at[pl.ds(pl.multiple_of(g * rout, rout), rout)]
        return pltpu.make_async_copy(ybuf.at[g % 2], dst, sem_out.at[g % 2])

    pl.when((c == 0) & (total > 0))(lambda: in_copy(0).start())

    def tile(g, carry):
        slot = g % 2
        in_copy(g).wait()
        pl.when(g + 1 < total)(lambda: in_copy(g + 1).start())
        pl.when(g >= 2)(lambda: out_copy(g - 2).wait())

        xt = xbuf.at[slot]
        w = jnp.concatenate([xt[pl.ds(k, tr, stride=ROW_PANELS), :] for k in range(px)], axis=1)
        wts = lax.bitcast_convert_type(xt[pl.ds(px, tr, stride=ROW_PANELS), :], f32)
        x = _unpack_bf16_pairs(w).astype(bf16)

        def ffn(wg_ref, wu_ref, wd_ref, wt):
            h1 = _dot(x, wg_ref[...])
            act = h1 * _sigmoid(h1) * _dot(x, wu_ref[...])
            return wt * _dot(act.astype(bf16), wd_ref[...])

        y = (ffn(wg_a_ref, wu_a_ref, wd_a_ref, wts[:, 0:1])
             + ffn(wg_b_ref, wu_b_ref, wd_b_ref, wts[:, 1:2]))
        _store_panels(ybuf.at[slot], _pack_bf16_pairs(y), tr)
        out_copy(g).start()
        return carry

    lax.fori_loop(t0, t1, tile, 0)

    @pl.when(c == n_cls - 1)
    def _():
        pl.when(total >= 2)(lambda: out_copy(total - 2).wait())
        pl.when(total >= 1)(lambda: out_copy(total - 1).wait())


def _experts_call(cls_ea, cls_eb, cls_first, cls_nt, xs, wg, wu, wd, tr):
    n_exp, d, d_exp = wg.shape
    n_cls = cls_ea.shape[0]
    n_rows = xs.shape[0] // ROW_PANELS
    py = d // 2 // LANES

    def wa_map(c, ea, eb, first, nt):
        return (ea[c], 0, 0)

    def wb_map(c, ea, eb, first, nt):
        return (eb[c], 0, 0)

    grid_spec = pltpu.PrefetchScalarGridSpec(
        num_scalar_prefetch=4,
        grid=(n_cls,),
        in_specs=[pl.BlockSpec(memory_space=pl.ANY),
                  pl.BlockSpec((None, d, d_exp), wa_map),
                  pl.BlockSpec((None, d, d_exp), wb_map),
                  pl.BlockSpec((None, d, d_exp), wa_map),
                  pl.BlockSpec((None, d, d_exp), wb_map),
                  pl.BlockSpec((None, d_exp, d), wa_map),
                  pl.BlockSpec((None, d_exp, d), wb_map)],
        out_specs=pl.BlockSpec(memory_space=pl.ANY),
        scratch_shapes=[pltpu.VMEM((2, tr * ROW_PANELS, LANES), jnp.uint32),
                        pltpu.VMEM((2, tr * py, LANES), jnp.uint32),
                        pltpu.SemaphoreType.DMA((2,)),
                        pltpu.SemaphoreType.DMA((2,))],
    )
    return pl.pallas_call(
        functools.partial(_experts_kernel, d=d, tr=tr),
        grid_spec=grid_spec,
        out_shape=jax.ShapeDtypeStruct((n_rows * py, LANES), jnp.uint32),
        compiler_params=_cparams(("arbitrary",)),
        name="experts",
    )(cls_ea, cls_eb, cls_first, cls_nt, xs, wg, wg, wu, wu, wd, wd)


def _combine_kernel(pos_ref, posn_ref, h_ref, mod_ref, gf_ref, ys_ref, o_ref,
                    g_ref, sem, *, d, tm, n_steps, final_norm):
    i = pl.program_id(0)
    slot = i % 2
    py = d // 2 // LANES

    def issue(p_ref, s):
        def body(g, c):
            r0 = g * ROW_DMA_UNROLL
            idx = [p_ref[0, 0, r0 + u] for u in range(ROW_DMA_UNROLL)]
            for u in range(ROW_DMA_UNROLL):
                _row_copy(ys_ref, g_ref.at[s], idx[u], r0 + u, py,
                          sem.at[s]).start(priority=u % 2)
            return c

        lax.fori_loop(0, tm // ROW_DMA_UNROLL, body, 0)

    pl.when(i == 0)(lambda: issue(pos_ref, 0))
    pl.when(i + 1 < n_steps)(lambda: issue(posn_ref, 1 - slot))
    _wait_rows(ys_ref, g_ref.at[slot], tm, py, sem.at[slot])

    gate2 = mod_ref[:, 5 * d:6 * d]
    h = h_ref[...] + gate2 * _unpack_bf16_pairs(_load_panels(g_ref.at[slot], tm, py))
    if final_norm:
        ms = jnp.mean(h * h, axis=-1, keepdims=True)
        h = h * lax.rsqrt(ms + EPS) * gf_ref[...]
    o_ref[...] = h


def _combine_call(pos3, h1, mod3, gf, ys, seq, tm, final_norm):
    t, d = h1.shape
    per = seq // tm
    n_steps = t // tm
    pos_spec = lambda f: pl.BlockSpec((1, 1, tm), f, memory_space=pltpu.SMEM)
    return pl.pallas_call(
        functools.partial(_combine_kernel, d=d, tm=tm, n_steps=n_steps, final_norm=final_norm),
        grid=(n_steps,),
        in_specs=[pos_spec(lambda i: (i, 0, 0)),
                  pos_spec(lambda i: (jnp.minimum(i + 1, n_steps - 1), 0, 0)),
                  pl.BlockSpec((tm, d), lambda i: (i, 0)),
                  pl.BlockSpec((None, 1, mod3.shape[2]), lambda i: (i // per, 0, 0)),
                  pl.BlockSpec((1, d), lambda i: (0, 0)),
                  pl.BlockSpec(memory_space=pl.ANY)],
        out_specs=pl.BlockSpec((tm, d), lambda i: (i, 0)),
        out_shape=jax.ShapeDtypeStruct((t, d), f32),
        scratch_shapes=[pltpu.VMEM((2, tm * (d // 2 // LANES), LANES), jnp.uint32),
                        pltpu.SemaphoreType.DMA((2,))],
        compiler_params=_cparams(("arbitrary",)),
        name="combine",
    )(pos3, pos3, h1, mod3, gf, ys)


def _row_tile(seq, target):
    tm = min(target, seq)
    assert seq % tm == 0 and tm % 8 == 0
    return tm


def _layer(h2, mod3, p, bsz, seq, final_gain, final_norm):
    t, d = h2.shape
    g, n_p, n_h = p["ssm_b_re"].shape
    d_ssm = g * n_h
    d_pool = p["pool_scale"].shape[-1]
    n_grp = p["router_coarse_w"].shape[-1]
    n_exp = p["router_fine_w"].shape[-1]
    assert n_grp + n_exp <= ROUTE_LANES // 2 and seq % SSM_SLABS == 0
    tm = _row_tile(seq, 512)

    swap = lambda v: jnp.swapaxes(v, 1, 2)
    pw, bbd, cbd = _ssm_prep_call(
        p["ssm_lam_re"], p["ssm_lam_im"], p["ssm_log_dt"], swap(p["ssm_b_re"]),
        swap(p["ssm_b_im"]), swap(p["ssm_c_re"]), swap(p["ssm_c_im"]), SSM_SLABS)
    bbd, cbd = bbd.astype(bf16), cbd.astype(bf16)
    pw2 = pw.reshape(SSM_SLABS, 2 * g * n_p)

    w_in, b_in = p["w_in"], p["b_in"]
    n_sp = d_ssm + d_pool
    a3, b2, u2d = _mixers_call(h2, mod3, p["norm1_g"].reshape(1, d), w_in[:, :n_sp].astype(bf16),
                          b_in[:n_sp].reshape(1, n_sp), bbd, cbd, pw2,
                          p["ssm_d"].reshape(1, d_ssm),
                          p["ssm_w_glu"].astype(bf16), p["ssm_b_glu"].reshape(1, d_ssm),
                          p["pool_w"].astype(bf16), p["pool_scale"].reshape(1, d_pool),
                          bsz, seq, tm)

    wr = jnp.concatenate([p["router_coarse_w"], p["router_fine_w"]], axis=1)
    wr_hi = wr.astype(bf16)
    wr_lo = (wr - wr_hi.astype(f32)).astype(bf16)
    padc = ROUTE_LANES // 2 - wr.shape[1]
    wr_cat = jnp.concatenate([jnp.pad(wr_hi, ((0, 0), (0, padc))),
                              jnp.pad(wr_lo, ((0, 0), (0, padc)))], axis=1)
    br = jnp.pad(jnp.concatenate([p["router_coarse_b"], p["router_fine_b"]]),
                 (0, ROUTE_LANES - wr.shape[1])).reshape(1, ROUTE_LANES)

    moe_w = (p["moe_w_gate"], p["moe_w_up"], p["moe_w_down"])
    h1, up, metat, cnt, *moe_bf = _post_call(
        h2, a3, b2, mod3, u2d, p["norm2_g"].reshape(1, d),
        w_in[:, n_sp:].astype(bf16), b_in[n_sp:].reshape(1, 2 * d),
        p["w_proj_ssm"].astype(bf16), p["w_proj_pool"].astype(bf16), p["w_out"].astype(bf16),
        wr_cat, br, moe_w, seq, _row_tile(seq, 2 * tm), tm, n_grp, n_exp)
    moe_bf = [v.reshape(w.shape) for v, w in zip(moe_bf, moe_w)]

    tr = EXPERT_TILE_ROWS
    epg = n_exp // n_grp
    pairs = [(a, b) for a in range(epg) for b in range(a + 1, epg)]
    n_cls = n_grp * len(pairs)
    assert n_cls <= ROUTE_LANES
    cls_ea = jnp.asarray([gi * epg + a for gi in range(n_grp) for a, _ in pairs], jnp.int32)
    cls_eb = jnp.asarray([gi * epg + b for gi in range(n_grp) for _, b in pairs], jnp.int32)
    counts = cnt[0, 0:n_cls].astype(jnp.int32)
    padded = ((counts + tr - 1) // tr) * tr
    ends = jnp.cumsum(padded)
    offs = ends - padded
    ids = jnp.arange(n_cls, dtype=jnp.int32)
    cls = metat[0].astype(jnp.int32)
    off = jnp.sum(jnp.where(cls[None, :] == ids[:, None], offs[:, None], 0), axis=0)
    pos = off + metat[1].astype(jnp.int32)
    tms = _row_tile(seq, 2 * tm)
    pos3 = pos.reshape(t // tms, 1, tms)
    nt_max = t // tr + n_cls

    xs = _scatter_call(up, pos3, nt_max * tr, tms, ROW_PANELS)
    ys = _experts_call(cls_ea, cls_eb, offs // tr, padded // tr, xs, *moe_bf, tr)
    return _combine_call(pos.reshape(t // tm, 1, tm), h1, mod3, final_gain.reshape(1, d), ys,
                         seq, tm, final_norm)


def kernel(x, c, w_mod, b_mod, norm1_g, w_in, b_in, ssm_lam_re, ssm_lam_im, ssm_log_dt, ssm_b_re, ssm_b_im, ssm_c_re, ssm_c_im, ssm_d, ssm_w_glu, ssm_b_glu, pool_w, pool_scale, w_proj_ssm, w_proj_pool, w_out, norm2_g, router_coarse_w, router_coarse_b, router_fine_w, router_fine_b, moe_w_gate, moe_w_up, moe_w_down, norm_f_g):
    bsz, seq, d = x.shape
    depth = w_mod.shape[0]
    per_layer = dict(
        norm1_g=norm1_g, w_in=w_in, b_in=b_in, ssm_lam_re=ssm_lam_re, ssm_lam_im=ssm_lam_im,
        ssm_log_dt=ssm_log_dt, ssm_b_re=ssm_b_re, ssm_b_im=ssm_b_im, ssm_c_re=ssm_c_re,
        ssm_c_im=ssm_c_im, ssm_d=ssm_d, ssm_w_glu=ssm_w_glu, ssm_b_glu=ssm_b_glu, pool_w=pool_w,
        pool_scale=pool_scale, w_proj_ssm=w_proj_ssm, w_proj_pool=w_proj_pool, w_out=w_out,
        norm2_g=norm2_g, router_coarse_w=router_coarse_w, router_coarse_b=router_coarse_b,
        router_fine_w=router_fine_w, router_fine_b=router_fine_b, moe_w_gate=moe_w_gate,
        moe_w_up=moe_w_up, moe_w_down=moe_w_down)
    h2 = x.reshape(bsz * seq, d)
    for l in range(depth):
        p = {k: v[l] for k, v in per_layer.items()}
        mod3 = _mod_call(c, w_mod[l], b_mod[l]).reshape(bsz, 1, N_MOD * d)
        h2 = _layer(h2, mod3, p, bsz, seq, norm_f_g, final_norm=(l == depth - 1))
    return h2.reshape(bsz, seq, d)
```

```python
import functools
import math

import jax
import jax.numpy as jnp
from jax import lax
from jax.experimental import pallas as pl
from jax.experimental.pallas import tpu as pltpu

EPS = 1e-6
POOL_WINDOWS = (2, 4, 8, 16)
TOP_K_FINE = 2
N_MOD = 6
LANES = 128
SSM_SLABS = 8
ROUTE_LANES = 128
ROW_DMA_UNROLL = 16
ROW_PANELS = 5
EXPERT_TILE_ROWS = 256
VMEM_LIMIT = 58 * 1024 * 1024
NEG = -1e30

f32 = jnp.float32
bf16 = jnp.bfloat16


def _cparams(sem):
    return pltpu.CompilerParams(dimension_semantics=sem, vmem_limit_bytes=VMEM_LIMIT)


def _dot(a, b):
    return jnp.dot(a, b, preferred_element_type=f32)


def _pack_bf16_pairs(v):
    half = v.shape[1] // 2
    hi = lax.bitcast_convert_type(v[:, 0:half].astype(bf16).astype(f32), jnp.uint32)
    lo = lax.bitcast_convert_type(v[:, half:].astype(bf16).astype(f32), jnp.uint32)
    return hi | (lo >> 16)


def _unpack_bf16_pairs(w):
    hi = lax.bitcast_convert_type(w & jnp.uint32(0xFFFF0000), f32)
    lo = lax.bitcast_convert_type(w << 16, f32)
    return jnp.concatenate([hi, lo], axis=1)


def _sigmoid(v):
    return 0.5 * jnp.tanh(0.5 * v) + 0.5


def _dot_split_rows(a, b):
    h = a.shape[0] // 2
    return jnp.concatenate([_dot(a[:h], b), _dot(a[h:], b)], axis=0)


def _store_panels(ref, val, rows, row0=0):
    p = val.shape[1] // LANES
    for j in range(p):
        ref[pl.ds(row0 * p + j, rows, stride=p), :] = val[:, j * LANES:(j + 1) * LANES]


def _load_panels(ref, rows, p):
    return jnp.concatenate([ref[pl.ds(j, rows, stride=p), :] for j in range(p)], axis=1)


def _split_bf16(v):
    hi = v.astype(bf16)
    return hi, (v - hi.astype(f32)).astype(bf16)


def _mod_kernel(c_ref, w_ref, b_ref, o_ref):
    c = c_ref[...]
    a_hi, a_lo = _split_bf16(c * _sigmoid(c))
    w_hi, w_lo = _split_bf16(w_ref[...])
    o_ref[...] = _dot(a_hi, w_hi) + (_dot(a_lo, w_hi) + _dot(a_hi, w_lo)) + b_ref[...]


def _mod_call(c, w_mod, b_mod):
    bsz, d = c.shape
    n = w_mod.shape[1]
    tn = d
    return pl.pallas_call(
        _mod_kernel,
        grid=(n // tn,),
        in_specs=[pl.BlockSpec((bsz, d), lambda j: (0, 0)),
                  pl.BlockSpec((d, tn), lambda j: (0, j)),
                  pl.BlockSpec((1, tn), lambda j: (0, j))],
        out_specs=pl.BlockSpec((bsz, tn), lambda j: (0, j)),
        out_shape=jax.ShapeDtypeStruct((bsz, n), f32),
        compiler_params=_cparams(("parallel",)),
        name="mod",
    )(c, w_mod, b_mod.reshape(1, n))


def _ssm_prep_kernel(lr_ref, li_ref, ldt_ref, btr_ref, bti_ref, ctr_ref, cti_ref,
                     pw_ref, bbd_ref, cbd_ref, *, n_pow, n_grp):
    n_h, n_p = btr_ref.shape[1], btr_ref.shape[2]
    gp = n_grp * n_p
    lr = lr_ref[...]
    li = li_ref[...]
    dt = jnp.exp(ldt_ref[...])
    for k in range(1, n_pow + 1):
        mag = jnp.exp(lr * dt * float(k))
        ang = li * dt * float(k)
        pw_ref[k - 1, 0] = mag * jnp.cos(ang)
        pw_ref[k - 1, 1] = mag * jnp.sin(ang)
    lb_re = pw_ref[0, 0]
    lb_im = pw_ref[0, 1]
    den = lr * lr + li * li
    nr = lb_re - 1.0
    f_re = (nr * lr + lb_im * li) / den
    f_im = (lb_im * lr - nr * li) / den
    bbd_ref[...] = jnp.zeros_like(bbd_ref)
    cbd_ref[...] = jnp.zeros_like(cbd_ref)
    for g in range(n_grp):
        fr, fi = f_re[g:g + 1, :], f_im[g:g + 1, :]
        hs, ps = pl.ds(g * n_h, n_h), pl.ds(g * n_p, n_p)
        bbd_ref[hs, ps] = fr * btr_ref[g] - fi * bti_ref[g]
        bbd_ref[hs, pl.ds(gp + g * n_p, n_p)] = fr * bti_ref[g] + fi * btr_ref[g]
        cbd_ref[ps, hs] = ctr_ref[g]
        cbd_ref[pl.ds(gp + g * n_p, n_p), hs] = -cti_ref[g]


def _ssm_prep_call(lam_re, lam_im, log_dt, bt_re, bt_im, ct_re, ct_im, n_pow):
    g, p = lam_re.shape
    h = bt_re.shape[1]
    return pl.pallas_call(
        functools.partial(_ssm_prep_kernel, n_pow=n_pow, n_grp=g),
        out_shape=(jax.ShapeDtypeStruct((n_pow, 2, g, p), f32),
                   jax.ShapeDtypeStruct((g * h, 2 * g * p), f32),
                   jax.ShapeDtypeStruct((2 * g * p, g * h), f32)),
        name="ssm_prep",
    )(lam_re, lam_im, log_dt.reshape(g, 1), bt_re, bt_im, ct_re, ct_im)


def _modulated_norm(x, gain, shift, scale):
    ms = jnp.mean(x * x, axis=-1, keepdims=True)
    return x * lax.rsqrt(ms + EPS) * (gain * (1.0 + scale)) + shift


def _cmul_add(ar, ai, xr, xi, br, bi):
    return ar * xr - ai * xi + br, ar * xi + ai * xr + bi


def _mixers_kernel(x_ref, mod_ref, g1_ref, wsp_ref, bsp_ref,
                   bbd_ref, cbd_ref, pw_ref, dsk_ref, wglu_ref, bglu_ref, pw_pool_ref, psc_ref,
                   a_ref, b_ref, u_ref, us_ref, up_ref, e_ref, s_ref, y_ref,
                   *, seq, d, d_ssm, d_pool, n_state, slabs, ts):
    nc = seq // slabs
    nl = d_ssm // LANES
    ns = n_state

    for r0 in range(0, seq, ts):
        rows = pl.ds(r0, ts)
        u = _modulated_norm(x_ref[rows, :], g1_ref[...], mod_ref[:, 0:d], mod_ref[:, d:2 * d])
        u = u.astype(bf16)
        u_ref[rows, :] = u
        r = _dot(u, wsp_ref[...]) + bsp_ref[...]
        for j in range(nl):
            us_ref[j, rows, :] = r[:, j * LANES:(j + 1) * LANES]
        up_ref[rows, :] = r[:, d_ssm:]

    dsk = dsk_ref[...]
    l1r = pw_ref[0:1, 0:ns]
    l1i = pw_ref[0:1, ns:2 * ns]

    xr = xi = None
    for t in range(slabs):
        ut = jnp.concatenate([us_ref[j, pl.ds(t, nc, stride=slabs), :] for j in range(nl)], axis=1)
        bu = _dot(ut.astype(bf16), bbd_ref[...])
        br, bi = bu[:, 0:ns], bu[:, ns:2 * ns]
        if t == 0:
            xr, xi = br, bi
        else:
            xr, xi = _cmul_add(l1r, l1i, xr, xi, br, bi)
        xc = jnp.concatenate([xr, xi], axis=1).astype(bf16)
        y_ref[t] = _dot_split_rows(xc, cbd_ref[...]) + dsk * ut
    e_ref[:, 0:ns] = xr
    e_ref[:, ns:2 * ns] = xi

    lLr = pw_ref[slabs - 1:slabs, 0:ns]
    lLi = pw_ref[slabs - 1:slabs, ns:2 * ns]

    def chunk_step(c, carry):
        sr, si = carry
        s_ref[pl.ds(c, 1), 0:ns] = sr
        s_ref[pl.ds(c, 1), ns:2 * ns] = si
        er = e_ref[pl.ds(c, 1), 0:ns]
        ei = e_ref[pl.ds(c, 1), ns:2 * ns]
        return _cmul_add(lLr, lLi, sr, si, er, ei)

    zero = jnp.zeros((1, ns), f32)
    lax.fori_loop(0, nc, chunk_step, (zero, zero), unroll=8)

    sr = s_ref[:, 0:ns]
    si = s_ref[:, ns:2 * ns]
    for t in range(slabs):
        pr = pw_ref[t:t + 1, 0:ns]
        pi = pw_ref[t:t + 1, ns:2 * ns]
        zr = pr * sr - pi * si
        zi = pr * si + pi * sr
        zc = jnp.concatenate([zr, zi], axis=1).astype(bf16)
        y = y_ref[t] + _dot_split_rows(zc, cbd_ref[...])
        y = jax.nn.gelu(y)
        z = y * _sigmoid(_dot(y.astype(bf16), wglu_ref[...]) + bglu_ref[...])
        for j in range(nl):
            a_ref[j, pl.ds(t, nc, stride=slabs), :] = z[:, j * LANES:(j + 1) * LANES]

    gc = d_pool // len(POOL_WINDOWS)
    row = lax.broadcasted_iota(jnp.int32, (seq, gc), 0)
    for gi, w in enumerate(POOL_WINDOWS):
        lo = gi * gc
        v = up_ref[:, lo:lo + gc]
        acc = v
        span = 1
        while span < w:
            acc = acc + jnp.where(row >= span, pltpu.roll(acc, span, axis=0), 0.0)
            span *= 2
        cnt = jnp.minimum(row + 1, w).astype(f32)
        m = acc / cnt - v
        yg = _dot(m.astype(bf16), pw_pool_ref[gi])
        b_ref[:, lo:lo + gc] = (yg * psc_ref[:, lo:lo + gc]).astype(bf16)


def _mixers_call(x2, mod3, g1, w_sp, b_sp, bbd, cbd, pw, dsk, wglu, bglu, pool_w, pool_scale,
                 bsz, seq, ts):
    t, d = x2.shape
    d_ssm = dsk.shape[1]
    d_pool = pool_scale.shape[1]
    nl = d_ssm // LANES
    ns2 = bbd.shape[1]
    slabs = SSM_SLABS
    nc = seq // slabs
    kern = functools.partial(_mixers_kernel, seq=seq, d=d, d_ssm=d_ssm, d_pool=d_pool,
                             n_state=ns2 // 2, slabs=slabs, ts=ts)
    const2 = lambda b: (0, 0)
    return pl.pallas_call(
        kern,
        grid=(bsz,),
        in_specs=[pl.BlockSpec((seq, d), lambda b: (b, 0)),
                  pl.BlockSpec((None, 1, mod3.shape[2]), lambda b: (b, 0, 0)),
                  pl.BlockSpec((1, d), const2),
                  pl.BlockSpec(w_sp.shape, const2),
                  pl.BlockSpec(b_sp.shape, const2),
                  pl.BlockSpec(bbd.shape, const2),
                  pl.BlockSpec(cbd.shape, const2),
                  pl.BlockSpec(pw.shape, const2),
                  pl.BlockSpec(dsk.shape, const2),
                  pl.BlockSpec(wglu.shape, const2),
                  pl.BlockSpec(bglu.shape, const2),
                  pl.BlockSpec(pool_w.shape, lambda b: (0, 0, 0)),
                  pl.BlockSpec(pool_scale.shape, const2)],
        out_specs=[pl.BlockSpec((nl, seq, LANES), lambda b: (0, b, 0)),
                   pl.BlockSpec((seq, d_pool), lambda b: (b, 0)),
                   pl.BlockSpec((seq, d), lambda b: (b, 0))],
        out_shape=(jax.ShapeDtypeStruct((nl, t, LANES), f32),
                   jax.ShapeDtypeStruct((t, d_pool), bf16),
                   jax.ShapeDtypeStruct((t, d), bf16)),
        scratch_shapes=[pltpu.VMEM((nl, seq, LANES), f32),
                        pltpu.VMEM((seq, d_pool), f32),
                        pltpu.VMEM((nc, ns2), f32),
                        pltpu.VMEM((nc, ns2), f32),
                        pltpu.VMEM((slabs, nc, d_ssm), f32)],
        compiler_params=_cparams(("parallel",)),
        name="mixers",
    )(x2, mod3, g1, w_sp, b_sp, bbd, cbd, pw, dsk, wglu, bglu, pool_w, pool_scale)


def _post_kernel(x_ref, a_ref, b_ref, mod_ref, u_ref, g2_ref, wg_ref, bg_ref, wps_ref, wpp_ref,
                 wout_ref, wr_ref, br_ref, tri_ref, mg_ref, mu_ref, md_ref,
                 h_ref, up_ref, metat_ref, cnt_ref, mgo_ref, muo_ref, mdo_ref, carry_ref,
                 *, d, d_ssm, n_grp, n_exp, tm, ts):
    i = pl.program_id(0)

    @pl.when(i == 0)
    def _():
        carry_ref[...] = jnp.zeros_like(carry_ref)

    mgo_ref[...] = mg_ref[...].astype(bf16)
    muo_ref[...] = mu_ref[...].astype(bf16)
    mdo_ref[...] = md_ref[...].astype(bf16)

    for r0 in range(0, tm, ts):
        _post_rows(x_ref, a_ref, b_ref, mod_ref, u_ref, g2_ref, wg_ref, bg_ref, wps_ref, wpp_ref,
                   wout_ref, wr_ref, br_ref, tri_ref, h_ref, up_ref, metat_ref, carry_ref,
                   d=d, d_ssm=d_ssm, n_grp=n_grp, n_exp=n_exp, r0=r0, tm=ts)
    cnt_ref[...] = carry_ref[...]


def _post_rows(x_ref, a_ref, b_ref, mod_ref, u_ref, g2_ref, wg_ref, bg_ref, wps_ref, wpp_ref,
               wout_ref, wr_ref, br_ref, tri_ref, h_ref, up_ref, metat_ref, carry_ref,
               *, d, d_ssm, n_grp, n_exp, r0, tm):
    rows = pl.ds(r0, tm)
    x = x_ref[rows, :]
    gate1 = mod_ref[:, 2 * d:3 * d]
    shift2, scale2 = mod_ref[:, 3 * d:4 * d], mod_ref[:, 4 * d:5 * d]
    gates = _dot(u_ref[rows, :], wg_ref[...]) + bg_ref[...]
    a = jnp.concatenate([a_ref[j, rows, :] for j in range(d_ssm // LANES)], axis=1).astype(bf16)
    b = b_ref[rows, :]
    merged = (_sigmoid(gates[:, 0:d]) * _dot(a, wps_ref[...])
              + _sigmoid(gates[:, d:2 * d]) * _dot(b, wpp_ref[...]))
    h = x + gate1 * _dot(merged.astype(bf16), wout_ref[...])
    h_ref[rows, :] = h

    u2 = _modulated_norm(h, g2_ref[...], shift2, scale2)

    half = d // 2
    words = _pack_bf16_pairs(u2)

    u_hi = u2.astype(bf16)
    u_lo = (u2 - u_hi.astype(f32)).astype(bf16)
    r1 = _dot_split_rows(u_hi, wr_ref[...])
    r2 = _dot_split_rows(u_lo, wr_ref[...])
    lg = r1 + pltpu.roll(r1, ROUTE_LANES // 2, axis=1) + r2 + br_ref[...]

    lane = lax.broadcasted_iota(jnp.int32, (tm, ROUTE_LANES), 1).astype(f32)
    big = float(ROUTE_LANES)
    epg = float(n_exp // n_grp)
    is_c = lane < n_grp
    cl = jnp.where(is_c, lg, NEG)
    cmax = jnp.max(cl, axis=-1, keepdims=True)
    grp = jnp.min(jnp.where(cl == cmax, lane, big), axis=-1, keepdims=True)
    p_grp = 1.0 / jnp.sum(jnp.where(is_c, jnp.exp(cl - cmax), 0.0), axis=-1, keepdims=True)

    f_lo = n_grp + grp * epg
    fl = jnp.where((lane >= f_lo) & (lane < f_lo + epg), lg, NEG)
    f1 = jnp.max(fl, axis=-1, keepdims=True)
    i1 = jnp.min(jnp.where(fl == f1, lane, big), axis=-1, keepdims=True)
    fl2 = jnp.where(lane == i1, NEG, fl)
    f2 = jnp.max(fl2, axis=-1, keepdims=True)
    i2 = jnp.min(jnp.where(fl2 == f2, lane, big), axis=-1, keepdims=True)
    t2 = jnp.exp(f2 - f1)
    w0 = p_grp / (1.0 + t2)
    w1 = p_grp * t2 / (1.0 + t2)

    j0 = i1 - f_lo
    j1 = i2 - f_lo
    first = j0 < j1
    ja = jnp.minimum(j0, j1)
    jb = jnp.maximum(j0, j1)
    wa = jnp.where(first, w0, w1)
    wb = jnp.where(first, w1, w0)
    n_pair = epg * (epg - 1.0) * 0.5
    cls = grp * n_pair + ja * (2.0 * epg - ja - 1.0) * 0.5 + (jb - ja - 1.0)

    sel = lane == cls
    oh = jnp.where(sel, 1.0, 0.0)
    before = carry_ref[...] + _dot_split_rows(tri_ref[...], oh.astype(bf16))
    rank = jnp.sum(jnp.where(sel, before, 0.0), axis=-1, keepdims=True)
    carry_ref[...] = carry_ref[...] + jnp.sum(oh, axis=0, keepdims=True)

    meta = jnp.zeros((tm, ROUTE_LANES), f32)
    for k, val in enumerate((cls, rank, wa, wb)):
        meta = jnp.where(lane == float(k), val, meta)

    wts = jnp.where(lane == 0.0, wa, jnp.where(lane == 1.0, wb, 0.0))
    assert half + LANES == ROW_PANELS * LANES
    row = jnp.concatenate([words, lax.bitcast_convert_type(wts, jnp.uint32)], axis=1)
    _store_panels(up_ref, row, tm, r0)
    metat_ref[:, rows] = meta.T[0:8, :]


def _slice_per_step(w, steps):
    e, r, c = w.shape
    k = max(1, steps // e)
    assert (e * k) % steps == 0 and r % k == 0
    return w.reshape(e * k, r // k, c), (e * k) // steps


def _post_call(x2, a3, b2, mod3, u2d, g2, wg, bg, wps, wpp, wout, wr, br, moe_w, seq, tm, ts,
               n_grp, n_exp):
    t, d = x2.shape
    d_ssm = wps.shape[0]
    nl = a3.shape[0]
    per = seq // tm
    tri = jnp.tril(jnp.ones((ts, ts), bf16), -1)
    moe_v = [_slice_per_step(w, t // tm) for w in moe_w]
    moe_specs = [pl.BlockSpec((n,) + v.shape[1:], lambda i: (i, 0, 0)) for v, n in moe_v]
    kern = functools.partial(_post_kernel, d=d, d_ssm=d_ssm, n_grp=n_grp, n_exp=n_exp,
                             tm=tm, ts=ts)
    const2 = lambda i: (0, 0)
    wspec = lambda w: pl.BlockSpec(w.shape, const2, pipeline_mode=pl.Buffered(1))
    return pl.pallas_call(
        kern,
        grid=(t // tm,),
        in_specs=[pl.BlockSpec((tm, d), lambda i: (i, 0)),
                  pl.BlockSpec((nl, tm, LANES), lambda i: (0, i, 0)),
                  pl.BlockSpec((tm, b2.shape[1]), lambda i: (i, 0)),
                  pl.BlockSpec((None, 1, mod3.shape[2]), lambda i: (i // per, 0, 0)),
                  pl.BlockSpec((tm, d), lambda i: (i, 0)),
                  pl.BlockSpec((1, d), const2),
                  wspec(wg),
                  pl.BlockSpec(bg.shape, const2),
                  wspec(wps),
                  wspec(wpp),
                  wspec(wout),
                  wspec(wr),
                  pl.BlockSpec(br.shape, const2),
                  wspec(tri)] + moe_specs,
        out_specs=[pl.BlockSpec((tm, d), lambda i: (i, 0)),
                   pl.BlockSpec((tm * ROW_PANELS, LANES), lambda i: (i, 0)),
                   pl.BlockSpec((8, tm), lambda i: (0, i)),
                   pl.BlockSpec((1, ROUTE_LANES), const2)] + moe_specs,
        out_shape=(jax.ShapeDtypeStruct((t, d), f32),
                   jax.ShapeDtypeStruct((t * ROW_PANELS, LANES), jnp.uint32),
                   jax.ShapeDtypeStruct((8, t), f32),
                   jax.ShapeDtypeStruct((1, ROUTE_LANES), f32))
        + tuple(jax.ShapeDtypeStruct(v.shape, bf16) for v, _ in moe_v),
        scratch_shapes=[pltpu.VMEM((1, ROUTE_LANES), f32)],
        compiler_params=_cparams(("arbitrary",)),
        name="post",
    )(x2, a3, b2, mod3, u2d, g2, wg, bg, wps, wpp, wout, wr, br, tri, *[v for v, _ in moe_v])


def _row_copy(src_ref, dst_ref, src_row, dst_row, p, sem):
    src = src_ref.at[pl.ds(pl.multiple_of(src_row * p, p), p)]
    dst = dst_ref.at[pl.ds(pl.multiple_of(dst_row * p, p), p)]
    return pltpu.make_async_copy(src, dst, sem)


def _wait_rows(src_ref, dst_ref, n_rows, p, sem):
    pltpu.make_async_copy(src_ref.at[pl.ds(0, n_rows * p)], dst_ref.at[pl.ds(0, n_rows * p)],
                          sem).wait()


def _scatter_kernel(pos_ref, up_ref, xs_ref, sem, *, tm, p):
    def issue(g, c):
        r0 = g * ROW_DMA_UNROLL
        idx = [pos_ref[0, 0, r0 + u] for u in range(ROW_DMA_UNROLL)]
        for u in range(ROW_DMA_UNROLL):
            _row_copy(up_ref, xs_ref, r0 + u, idx[u], p, sem).start(priority=u % 2)
        return c

    lax.fori_loop(0, tm // ROW_DMA_UNROLL, issue, 0)
    _wait_rows(up_ref, xs_ref, tm, p, sem)


def _scatter_call(up, pos3, n_rows, tm, p):
    n_steps = up.shape[0] // (tm * p)
    return pl.pallas_call(
        functools.partial(_scatter_kernel, tm=tm, p=p),
        grid=(n_steps,),
        in_specs=[pl.BlockSpec((1, 1, tm), lambda i: (i, 0, 0), memory_space=pltpu.SMEM),
                  pl.BlockSpec((tm * p, LANES), lambda i: (i, 0))],
        out_specs=pl.BlockSpec(memory_space=pl.ANY),
        out_shape=jax.ShapeDtypeStruct((n_rows * p, LANES), up.dtype),
        scratch_shapes=[pltpu.SemaphoreType.DMA(())],
        compiler_params=_cparams(("arbitrary",)),
        name="scatter_rows",
    )(pos3, up)


def _experts_kernel(ea_ref, eb_ref, first_ref, nt_ref, xs_ref, wg_a_ref, wg_b_ref, wu_a_ref,
                    wu_b_ref, wd_a_ref, wd_b_ref, ys_ref, xbuf, ybuf, sem_in, sem_out,
                    *, d, tr):
    c = pl.program_id(0)
    n_cls = pl.num_programs(0)
    t0 = first_ref[c]
    t1 = t0 + nt_ref[c]
    total = first_ref[n_cls - 1] + nt_ref[n_cls - 1]
    px = d // 2 // LANES
    py = px
    rin, rout = tr * ROW_PANELS, tr * py

    def in_copy(g):
        src = xs_ref.at[pl.ds(pl.multiple_of(g * rin, rin), rin)]
        return pltpu.make_async_copy(src, xbuf.at[g % 2], sem_in.at[g % 2])

    def out_copy(g):
        dst = ys_ref.at[pl.ds(pl.multiple_of(g * rout, rout), rout)]
        return pltpu.make_async_copy(ybuf.at[g % 2], dst, sem_out.at[g % 2])

    pl.when((c == 0) & (total > 0))(lambda: in_copy(0).start())

    def tile(g, carry):
        slot = g % 2
        in_copy(g).wait()
        pl.when(g + 1 < total)(lambda: in_copy(g + 1).start())
        pl.when(g >= 2)(lambda: out_copy(g - 2).wait())

        xt = xbuf.at[slot]
        w = jnp.concatenate([xt[pl.ds(k, tr, stride=ROW_PANELS), :] for k in range(px)], axis=1)
        wts = lax.bitcast_convert_type(xt[pl.ds(px, tr, stride=ROW_PANELS), :], f32)
        x = _unpack_bf16_pairs(w).astype(bf16)

        def ffn(wg_ref, wu_ref, wd_ref, wt):
            h1 = _dot(x, wg_ref[...])
            act = h1 * _sigmoid(h1) * _dot(x, wu_ref[...])
            return wt * _dot(act.astype(bf16), wd_ref[...])

        y = (ffn(wg_a_ref, wu_a_ref, wd_a_ref, wts[:, 0:1])
             + ffn(wg_b_ref, wu_b_ref, wd_b_ref, wts[:, 1:2]))
        _store_panels(ybuf.at[slot], _pack_bf16_pairs(y), tr)
        out_copy(g).start()
        return carry

    lax.fori_loop(t0, t1, tile, 0)

    @pl.when(c == n_cls - 1)
    def _():
        pl.when(total >= 2)(lambda: out_copy(total - 2).wait())
        pl.when(total >= 1)(lambda: out_copy(total - 1).wait())


def _experts_call(cls_ea, cls_eb, cls_first, cls_nt, xs, wg, wu, wd, tr):
    n_exp, d, d_exp = wg.shape
    n_cls = cls_ea.shape[0]
    n_rows = xs.shape[0] // ROW_PANELS
    py = d // 2 // LANES

    def wa_map(c, ea, eb, first, nt):
        return (ea[c], 0, 0)

    def wb_map(c, ea, eb, first, nt):
        return (eb[c], 0, 0)

    grid_spec = pltpu.PrefetchScalarGridSpec(
        num_scalar_prefetch=4,
        grid=(n_cls,),
        in_specs=[pl.BlockSpec(memory_space=pl.ANY),
                  pl.BlockSpec((None, d, d_exp), wa_map),
                  pl.BlockSpec((None, d, d_exp), wb_map),
                  pl.BlockSpec((None, d, d_exp), wa_map),
                  pl.BlockSpec((None, d, d_exp), wb_map),
                  pl.BlockSpec((None, d_exp, d), wa_map),
                  pl.BlockSpec((None, d_exp, d), wb_map)],
        out_specs=pl.BlockSpec(memory_space=pl.ANY),
        scratch_shapes=[pltpu.VMEM((2, tr * ROW_PANELS, LANES), jnp.uint32),
                        pltpu.VMEM((2, tr * py, LANES), jnp.uint32),
                        pltpu.SemaphoreType.DMA((2,)),
                        pltpu.SemaphoreType.DMA((2,))],
    )
    return pl.pallas_call(
        functools.partial(_experts_kernel, d=d, tr=tr),
        grid_spec=grid_spec,
        out_shape=jax.ShapeDtypeStruct((n_rows * py, LANES), jnp.uint32),
        compiler_params=_cparams(("arbitrary",)),
        name="experts",
    )(cls_ea, cls_eb, cls_first, cls_nt, xs, wg, wg, wu, wu, wd, wd)


def _combine_kernel(pos_ref, posn_ref, h_ref, mod_ref, gf_ref, ys_ref, o_ref,
                    g_ref, sem, *, d, tm, n_steps, final_norm):
    i = pl.program_id(0)
    slot = i % 2
    py = d // 2 // LANES

    def issue(p_ref, s):
        def body(g, c):
            r0 = g * ROW_DMA_UNROLL
            idx = [p_ref[0, 0, r0 + u] for u in range(ROW_DMA_UNROLL)]
            for u in range(ROW_DMA_UNROLL):
                _row_copy(ys_ref, g_ref.at[s], idx[u], r0 + u, py,
                          sem.at[s]).start(priority=u % 2)
            return c

        lax.fori_loop(0, tm // ROW_DMA_UNROLL, body, 0)

    pl.when(i == 0)(lambda: issue(pos_ref, 0))
    pl.when(i + 1 < n_steps)(lambda: issue(posn_ref, 1 - slot))
    _wait_rows(ys_ref, g_ref.at[slot], tm, py, sem.at[slot])

    gate2 = mod_ref[:, 5 * d:6 * d]
    h = h_ref[...] + gate2 * _unpack_bf16_pairs(_load_panels(g_ref.at[slot], tm, py))
    if final_norm:
        ms = jnp.mean(h * h, axis=-1, keepdims=True)
        h = h * lax.rsqrt(ms + EPS) * gf_ref[...]
    o_ref[...] = h


def _combine_call(pos3, h1, mod3, gf, ys, seq, tm, final_norm):
    t, d = h1.shape
    per = seq // tm
    n_steps = t // tm
    pos_spec = lambda f: pl.BlockSpec((1, 1, tm), f, memory_space=pltpu.SMEM)
    return pl.pallas_call(
        functools.partial(_combine_kernel, d=d, tm=tm, n_steps=n_steps, final_norm=final_norm),
        grid=(n_steps,),
        in_specs=[pos_spec(lambda i: (i, 0, 0)),
                  pos_spec(lambda i: (jnp.minimum(i + 1, n_steps - 1), 0, 0)),
                  pl.BlockSpec((tm, d), lambda i: (i, 0)),
                  pl.BlockSpec((None, 1, mod3.shape[2]), lambda i: (i // per, 0, 0)),
                  pl.BlockSpec((1, d), lambda i: (0, 0)),
                  pl.BlockSpec(memory_space=pl.ANY)],
        out_specs=pl.BlockSpec((tm, d), lambda i: (i, 0)),
        out_shape=jax.ShapeDtypeStruct((t, d), f32),
        scratch_shapes=[pltpu.VMEM((2, tm * (d // 2 // LANES), LANES), jnp.uint32),
                        pltpu.SemaphoreType.DMA((2,))],
        compiler_params=_cparams(("arbitrary",)),
        name="combine",
    )(pos3, pos3, h1, mod3, gf, ys)


def _row_tile(seq, target):
    tm = min(target, seq)
    assert seq % tm == 0 and tm % 8 == 0
    return tm


def _layer(h2, mod3, p, bsz, seq, final_gain, final_norm):
    t, d = h2.shape
    g, n_p, n_h = p["ssm_b_re"].shape
    d_ssm = g * n_h
    d_pool = p["pool_scale"].shape[-1]
    n_grp = p["router_coarse_w"].shape[-1]
    n_exp = p["router_fine_w"].shape[-1]
    assert n_grp + n_exp <= ROUTE_LANES // 2 and seq % SSM_SLABS == 0
    tm = _row_tile(seq, 512)

    swap = lambda v: jnp.swapaxes(v, 1, 2)
    pw, bbd, cbd = _ssm_prep_call(
        p["ssm_lam_re"], p["ssm_lam_im"], p["ssm_log_dt"], swap(p["ssm_b_re"]),
        swap(p["ssm_b_im"]), swap(p["ssm_c_re"]), swap(p["ssm_c_im"]), SSM_SLABS)
    bbd, cbd = bbd.astype(bf16), cbd.astype(bf16)
    pw2 = pw.reshape(SSM_SLABS, 2 * g * n_p)

    w_in, b_in = p["w_in"], p["b_in"]
    n_sp = d_ssm + d_pool
    a3, b2, u2d = _mixers_call(h2, mod3, p["norm1_g"].reshape(1, d), w_in[:, :n_sp].astype(bf16),
                          b_in[:n_sp].reshape(1, n_sp), bbd, cbd, pw2,
                          p["ssm_d"].reshape(1, d_ssm),
                          p["ssm_w_glu"].astype(bf16), p["ssm_b_glu"].reshape(1, d_ssm),
                          p["pool_w"].astype(bf16), p["pool_scale"].reshape(1, d_pool),
                          bsz, seq, tm)

    wr = jnp.concatenate([p["router_coarse_w"], p["router_fine_w"]], axis=1)
    wr_hi = wr.astype(bf16)
    wr_lo = (wr - wr_hi.astype(f32)).astype(bf16)
    padc = ROUTE_LANES // 2 - wr.shape[1]
    wr_cat = jnp.concatenate([jnp.pad(wr_hi, ((0, 0), (0, padc))),
                              jnp.pad(wr_lo, ((0, 0), (0, padc)))], axis=1)
    br = jnp.pad(jnp.concatenate([p["router_coarse_b"], p["router_fine_b"]]),
                 (0, ROUTE_LANES - wr.shape[1])).reshape(1, ROUTE_LANES)

    moe_w = (p["moe_w_gate"], p["moe_w_up"], p["moe_w_down"])
    h1, up, metat, cnt, *moe_bf = _post_call(
        h2, a3, b2, mod3, u2d, p["norm2_g"].reshape(1, d),
        w_in[:, n_sp:].astype(bf16), b_in[n_sp:].reshape(1, 2 * d),
        p["w_proj_ssm"].astype(bf16), p["w_proj_pool"].astype(bf16), p["w_out"].astype(bf16),
        wr_cat, br, moe_w, seq, _row_tile(seq, 2 * tm), _row_tile(seq, 2 * tm), n_grp, n_exp)
    moe_bf = [v.reshape(w.shape) for v, w in zip(moe_bf, moe_w)]

    tr = EXPERT_TILE_ROWS
    epg = n_exp // n_grp
    pairs = [(a, b) for a in range(epg) for b in range(a + 1, epg)]
    n_cls = n_grp * len(pairs)
    assert n_cls <= ROUTE_LANES
    cls_ea = jnp.asarray([gi * epg + a for gi in range(n_grp) for a, _ in pairs], jnp.int32)
    cls_eb = jnp.asarray([gi * epg + b for gi in range(n_grp) for _, b in pairs], jnp.int32)
    counts = cnt[0, 0:n_cls].astype(jnp.int32)
    padded = ((counts + tr - 1) // tr) * tr
    ends = jnp.cumsum(padded)
    offs = ends - padded
    ids = jnp.arange(n_cls, dtype=jnp.int32)
    cls = metat[0].astype(jnp.int32)
    off = jnp.sum(jnp.where(cls[None, :] == ids[:, None], offs[:, None], 0), axis=0)
    pos = off + metat[1].astype(jnp.int32)
    tms = _row_tile(seq, 2 * tm)
    pos3 = pos.reshape(t // tms, 1, tms)
    nt_max = t // tr + n_cls

    xs = _scatter_call(up, pos3, nt_max * tr, tms, ROW_PANELS)
    ys = _experts_call(cls_ea, cls_eb, offs // tr, padded // tr, xs, *moe_bf, tr)
    return _combine_call(pos.reshape(t // tm, 1, tm), h1, mod3, final_gain.reshape(1, d), ys,
                         seq, tm, final_norm)


def kernel(x, c, w_mod, b_mod, norm1_g, w_in, b_in, ssm_lam_re, ssm_lam_im, ssm_log_dt, ssm_b_re, ssm_b_im, ssm_c_re, ssm_c_im, ssm_d, ssm_w_glu, ssm_b_glu, pool_w, pool_scale, w_proj_ssm, w_proj_pool, w_out, norm2_g, router_coarse_w, router_coarse_b, router_fine_w, router_fine_b, moe_w_gate, moe_w_up, moe_w_down, norm_f_g):
    bsz, seq, d = x.shape
    depth = w_mod.shape[0]
    per_layer = dict(
        norm1_g=norm1_g, w_in=w_in, b_in=b_in, ssm_lam_re=ssm_lam_re, ssm_lam_im=ssm_lam_im,
        ssm_log_dt=ssm_log_dt, ssm_b_re=ssm_b_re, ssm_b_im=ssm_b_im, ssm_c_re=ssm_c_re,
        ssm_c_im=ssm_c_im, ssm_d=ssm_d, ssm_w_glu=ssm_w_glu, ssm_b_glu=ssm_b_glu, pool_w=pool_w,
        pool_scale=pool_scale, w_proj_ssm=w_proj_ssm, w_proj_pool=w_proj_pool, w_out=w_out,
        norm2_g=norm2_g, router_coarse_w=router_coarse_w, router_coarse_b=router_coarse_b,
        router_fine_w=router_fine_w, router_fine_b=router_fine_b, moe_w_gate=moe_w_gate,
        moe_w_up=moe_w_up, moe_w_down=moe_w_down)
    h2 = x.reshape(bsz * seq, d)
    for l in range(depth):
        p = {k: v[l] for k, v in per_layer.items()}
        mod3 = _mod_call(c, w_mod[l], b_mod[l]).reshape(bsz, 1, N_MOD * d)
        h2 = _layer(h2, mod3, p, bsz, seq, norm_f_g, final_norm=(l == depth - 1))
    return h2.reshape(bsz, seq, d)
```
